```python
import math
import jax, jax.numpy as jnp
from jax import lax
import numpy as np

D_MODEL = 1024
BATCH = 8
SEQ = 8192
DEPTH = 1
DEC_BATCH = 4
DEC_SEQ = 4096
PAST_LEN = 128

N_META = 16
M_HEADS = 4
M_DK = 128
M_DV = 128
M_WIDTH = M_HEADS * M_DV
CHUNK = 64
A_HEADS = 8
A_KV_HEADS = 2
A_GROUP = A_HEADS // A_KV_HEADS
A_DH = 64
A_WIDTH = A_HEADS * A_DH
WINDOW = 128
BLOCK = 128
N_BUCKETS = 32
MAX_DIST = 128
D_FF = 2816
EPS = 1e-6
NEG = -1e30

SPLITS = [M_HEADS * M_DK, M_HEADS * M_DK, M_WIDTH, M_WIDTH, 4 * M_HEADS,
          A_WIDTH, A_KV_HEADS * A_DH, A_KV_HEADS * A_DH, D_MODEL, D_MODEL]
D_IN = sum(SPLITS)

kernel_name = 'hybrid_mlstm_swa_macaron_encoder'


def rms_norm(x, g):
    xf = x.astype(jnp.float32)
    y = xf * lax.rsqrt(jnp.mean(xf * xf, axis=-1, keepdims=True) + EPS)
    return (y * g.astype(jnp.float32)).astype(x.dtype)


def swiglu(x, w_gu, w_down):
    g, u = jnp.split(x @ w_gu, 2, axis=-1)
    return (jax.nn.silu(g) * u) @ w_down


def t5_bucket(rel):
    nb = N_BUCKETS // 2
    max_exact = nb // 2
    ret = jnp.where(rel > 0, nb, 0)
    n = jnp.abs(rel)
    nf = jnp.maximum(n, 1).astype(jnp.float32)
    large = max_exact + (jnp.log(nf / max_exact) / math.log(MAX_DIST / max_exact)
                         * (nb - max_exact)).astype(jnp.int32)
    large = jnp.minimum(large, nb - 1)
    return ret + jnp.where(n < max_exact, n, large)


def mlstm_chunkwise(q, k, v, log_i, log_f):
    B, H, T, DK = q.shape
    NC = T // CHUNK
    q = q.reshape(B, H, NC, CHUNK, DK) * (DK ** -0.5)
    k = k.reshape(B, H, NC, CHUNK, DK)
    v = v.reshape(B, H, NC, CHUNK, -1)
    li = log_i.reshape(B, H, NC, CHUNK)
    b = jnp.cumsum(log_f.reshape(B, H, NC, CHUNK), axis=-1)
    b_last = b[..., -1]
    a = b_last[..., None] - b + li
    m_loc = jnp.max(a, axis=-1)
    w = jnp.exp(a - m_loc[..., None])
    C_loc = jnp.einsum('bhnc,bhncv,bhnck->bhnvk', w, v, k)
    n_loc = jnp.einsum('bhnc,bhnck->bhnk', w, k)

    def step(carry, inp):
        C, n, m = carry
        Cl, nl, ml, bl = inp
        m_new = jnp.maximum(bl + m, ml)
        s_old = jnp.exp(bl + m - m_new)
        s_loc = jnp.exp(ml - m_new)
        C_new = s_old[..., None, None] * C + s_loc[..., None, None] * Cl
        n_new = s_old[..., None] * n + s_loc[..., None] * nl
        return (C_new, n_new, m_new), (C, n, m)

    DV = v.shape[-1]
    init = (jnp.zeros((B, H, DV, DK), jnp.float32), jnp.zeros((B, H, DK), jnp.float32),
            jnp.zeros((B, H), jnp.float32))
    xs = (jnp.moveaxis(C_loc, 2, 0), jnp.moveaxis(n_loc, 2, 0),
          jnp.moveaxis(m_loc, 2, 0), jnp.moveaxis(b_last, 2, 0))
    _, (C0, n0, m0) = lax.scan(step, init, xs)
    C0 = jnp.moveaxis(C0, 0, 2)
    n0 = jnp.moveaxis(n0, 0, 2)
    m0 = jnp.moveaxis(m0, 0, 2)

    lower = jnp.tril(jnp.ones((CHUNK, CHUNK), dtype=bool))
    d = b[..., :, None] - b[..., None, :] + li[..., None, :]
    d = jnp.where(lower, d, NEG)
    m_inter = b + m0[..., None]
    m_t = jnp.maximum(m_inter, jnp.max(d, axis=-1))
    s = jnp.einsum('bhntk,bhnsk->bhnts', q, k) * jnp.exp(d - m_t[..., None])
    s_inter = jnp.exp(m_inter - m_t)
    num = (jnp.einsum('bhnts,bhnsv->bhntv', s, v)
           + s_inter[..., None] * jnp.einsum('bhnvk,bhntk->bhntv', C0, q))
    den = jnp.sum(s, axis=-1) + s_inter * jnp.einsum('bhnk,bhntk->bhnt', n0, q)
    h = num / jnp.maximum(jnp.abs(den), jnp.exp(-m_t))[..., None]
    return h.reshape(B, H, T, DV)


def mlstm_mixer(q, k, v, o, gates, out_gain):
    B, L, _ = q.shape
    pad = (-L) % CHUNK
    f32 = jnp.float32

    def heads(t, dh):
        t = t.astype(f32).reshape(B, L, M_HEADS, dh).transpose(0, 2, 1, 3)
        return jnp.pad(t, ((0, 0), (0, 0), (pad, 0), (0, 0)))

    qh, kh, vh = heads(q, M_DK), heads(k, M_DK), heads(v, M_DV)
    g = jnp.pad(gates.astype(f32).transpose(0, 2, 1), ((0, 0), (0, 0), (pad, 0)))
    i_f, f_f, i_b, f_b = jnp.split(g, 4, axis=1)
    valid = jnp.arange(L + pad) >= pad

    def run(qd, kd, vd, i_pre, f_pre, vmask):
        log_i = jnp.where(vmask, i_pre, NEG)
        log_f = jnp.where(vmask, jax.nn.log_sigmoid(f_pre), 0.0)
        return mlstm_chunkwise(qd, kd, vd, log_i, log_f)

    flip = lambda t: jnp.flip(t, axis=2)
    h = (run(qh, kh, vh, i_f, f_f, valid)
         + flip(run(flip(qh), flip(kh), flip(vh), flip(i_b), flip(f_b), valid[::-1])))
    h = h[:, :, pad:].transpose(0, 2, 1, 3)
    h = rms_norm(h, out_gain.reshape(M_HEADS, M_DV))
    return (h.reshape(B, L, M_WIDTH) * jax.nn.sigmoid(o.astype(f32))).astype(q.dtype)


def softmax_with_sink(logits, sink):
    col = jnp.broadcast_to(sink[:, :, None, None], logits.shape[:-1] + (1,))
    p = jax.nn.softmax(jnp.concatenate([logits, col], axis=-1), axis=-1)
    return p[..., :-1]


def window_attention(q, k, v, q_gain, k_gain, sink, rel_bias):
    B, L, _ = q.shape
    S = L - N_META
    NB = S // BLOCK
    f32 = jnp.float32
    q = rms_norm(q.astype(f32).reshape(B, L, A_HEADS, A_DH), q_gain) * (A_DH ** -0.5)
    q = q.reshape(B, L, A_KV_HEADS, A_GROUP, A_DH)
    k = rms_norm(k.astype(f32).reshape(B, L, A_KV_HEADS, A_DH), k_gain)
    v = v.astype(f32).reshape(B, L, A_KV_HEADS, A_DH)
    sink = sink.astype(f32).reshape(A_KV_HEADS, A_GROUP)
    table = rel_bias.astype(f32)

    qm = q[:, :N_META]
    kq, vq = k[:, :N_META + BLOCK], v[:, :N_META + BLOCK]
    rel_mq = jnp.arange(N_META + BLOCK)[None, :] - jnp.arange(N_META)[:, None]
    bias_mq = table[t5_bucket(rel_mq)].reshape(N_META, N_META + BLOCK, A_KV_HEADS, A_GROUP)
    s_mq = jnp.einsum('bqkgd,bskd->bkgqs', qm, kq) + bias_mq.transpose(2, 3, 0, 1)
    s_mq = jnp.where(jnp.abs(rel_mq) <= WINDOW, s_mq, NEG)
    o_meta = jnp.einsum('bkgqs,bskd->bqkgd', softmax_with_sink(s_mq, sink), vq)
    o_meta = o_meta.reshape(B, N_META, A_WIDTH)

    km, vm = k[:, :N_META], v[:, :N_META]
    qb = q[:, N_META:].reshape(B, NB, BLOCK, A_KV_HEADS, A_GROUP, A_DH)
    padw = ((0, 0), (BLOCK, BLOCK), (0, 0), (0, 0))
    kp = jnp.pad(k[:, N_META:], padw).reshape(B, NB + 2, BLOCK, A_KV_HEADS, A_DH)
    vp = jnp.pad(v[:, N_META:], padw).reshape(B, NB + 2, BLOCK, A_KV_HEADS, A_DH)
    kb = jnp.concatenate([kp[:, :-2], kp[:, 1:-1], kp[:, 2:]], axis=2)
    vb = jnp.concatenate([vp[:, :-2], vp[:, 1:-1], vp[:, 2:]], axis=2)

    t_off = jnp.arange(BLOCK)
    s_off = jnp.arange(3 * BLOCK) - BLOCK
    rel_band = s_off[None, :] - t_off[:, None]
    key_idx = jnp.arange(NB)[:, None] * BLOCK + s_off[None, :]
    mask = ((jnp.abs(rel_band) <= WINDOW)[None]
            & ((key_idx >= 0) & (key_idx < S))[:, None, :])
    bias_band = table[t5_bucket(rel_band)].reshape(BLOCK, 3 * BLOCK, A_KV_HEADS, A_GROUP)
    s_band = jnp.einsum('bntkgd,bnskd->bnkgts', qb, kb) + bias_band.transpose(2, 3, 0, 1)
    s_band = jnp.where(mask[None, :, None, None], s_band, NEG)

    q_pos = N_META + jnp.arange(S).reshape(NB, BLOCK)
    rel_meta = jnp.arange(N_META)[None, None, :] - q_pos[..., None]
    bias_meta = table[t5_bucket(rel_meta)].reshape(NB, BLOCK, N_META, A_KV_HEADS, A_GROUP)
    s_meta = (jnp.einsum('bntkgd,bmkd->bnkgtm', qb, km)
              + bias_meta.transpose(0, 3, 4, 1, 2)[None])

    p = softmax_with_sink(jnp.concatenate([s_meta, s_band], axis=-1), sink)
    o_real = (jnp.einsum('bnkgtm,bmkd->bntkgd', p[..., :N_META], vm)
              + jnp.einsum('bnkgts,bnskd->bntkgd', p[..., N_META:], vb))
    o_real = o_real.reshape(B, S, A_WIDTH)
    return jnp.concatenate([o_meta, o_real], axis=1)


def encoder_layer(x, rel_bias, norm_ffn1, w_ffn1_gu, w_ffn1_down, norm_mix, w_in, b_gates,
                  m_out_gain, q_norm_gain, k_norm_gain, sink_logits, w_up_m, w_up_a, w_out,
                  norm_ffn2, w_ffn2_gu, w_ffn2_down):
    x = x + 0.5 * swiglu(rms_norm(x, norm_ffn1), w_ffn1_gu, w_ffn1_down)
    h = rms_norm(x, norm_mix)
    cuts = [int(c) for c in np.cumsum(SPLITS)[:-1]]
    qm, km, vm, om, gm, qa, ka, va, ga, gb = jnp.split(h @ w_in, cuts, axis=-1)
    y_m = mlstm_mixer(qm, km, vm, om, gm + b_gates, m_out_gain) @ w_up_m
    y_a = window_attention(qa, ka, va, q_norm_gain, k_norm_gain, sink_logits, rel_bias).astype(x.dtype) @ w_up_a
    mixed = jax.nn.sigmoid(ga) * y_m + jax.nn.sigmoid(gb) * y_a
    x = x + mixed @ w_out
    x = x + 0.5 * swiglu(rms_norm(x, norm_ffn2), w_ffn2_gu, w_ffn2_down)
    return x


def encode(x, meta_tokens, rel_bias, norm_ffn1, w_ffn1_gu, w_ffn1_down, norm_mix, w_in, b_gates,
           m_out_gain, q_norm_gain, k_norm_gain, sink_logits, w_up_m, w_up_a, w_out,
           norm_ffn2, w_ffn2_gu, w_ffn2_down):
    B = x.shape[0]
    meta = jnp.broadcast_to(meta_tokens.astype(x.dtype)[None], (B, N_META, D_MODEL))
    x = jnp.concatenate([meta, x], axis=1)
    for l in range(DEPTH):
        x = encoder_layer(x, rel_bias, norm_ffn1[l], w_ffn1_gu[l], w_ffn1_down[l], norm_mix[l],
                          w_in[l], b_gates[l], m_out_gain[l], q_norm_gain[l], k_norm_gain[l],
                          sink_logits[l], w_up_m[l], w_up_a[l], w_out[l], norm_ffn2[l],
                          w_ffn2_gu[l], w_ffn2_down[l])
    return x[:, N_META:]


def setup_inputs(seed: int = 0) -> dict:
    key = jax.random.key(seed)
    ks = jax.random.split(key, 24)
    f32 = jnp.float32
    nrm = lambda kk, shape, scale: jax.random.normal(kk, shape, f32) * scale
    gain = lambda kk, shape: 1.0 + 0.05 * jax.random.normal(kk, shape, f32)
    f_bias = jnp.linspace(3.0, 6.0, M_HEADS, dtype=f32)
    gb = 0.1 * jax.random.normal(ks[10], (DEPTH, 4, M_HEADS), f32)
    b_gates = (gb + jnp.stack([jnp.zeros((M_HEADS,), f32), f_bias,
                               jnp.zeros((M_HEADS,), f32), f_bias])[None]).reshape(DEPTH, 4 * M_HEADS)
    return {
        'x_prompt': jax.random.normal(ks[0], (BATCH, SEQ, D_MODEL), f32),
        'x_sample': jax.random.normal(ks[1], (DEC_BATCH, DEC_SEQ, D_MODEL), f32),
        'meta_tokens': nrm(ks[2], (N_META, D_MODEL), 1.0),
        'rel_bias': nrm(ks[3], (N_BUCKETS, A_HEADS), 0.5),
        'norm_ffn1': gain(ks[4], (DEPTH, D_MODEL)),
        'w_ffn1_gu': nrm(ks[5], (DEPTH, D_MODEL, 2 * D_FF), D_MODEL ** -0.5),
        'w_ffn1_down': nrm(ks[6], (DEPTH, D_FF, D_MODEL), D_FF ** -0.5),
        'norm_mix': gain(ks[7], (DEPTH, D_MODEL)),
        'w_in': nrm(ks[8], (DEPTH, D_MODEL, D_IN), D_MODEL ** -0.5),
        'b_gates': b_gates,
        'm_out_gain': gain(ks[11], (DEPTH, M_WIDTH)),
        'q_norm_gain': gain(ks[12], (DEPTH, A_DH)),
        'k_norm_gain': gain(ks[13], (DEPTH, A_DH)),
        'sink_logits': nrm(ks[14], (DEPTH, A_HEADS), 0.5),
        'w_up_m': nrm(ks[15], (DEPTH, M_WIDTH, D_MODEL), M_WIDTH ** -0.5),
        'w_up_a': nrm(ks[16], (DEPTH, A_WIDTH, D_MODEL), A_WIDTH ** -0.5),
        'w_out': nrm(ks[17], (DEPTH, D_MODEL, D_MODEL), D_MODEL ** -0.5),
        'norm_ffn2': gain(ks[18], (DEPTH, D_MODEL)),
        'w_ffn2_gu': nrm(ks[19], (DEPTH, D_MODEL, 2 * D_FF), D_MODEL ** -0.5),
        'w_ffn2_down': nrm(ks[20], (DEPTH, D_FF, D_MODEL), D_FF ** -0.5),
    }


def reference(x_prompt, x_sample, meta_tokens, rel_bias, norm_ffn1, w_ffn1_gu, w_ffn1_down,
              norm_mix, w_in, b_gates, m_out_gain, q_norm_gain, k_norm_gain, sink_logits,
              w_up_m, w_up_a, w_out, norm_ffn2, w_ffn2_gu, w_ffn2_down):
    y_prompt = encode(x_prompt, meta_tokens, rel_bias, norm_ffn1, w_ffn1_gu, w_ffn1_down,
                      norm_mix, w_in, b_gates, m_out_gain, q_norm_gain, k_norm_gain,
                      sink_logits, w_up_m, w_up_a, w_out, norm_ffn2, w_ffn2_gu, w_ffn2_down)
    y_sample = encode(x_sample, meta_tokens, rel_bias, norm_ffn1, w_ffn1_gu, w_ffn1_down,
                      norm_mix, w_in, b_gates, m_out_gain, q_norm_gain, k_norm_gain,
                      sink_logits, w_up_m, w_up_a, w_out, norm_ffn2, w_ffn2_gu, w_ffn2_down)
    return (y_prompt, y_sample)
```

```python
import functools
import math

import numpy as np
import jax
import jax.numpy as jnp
from jax import lax
from jax.experimental import pallas as pl
from jax.experimental.pallas import tpu as pltpu

F32 = jnp.float32
BF16 = jnp.bfloat16

D_MODEL = 1024
D_FF = 2816
N_META = 16
M_HEADS = 4
M_DK = 128
M_DV = 128
M_WIDTH = M_HEADS * M_DV
A_HEADS = 8
A_KV_HEADS = 2
A_GROUP = A_HEADS // A_KV_HEADS
A_DH = 64
A_WIDTH = A_HEADS * A_DH
WINDOW = 128
BLOCK = 128
N_BUCKETS = 32
MAX_DIST = 128
EPS = 1e-6
NEG = -1e30

LANES = 128
MXU_DIM = 256
VMEM_LIMIT_BYTES = 58 * 1024 * 1024

FF_CHUNK = MXU_DIM
N_FF_CHUNKS = D_FF // FF_CHUNK
TOKEN_TILE = 512
M_CHUNK = 256
META_PAD = 128
N_KEYS = 3 * BLOCK + N_META + 16

_NN = (((1,), (0,)), ((), ()))
_NT = (((1,), (1,)), ((), ()))


def _dot(a, b, dims=_NN):
    return lax.dot_general(a, b, dims, preferred_element_type=F32)


def _split_dot_rhs(a_bf16, x_f32):
    hi = x_f32.astype(BF16)
    lo = (x_f32 - hi.astype(F32)).astype(BF16)
    return _dot(a_bf16, hi) + _dot(a_bf16, lo)


def _split_dot_lhs(x_f32, a_bf16):
    hi = x_f32.astype(BF16)
    lo = (x_f32 - hi.astype(F32)).astype(BF16)
    return _dot(hi, a_bf16) + _dot(lo, a_bf16)


def _rms(x, g):
    ms = jnp.mean(x * x, axis=-1, keepdims=True)
    return x * lax.rsqrt(ms + EPS) * g


def _sigmoid(x):
    return 1.0 / (1.0 + jnp.exp(-x))


def _log_sigmoid(x):
    return jnp.minimum(x, 0.0) - jnp.log1p(jnp.exp(-jnp.abs(x)))


def _const_spec(shape):
    nd = len(shape)
    return pl.BlockSpec(shape, lambda *_: (0,) * nd, pipeline_mode=pl.Buffered(1))


def _ffn_residual(x, nf_ref, wgu_ref, wd_ref, a_scr):
    xn = _rms(x, nf_ref[...]).astype(BF16)
    for j in range(N_FF_CHUNKS):
        gu = _dot(xn, wgu_ref[j])
        g = gu[:, :FF_CHUNK]
        u = gu[:, FF_CHUNK:]
        a_scr[:, j * FF_CHUNK:(j + 1) * FF_CHUNK] = (g * _sigmoid(g) * u).astype(BF16)
    y = _dot(a_scr[...], wd_ref[...])
    return x + 0.5 * y


def _group_sumsq(x, gsum_ref):
    sq = x * x
    hi = sq.astype(BF16)
    lo = (sq - hi.astype(F32)).astype(BF16)
    return _dot(jnp.concatenate([hi, lo], axis=1), gsum_ref[...])


def _stage_a_kernel(x_ref, nf_ref, wgu_ref, wd_ref, nm_ref, wqkv_ref, wg_ref, bg_ref, wqa_ref,
                    wkv_ref, qg_ref, kg_ref, gsum_ref,
                    x1_ref, qkv_ref, gate_ref, qa_ref, kva_ref, a_scr):
    x1 = _ffn_residual(x_ref[...], nf_ref, wgu_ref, wd_ref, a_scr)
    x1_ref[...] = x1
    hn = _rms(x1, nm_ref[...]).astype(BF16)
    qkv_ref[:, :M_WIDTH] = (_dot(hn, wqkv_ref[:, :M_WIDTH]) * (M_DK ** -0.5)).astype(BF16)
    qkv_ref[:, M_WIDTH:] = _dot(hn, wqkv_ref[:, M_WIDTH:]).astype(BF16)
    gates = _dot(hn, wg_ref[...]) + bg_ref[...]
    gate_ref[...] = gates[:, :4 * M_HEADS]
    for t in range(A_WIDTH // LANES):
        sl = slice(t * LANES, (t + 1) * LANES)
        q = _dot(hn, wqa_ref[:, sl])
        ss = _group_sumsq(q, gsum_ref)
        qn = q * lax.rsqrt(ss * (1.0 / A_DH) + EPS) * qg_ref[:, sl] * (A_DH ** -0.5)
        qa_ref[:, sl] = qn.astype(BF16)
    for t in range(A_KV_HEADS):
        sl = slice(t * LANES, (t + 1) * LANES)
        k = _dot(hn, wkv_ref[:, sl])
        ss = _group_sumsq(k, gsum_ref)
        kva_ref[:, sl] = (k * lax.rsqrt(ss * (1.0 / A_DH) + EPS) * kg_ref[...]).astype(BF16)
    vsl = slice(A_KV_HEADS * LANES, 2 * A_KV_HEADS * LANES)
    kva_ref[:, vsl] = _dot(hn, wkv_ref[:, vsl]).astype(BF16)


def _stage_a(x2d, w, tile):
    n = x2d.shape[0]
    assert n % tile == 0
    row = lambda width: pl.BlockSpec((tile, width), lambda i: (i, 0))
    out_shape = (
        jax.ShapeDtypeStruct((n, D_MODEL), F32),
        jax.ShapeDtypeStruct((n, 3 * M_WIDTH), BF16),
        jax.ShapeDtypeStruct((n, 4 * M_HEADS), F32),
        jax.ShapeDtypeStruct((n, A_WIDTH), BF16),
        jax.ShapeDtypeStruct((n, 4 * LANES), BF16),
    )
    consts = (w['nf1'], w['wgu1'], w['wd1'], w['nm'], w['wqkv'], w['wg'], w['bg'], w['wqa'],
              w['wkv'], w['qg'], w['kg'], w['gsum'])
    return pl.pallas_call(
        _stage_a_kernel,
        grid=(n // tile,),
        in_specs=[row(D_MODEL)] + [_const_spec(c.shape) for c in consts],
        out_specs=(row(D_MODEL), row(3 * M_WIDTH), row(4 * M_HEADS), row(A_WIDTH), row(4 * LANES)),
        out_shape=out_shape,
        scratch_shapes=[pltpu.VMEM((tile, D_FF), BF16)],
        compiler_params=pltpu.CompilerParams(dimension_semantics=("parallel",),
                                             vmem_limit_bytes=VMEM_LIMIT_BYTES),
        name="stage_a",
    )(x2d, *consts)


def _mlstm_state_update(k, vaug, a_col, decay_log, m0, caug):
    m_loc = jnp.max(a_col, axis=0, keepdims=True)
    m_new = jnp.maximum(decay_log + m0, m_loc)
    wk = k.astype(F32) * jnp.exp(a_col - m_new)
    wkt = jnp.transpose(wk).astype(BF16)
    caug_new = jnp.exp(decay_log + m0 - m_new) * caug + _dot(wkt, vaug)
    return caug_new, m_new


def _mlstm_unit(q, k, v, b_col, r_col, r_row, b_last, mask, ones, c_ref, m_ref, u):
    m0 = m_ref[u][0:1, 0:1]
    caug = c_ref[u]
    vaug = jnp.concatenate([v, ones], axis=1)
    s = _dot(q, k, _NT)
    d = jnp.where(mask, b_col + r_row, NEG)
    m_inter = b_col + m0
    m_t = jnp.maximum(m_inter, jnp.max(d, axis=1, keepdims=True))
    sp = (s * jnp.exp(d - m_t)).astype(BF16)
    tot = _dot(sp, vaug) + jnp.exp(m_inter - m_t) * _dot(q, caug.astype(BF16))
    h = tot[:, :M_DV] / jnp.maximum(jnp.abs(tot[:, M_DV:]), jnp.exp(-m_t))
    caug_new, m_new = _mlstm_state_update(k, vaug, b_last + r_col, b_last, m0, caug)
    c_ref[u] = caug_new
    m_ref[u] = jnp.broadcast_to(m_new, m_ref.shape[1:])
    return h


def _mlstm_kernel(qf_ref, kf_ref, vf_ref, qb_ref, kb_ref, vb_ref, gcf_ref, gcb_ref, grf_ref,
                  grb_ref, mk_ref, mv_ref, mg_ref, tril_ref, triu_ref, trim_ref,
                  hf_ref, hb_ref, c_ref, m_ref):
    L = qf_ref.shape[1]
    H = M_HEADS
    c = pl.program_id(1)

    @pl.when(c == 0)
    def _init():
        c_ref[...] = jnp.zeros_like(c_ref)
        m_ref[...] = jnp.zeros_like(m_ref)
        g = mg_ref[...]
        valid = lax.broadcasted_iota(jnp.int32, g.shape, 0) < N_META
        lf = jnp.where(valid, _log_sigmoid(g), 0.0)
        li = jnp.where(valid, g, NEG)
        cs = _split_dot_rhs(trim_ref[...], lf)
        tot = cs[META_PAD - 1:META_PAD, :]
        ones_m = jnp.ones((META_PAD, M_DV), BF16)
        zero = jnp.zeros((1, 1), F32)
        for h in range(H):
            a_col = tot[:, H + h:H + h + 1] - cs[:, H + h:H + h + 1] + li[:, h:h + 1]
            sl = slice(h * M_DK, (h + 1) * M_DK)
            vaug = jnp.concatenate([mv_ref[:, sl], ones_m], axis=1)
            caug, m_new = _mlstm_state_update(mk_ref[:, sl], vaug, a_col, tot[:, H + h:H + h + 1],
                                              zero, jnp.zeros(c_ref.shape[1:], F32))
            c_ref[h] = caug
            m_ref[h] = jnp.broadcast_to(m_new, m_ref.shape[1:])

    row = lax.broadcasted_iota(jnp.int32, (L, L), 0)
    col = lax.broadcasted_iota(jnp.int32, (L, L), 1)
    ones = jnp.ones((L, M_DV), BF16)

    gcf = gcf_ref[0]
    grf = grf_ref[0]
    csc = _split_dot_rhs(tril_ref[...], _log_sigmoid(gcf))
    csr = _split_dot_lhs(_log_sigmoid(grf), triu_ref[...])
    mask = row >= col
    for h in range(H):
        sl = slice(h * M_DK, (h + 1) * M_DK)
        b_col = csc[:, H + h:H + h + 1]
        r_col = gcf[:, h:h + 1] - b_col
        r_row = grf[h:h + 1, :] - csr[H + h:H + h + 1, :]
        hf_ref[0, :, sl] = _mlstm_unit(qf_ref[0, :, sl], kf_ref[0, :, sl], vf_ref[0, :, sl],
                                       b_col, r_col, r_row, b_col[L - 1:L, :], mask, ones,
                                       c_ref, m_ref, h)

    gcb = gcb_ref[0]
    grb = grb_ref[0]
    lfc = _log_sigmoid(gcb)
    lfr = _log_sigmoid(grb)
    csc = _split_dot_rhs(tril_ref[...], lfc)
    csr = _split_dot_lhs(lfr, triu_ref[...])
    csc = csc[L - 1:L, :] - csc + lfc
    csr = csr[:, L - 1:L] - csr + lfr
    mask = row <= col
    for h in range(H):
        sl = slice(h * M_DK, (h + 1) * M_DK)
        b_col = csc[:, 3 * H + h:3 * H + h + 1]
        r_col = gcb[:, 2 * H + h:2 * H + h + 1] - b_col
        r_row = grb[2 * H + h:2 * H + h + 1, :] - csr[3 * H + h:3 * H + h + 1, :]
        hb_ref[0, :, sl] = _mlstm_unit(qb_ref[0, :, sl], kb_ref[0, :, sl], vb_ref[0, :, sl],
                                       b_col, r_col, r_row, b_col[0:1, :], mask, ones,
                                       c_ref, m_ref, H + h)


def _mlstm(qkv, gates, meta_qkv, meta_gates, chunk):
    B, S, _ = qkv.shape
    L = chunk
    assert S % L == 0
    NC = S // L
    gates_t = jnp.transpose(gates, (0, 2, 1))
    tril = jnp.asarray(np.tril(np.ones((L, L), np.float32)), BF16)
    triu = jnp.asarray(np.triu(np.ones((L, L), np.float32)), BF16)
    trim = jnp.asarray(np.tril(np.ones((META_PAD, META_PAD), np.float32)), BF16)
    G = 4 * M_HEADS
    fwd = lambda j: pl.BlockSpec((1, L, M_WIDTH), lambda b, c: (b, c, j))
    bwd = lambda j: pl.BlockSpec((1, L, M_WIDTH), lambda b, c: (b, NC - 1 - c, j))
    in_specs = [
        fwd(0), fwd(1), fwd(2), bwd(0), bwd(1), bwd(2),
        pl.BlockSpec((1, L, G), lambda b, c: (b, c, 0)),
        pl.BlockSpec((1, L, G), lambda b, c: (b, NC - 1 - c, 0)),
        pl.BlockSpec((1, G, L), lambda b, c: (b, 0, c)),
        pl.BlockSpec((1, G, L), lambda b, c: (b, 0, NC - 1 - c)),
        pl.BlockSpec((META_PAD, M_WIDTH), lambda b, c: (0, 1)),
        pl.BlockSpec((META_PAD, M_WIDTH), lambda b, c: (0, 2)),
        pl.BlockSpec((META_PAD, G), lambda b, c: (0, 0)),
        pl.BlockSpec((L, L), lambda b, c: (0, 0)),
        pl.BlockSpec((L, L), lambda b, c: (0, 0)),
        pl.BlockSpec((META_PAD, META_PAD), lambda b, c: (0, 0)),
    ]
    out_specs = (pl.BlockSpec((1, L, M_WIDTH), lambda b, c: (b, c, 0)),
                 pl.BlockSpec((1, L, M_WIDTH), lambda b, c: (b, NC - 1 - c, 0)))
    return pl.pallas_call(
        _mlstm_kernel,
        grid=(B, NC),
        in_specs=in_specs,
        out_specs=out_specs,
        out_shape=(jax.ShapeDtypeStruct((B, S, M_WIDTH), F32),
                   jax.ShapeDtypeStruct((B, S, M_WIDTH), F32)),
        scratch_shapes=[pltpu.VMEM((2 * M_HEADS, M_DK, 2 * M_DV), F32),
                        pltpu.VMEM((2 * M_HEADS, 8, LANES), F32)],
        compiler_params=pltpu.CompilerParams(dimension_semantics=("parallel", "arbitrary"),
                                             vmem_limit_bytes=VMEM_LIMIT_BYTES),
        name="mlstm",
    )(qkv, qkv, qkv, qkv, qkv, qkv, gates, gates, gates_t, gates_t,
      meta_qkv, meta_qkv, meta_gates, tril, triu, trim)


def _attn_kernel(q_ref, kp_ref, kc_ref, kn_ref, km_ref, bias_ref, o_ref):
    GW = A_GROUP * A_DH
    lane = lax.broadcasted_iota(jnp.int32, (BLOCK, GW), 1)
    head_masks = [((lane >= g * A_DH) & (lane < (g + 1) * A_DH)).astype(F32)
                  for g in range(A_GROUP)]
    pad = jnp.zeros((N_KEYS - 3 * BLOCK - N_META, LANES), BF16)
    for j in range(A_KV_HEADS):
        ksl = slice(j * LANES, (j + 1) * LANES)
        vsl = slice((A_KV_HEADS + j) * LANES, (A_KV_HEADS + j + 1) * LANES)
        qg = q_ref[0, :, j * GW:(j + 1) * GW].astype(F32)
        qs = jnp.concatenate([(qg * m).astype(BF16) for m in head_masks], axis=0)
        k = jnp.concatenate([kp_ref[0, :, ksl], kc_ref[0, :, ksl], kn_ref[0, :, ksl],
                             km_ref[:, ksl], pad], axis=0)
        v = jnp.concatenate([kp_ref[0, :, vsl], kc_ref[0, :, vsl], kn_ref[0, :, vsl],
                             km_ref[:, vsl], pad], axis=0)
        k = jnp.concatenate([k, k], axis=1)
        v = jnp.concatenate([v, v], axis=1)
        s = _dot(qs, k, _NT) + bias_ref[0, j]
        e = jnp.exp(s - jnp.max(s, axis=1, keepdims=True))
        den = jnp.sum(e, axis=1, keepdims=True)
        o = _dot(e.astype(BF16), v) / den
        og = head_masks[0] * o[:BLOCK]
        for g in range(1, A_GROUP):
            og = og + head_masks[g] * o[g * BLOCK:(g + 1) * BLOCK]
        o_ref[0, :, j * GW:(j + 1) * GW] = og.astype(BF16)


def _attention(qa, kva, meta_kva, bias):
    B, S, _ = qa.shape
    NB = S // BLOCK
    assert S % BLOCK == 0 and NB >= 2
    W = 4 * LANES
    variant = lambda n: jnp.where(n == 0, 0, jnp.where(n == NB - 1, 2, 1))
    in_specs = [
        pl.BlockSpec((1, BLOCK, A_WIDTH), lambda b, n: (b, n, 0)),
        pl.BlockSpec((1, BLOCK, W), lambda b, n: (b, jnp.maximum(n - 1, 0), 0)),
        pl.BlockSpec((1, BLOCK, W), lambda b, n: (b, n, 0)),
        pl.BlockSpec((1, BLOCK, W), lambda b, n: (b, jnp.minimum(n + 1, NB - 1), 0)),
        pl.BlockSpec((N_META, W), lambda b, n: (0, 0)),
        pl.BlockSpec((1, A_KV_HEADS, A_GROUP * BLOCK, N_KEYS), lambda b, n: (variant(n), 0, 0, 0)),
    ]
    return pl.pallas_call(
        _attn_kernel,
        grid=(B, NB),
        in_specs=in_specs,
        out_specs=pl.BlockSpec((1, BLOCK, A_WIDTH), lambda b, n: (b, n, 0)),
        out_shape=jax.ShapeDtypeStruct((B, S, A_WIDTH), BF16),
        compiler_params=pltpu.CompilerParams(dimension_semantics=("parallel", "parallel"),
                                             vmem_limit_bytes=VMEM_LIMIT_BYTES),
        name="window_attn",
    )(qa, kva, kva, kva, meta_kva, bias)


def _t5_bucket_np(rel):
    nb = N_BUCKETS // 2
    max_exact = nb // 2
    ret = np.where(rel > 0, nb, 0)
    n = np.abs(rel)
    nf = np.maximum(n, 1).astype(np.float32)
    large = max_exact + (np.log(nf / max_exact) / math.log(MAX_DIST / max_exact)
                         * (nb - max_exact)).astype(np.int32)
    large = np.minimum(large, nb - 1)
    return ret + np.where(n < max_exact, n, large)


def _attention_bias(rel_bias, sink_logits, n_blocks):
    t = np.arange(BLOCK)
    s_off = np.arange(3 * BLOCK) - BLOCK
    rel_band = s_off[None, :] - t[:, None]
    band_ok = np.abs(rel_band) <= WINDOW
    band_bucket = _t5_bucket_np(rel_band)
    q_pos = N_META + np.arange(n_blocks * BLOCK).reshape(n_blocks, BLOCK)
    meta_bucket = _t5_bucket_np(np.arange(N_META)[None, None, :] - q_pos[..., None])
    assert (meta_bucket[1:] == meta_bucket[1:2]).all()
    table = rel_bias.astype(F32)
    neg = jnp.full((BLOCK, 3 * BLOCK, A_HEADS), NEG, F32)
    variants = []
    for var in range(3):
        ok = band_ok.copy()
        if var == 0:
            ok[:, :BLOCK] = False
        if var == 2:
            ok[:, 2 * BLOCK:] = False
        band = jnp.where(jnp.asarray(ok)[:, :, None], table[band_bucket], neg)
        meta = table[meta_bucket[0 if var == 0 else 1]]
        sink = jnp.broadcast_to(sink_logits.astype(F32)[None, None, :], (BLOCK, 1, A_HEADS))
        padc = jnp.full((BLOCK, N_KEYS - 3 * BLOCK - N_META - 1, A_HEADS), NEG, F32)
        full = jnp.concatenate([band, meta, sink, padc], axis=1)
        full = jnp.transpose(full, (2, 0, 1)).reshape(A_KV_HEADS, A_GROUP * BLOCK, N_KEYS)
        variants.append(full)
    return jnp.stack(variants)


def _stage_d_kernel(x1_ref, hf_ref, hb_ref, at_ref, nm_ref, wo_ref, mg_ref, wga_ref, wgb_ref,
                    wupm_ref, wupa_ref, wout_ref, nf_ref, wgu_ref, wd_ref, out_ref, a_scr, hm_scr):
    x1 = x1_ref[...]
    hn = _rms(x1, nm_ref[...]).astype(BF16)
    for h in range(M_HEADS):
        sl = slice(h * M_DV, (h + 1) * M_DV)
        hs = hf_ref[:, sl] + hb_ref[:, sl]
        o = _dot(hn, wo_ref[:, sl])
        hm_scr[:, sl] = (_rms(hs, mg_ref[:, sl]) * _sigmoid(o)).astype(BF16)
    mixed = _sigmoid(_dot(hn, wga_ref[...])) * _dot(hm_scr[...], wupm_ref[...])
    mixed = mixed + _sigmoid(_dot(hn, wgb_ref[...])) * _dot(at_ref[...], wupa_ref[...])
    x2 = x1 + _dot(mixed.astype(BF16), wout_ref[...])
    out_ref[...] = _ffn_residual(x2, nf_ref, wgu_ref, wd_ref, a_scr)


def _stage_d(x1, hf, hb, at, w, tile):
    n = x1.shape[0]
    row = lambda width: pl.BlockSpec((tile, width), lambda i: (i, 0))
    consts = (w['nm'], w['wo'], w['mg'], w['wga'], w['wgb'], w['wupm'], w['wupa'], w['wout'],
              w['nf2'], w['wgu2'], w['wd2'])
    return pl.pallas_call(
        _stage_d_kernel,
        grid=(n // tile,),
        in_specs=[row(D_MODEL), row(M_WIDTH), row(M_WIDTH), row(A_WIDTH)]
                 + [_const_spec(c.shape) for c in consts],
        out_specs=row(D_MODEL),
        out_shape=jax.ShapeDtypeStruct((n, D_MODEL), F32),
        scratch_shapes=[pltpu.VMEM((tile, D_FF), BF16), pltpu.VMEM((tile, M_WIDTH), BF16)],
        compiler_params=pltpu.CompilerParams(dimension_semantics=("parallel",),
                                             vmem_limit_bytes=VMEM_LIMIT_BYTES),
        name="stage_d",
    )(x1, hf, hb, at, *consts)


def _prep_weights(norm_ffn1, w_ffn1_gu, w_ffn1_down, norm_mix, w_in, b_gates, m_out_gain,
                  q_norm_gain, k_norm_gain, w_up_m, w_up_a, w_out, norm_ffn2, w_ffn2_gu,
                  w_ffn2_down):
    def gu_chunks(w_gu):
        g = w_gu[:, :D_FF].reshape(D_MODEL, N_FF_CHUNKS, FF_CHUNK)
        u = w_gu[:, D_FF:].reshape(D_MODEL, N_FF_CHUNKS, FF_CHUNK)
        return jnp.transpose(jnp.concatenate([g, u], axis=2), (1, 0, 2)).astype(BF16)

    row = lambda v: v.reshape(1, -1).astype(F32)
    o_qm, o_om = 0, 3 * M_WIDTH
    o_gm = o_om + M_WIDTH
    o_qa = o_gm + 4 * M_HEADS
    o_ka = o_qa + A_WIDTH
    o_va = o_ka + A_KV_HEADS * A_DH
    o_ga = o_va + A_KV_HEADS * A_DH
    o_gb = o_ga + D_MODEL
    cols = lambda a, b: w_in[:, a:b]
    heads = lambda base, j: cols(base + j * A_DH, base + (j + 1) * A_DH)
    wkv = jnp.concatenate([heads(o_ka, 0), heads(o_ka, 0), heads(o_ka, 1), heads(o_ka, 1),
                           heads(o_va, 0), heads(o_va, 0), heads(o_va, 1), heads(o_va, 1)], axis=1)
    n_g = 4 * M_HEADS
    gsum = np.kron(np.eye(LANES // A_DH, dtype=np.float32), np.ones((A_DH, A_DH), np.float32))
    return {
        'nf1': row(norm_ffn1), 'wgu1': gu_chunks(w_ffn1_gu), 'wd1': w_ffn1_down.astype(BF16),
        'nm': row(norm_mix),
        'wqkv': cols(o_qm, o_om).astype(BF16),
        'wo': cols(o_om, o_gm).astype(BF16),
        'wg': jnp.pad(cols(o_gm, o_qa), ((0, 0), (0, LANES - n_g))).astype(BF16),
        'bg': jnp.pad(row(b_gates), ((0, 0), (0, LANES - n_g))),
        'wqa': cols(o_qa, o_ka).astype(BF16),
        'wkv': wkv.astype(BF16),
        'qg': jnp.tile(row(q_norm_gain), (1, A_HEADS)),
        'kg': jnp.tile(row(k_norm_gain), (1, LANES // A_DH)),
        'gsum': jnp.asarray(np.concatenate([gsum, gsum], axis=0), BF16),
        'wga': cols(o_ga, o_gb).astype(BF16), 'wgb': cols(o_gb, o_gb + D_MODEL).astype(BF16),
        'mg': row(m_out_gain),
        'wupm': w_up_m.astype(BF16), 'wupa': w_up_a.astype(BF16), 'wout': w_out.astype(BF16),
        'nf2': row(norm_ffn2), 'wgu2': gu_chunks(w_ffn2_gu), 'wd2': w_ffn2_down.astype(BF16),
    }


def _encode(x, w, meta, bias_fn, tile, chunk):
    B, S, _ = x.shape
    meta_qkv, meta_gates, meta_kva = meta
    x1, qkv, gates, qa, kva = _stage_a(x.reshape(B * S, D_MODEL), w, tile)
    hf, hb = _mlstm(qkv.reshape(B, S, -1), gates.reshape(B, S, -1), meta_qkv, meta_gates, chunk)
    at = _attention(qa.reshape(B, S, -1), kva.reshape(B, S, -1), meta_kva, bias_fn(S // BLOCK))
    y = _stage_d(x1, hf.reshape(B * S, -1), hb.reshape(B * S, -1), at.reshape(B * S, -1), w, tile)
    return y.reshape(B, S, D_MODEL)


def _layer(x_groups, meta_tokens, rel_bias, norm_ffn1, w_ffn1_gu, w_ffn1_down, norm_mix, w_in,
           b_gates, m_out_gain, q_norm_gain, k_norm_gain, sink_logits, w_up_m, w_up_a, w_out,
           norm_ffn2, w_ffn2_gu, w_ffn2_down, tile=TOKEN_TILE, chunk=M_CHUNK):
    assert norm_ffn1.shape[0] == 1, "single layer"
    w = _prep_weights(norm_ffn1[0], w_ffn1_gu[0], w_ffn1_down[0], norm_mix[0], w_in[0], b_gates[0],
                      m_out_gain[0], q_norm_gain[0], k_norm_gain[0], w_up_m[0], w_up_a[0],
                      w_out[0], norm_ffn2[0], w_ffn2_gu[0], w_ffn2_down[0])
    _, m_qkv, m_gates, _, m_kva = _stage_a(meta_tokens.astype(F32), w, N_META)
    pad_rows = lambda a: jnp.pad(a, ((0, META_PAD - N_META), (0, 0)))
    meta = (pad_rows(m_qkv), pad_rows(m_gates), m_kva)
    bias_fn = functools.lru_cache(None)(
        lambda nb: _attention_bias(rel_bias, sink_logits[0], nb))
    return tuple(_encode(x, w, meta, bias_fn, tile, chunk) for x in x_groups)


def kernel(x_prompt, x_sample, meta_tokens, rel_bias, norm_ffn1, w_ffn1_gu, w_ffn1_down, norm_mix,
           w_in, b_gates, m_out_gain, q_norm_gain, k_norm_gain, sink_logits, w_up_m, w_up_a, w_out,
           norm_ffn2, w_ffn2_gu, w_ffn2_down):
    return _layer((x_prompt, x_sample), meta_tokens, rel_bias, norm_ffn1, w_ffn1_gu, w_ffn1_down,
                  norm_mix, w_in, b_gates, m_out_gain, q_norm_gain, k_norm_gain, sink_logits,
                  w_up_m, w_up_a, w_out, norm_ffn2, w_ffn2_gu, w_ffn2_down)
```

```python
import functools
import math

import numpy as np
import jax
import jax.numpy as jnp
from jax import lax
from jax.experimental import pallas as pl
from jax.experimental.pallas import tpu as pltpu

F32 = jnp.float32
BF16 = jnp.bfloat16

D_MODEL = 1024
D_FF = 2816
N_META = 16
M_HEADS = 4
M_DK = 128
M_DV = 128
M_WIDTH = M_HEADS * M_DV
A_HEADS = 8
A_KV_HEADS = 2
A_GROUP = A_HEADS // A_KV_HEADS
A_DH = 64
A_WIDTH = A_HEADS * A_DH
WINDOW = 128
BLOCK = 128
N_BUCKETS = 32
MAX_DIST = 128
EPS = 1e-6
NEG = -1e30

LANES = 128
MXU_DIM = 256
VMEM_LIMIT_BYTES = 58 * 1024 * 1024

FF_CHUNK = MXU_DIM
N_FF_CHUNKS = D_FF // FF_CHUNK
TOKEN_TILE = 512
M_CHUNK = 256
META_PAD = 128
N_KEYS = 3 * BLOCK + N_META + 16

_NN = (((1,), (0,)), ((), ()))
_NT = (((1,), (1,)), ((), ()))


def _dot(a, b, dims=_NN):
    return lax.dot_general(a, b, dims, preferred_element_type=F32)


def _split_dot_lhs(x_f32, a_bf16):
    hi = x_f32.astype(BF16)
    lo = (x_f32 - hi.astype(F32)).astype(BF16)
    return _dot(hi, a_bf16) + _dot(lo, a_bf16)


def _rms(x, g):
    ms = jnp.mean(x * x, axis=-1, keepdims=True)
    return x * lax.rsqrt(ms + EPS) * g


def _sigmoid(x):
    return 1.0 / (1.0 + jnp.exp(-x))


def _log_sigmoid(x):
    return jnp.minimum(x, 0.0) - jnp.log1p(jnp.exp(-jnp.abs(x)))


def _const_spec(shape):
    nd = len(shape)
    return pl.BlockSpec(shape, lambda *_: (0,) * nd, pipeline_mode=pl.Buffered(1))


def _ffn_residual(x, nf_ref, wgu_ref, wd_ref, a_scr):
    xn = _rms(x, nf_ref[...]).astype(BF16)
    for j in range(N_FF_CHUNKS):
        gu = _dot(xn, wgu_ref[j])
        g = gu[:, :FF_CHUNK]
        u = gu[:, FF_CHUNK:]
        a_scr[:, j * FF_CHUNK:(j + 1) * FF_CHUNK] = (g * _sigmoid(g) * u).astype(BF16)
    y = _dot(a_scr[...], wd_ref[...])
    return x + 0.5 * y


def _group_sumsq(x, gsum_ref):
    sq = x * x
    hi = sq.astype(BF16)
    lo = (sq - hi.astype(F32)).astype(BF16)
    return _dot(jnp.concatenate([hi, lo], axis=1), gsum_ref[...])


def _stage_a_kernel(x_ref, nf_ref, wgu_ref, wd_ref, nm_ref, wqkv_ref, wg_ref, bg_ref, wqa_ref,
                    wkv_ref, qg_ref, kg_ref, gsum_ref,
                    x1_ref, qkv_ref, gate_ref, qa_ref, kva_ref, a_scr):
    x1 = _ffn_residual(x_ref[...], nf_ref, wgu_ref, wd_ref, a_scr)
    x1_ref[...] = x1
    hn = _rms(x1, nm_ref[...]).astype(BF16)
    qkv_ref[:, :M_WIDTH] = (_dot(hn, wqkv_ref[:, :M_WIDTH]) * (M_DK ** -0.5)).astype(BF16)
    qkv_ref[:, M_WIDTH:] = _dot(hn, wqkv_ref[:, M_WIDTH:]).astype(BF16)
    gates = _dot(hn, wg_ref[...]) + bg_ref[...]
    gate_ref[...] = gates[:, :4 * M_HEADS]
    q_all = _dot(hn, wqa_ref[...])
    for t in range(A_WIDTH // LANES):
        sl = slice(t * LANES, (t + 1) * LANES)
        q = q_all[:, sl]
        ss = _group_sumsq(q, gsum_ref)
        qn = q * lax.rsqrt(ss * (1.0 / A_DH) + EPS) * qg_ref[:, sl] * (A_DH ** -0.5)
        qa_ref[:, sl] = qn.astype(BF16)
    kv = _dot(hn, wkv_ref[...])
    for t in range(A_KV_HEADS):
        sl = slice(t * LANES, (t + 1) * LANES)
        k = kv[:, sl]
        ss = _group_sumsq(k, gsum_ref)
        kva_ref[:, sl] = (k * lax.rsqrt(ss * (1.0 / A_DH) + EPS) * kg_ref[...]).astype(BF16)
    vsl = slice(A_KV_HEADS * LANES, 2 * A_KV_HEADS * LANES)
    kva_ref[:, vsl] = kv[:, vsl].astype(BF16)


def _stage_a(x2d, w, tile):
    n = x2d.shape[0]
    assert n % tile == 0
    row = lambda width: pl.BlockSpec((tile, width), lambda i: (i, 0))
    out_shape = (
        jax.ShapeDtypeStruct((n, D_MODEL), F32),
        jax.ShapeDtypeStruct((n, 3 * M_WIDTH), BF16),
        jax.ShapeDtypeStruct((n, 4 * M_HEADS), F32),
        jax.ShapeDtypeStruct((n, A_WIDTH), BF16),
        jax.ShapeDtypeStruct((n, 4 * LANES), BF16),
    )
    consts = (w['nf1'], w['wgu1'], w['wd1'], w['nm'], w['wqkv'], w['wg'], w['bg'], w['wqa'],
              w['wkv'], w['qg'], w['kg'], w['gsum'])
    return pl.pallas_call(
        _stage_a_kernel,
        grid=(n // tile,),
        in_specs=[row(D_MODEL)] + [_const_spec(c.shape) for c in consts],
        out_specs=(row(D_MODEL), row(3 * M_WIDTH), row(4 * M_HEADS), row(A_WIDTH), row(4 * LANES)),
        out_shape=out_shape,
        scratch_shapes=[pltpu.VMEM((tile, D_FF), BF16)],
        compiler_params=pltpu.CompilerParams(dimension_semantics=("parallel",),
                                             vmem_limit_bytes=VMEM_LIMIT_BYTES),
        name="stage_a",
    )(x2d, *consts)


N_UNITS = 2 * M_HEADS
LOG2E = 1.4426950408889634


def _running_max(x, L):
    row = lax.broadcasted_iota(jnp.int32, x.shape, 0)
    lane = lax.broadcasted_iota(jnp.int32, x.shape, 1)
    fwd = row < M_HEADS
    k = 1
    while k < L:
        prev = jnp.where(lane >= k, pltpu.roll(x, k, axis=1), NEG)
        nxt = jnp.where(lane < L - k, pltpu.roll(x, L - k, axis=1), NEG)
        x = jnp.maximum(x, jnp.where(fwd, prev, nxt))
        k *= 2
    return x


def _state_update(kt_f32, w_row, vaug, s_old_row, caug):
    wkt = (kt_f32 * w_row).astype(BF16)
    s_old = jnp.concatenate([s_old_row, s_old_row], axis=1)
    return s_old * caug + _dot(wkt, vaug)


def _mlstm_kernel(qf_ref, kf_ref, vf_ref, qb_ref, kb_ref, vb_ref, grf_ref, grb_ref,
                  mk_ref, mv_ref, mg_ref, triu_ref, trim_ref,
                  hf_ref, hb_ref, c_ref, m_ref):
    L = qf_ref.shape[1]
    H = M_HEADS
    c = pl.program_id(1)

    @pl.when(c == 0)
    def _init():
        c_ref[...] = jnp.zeros_like(c_ref)
        g = mg_ref[...]
        valid = lax.broadcasted_iota(jnp.int32, g.shape, 1) < N_META
        lf = jnp.where(valid, _log_sigmoid(g), 0.0)
        li = jnp.where(valid, g, NEG)
        cs = _split_dot_lhs(lf, trim_ref[...])
        tot = cs[H:2 * H, META_PAD - 1:META_PAD]
        a = tot - cs[H:2 * H] + li[0:H]
        m_new = jnp.maximum(tot, jnp.max(a, axis=1, keepdims=True))
        w = jnp.exp(a - m_new)
        ones_m = jnp.ones((META_PAD, M_DV), BF16)
        for h in range(H):
            sl = slice(h * M_DK, (h + 1) * M_DK)
            vaug = jnp.concatenate([mv_ref[:, sl], ones_m], axis=1)
            kt = jnp.transpose(mk_ref[:, sl].astype(F32))
            c_ref[h] = _state_update(kt, w[h:h + 1], vaug, jnp.zeros((1, LANES), F32),
                                     jnp.zeros(c_ref.shape[1:], F32))
        m_ref[...] = jnp.concatenate([jnp.broadcast_to(m_new, (H, LANES)),
                                      jnp.zeros((H, LANES), F32)], axis=0)

    grf = grf_ref[0]
    grb = grb_ref[0]
    lff = _log_sigmoid(grf[H:2 * H])
    lfb = _log_sigmoid(grb[3 * H:4 * H])
    cs = _split_dot_lhs(jnp.concatenate([lff, lfb], axis=0), triu_ref[...])
    bf = cs[0:H]
    bb = cs[H:2 * H, L - 1:L] - cs[H:2 * H] + lfb
    b = jnp.concatenate([bf, bb], axis=0)
    r = jnp.concatenate([grf[0:H], grb[2 * H:3 * H]], axis=0) - b
    m0 = m_ref[...]
    m0_l = jnp.concatenate([m0] * (L // LANES), axis=1)
    g = jnp.maximum(m0_l, _running_max(r, L))
    b_last = jnp.concatenate([jnp.broadcast_to(bf[:, L - 1:L], (H, LANES)),
                              jnp.broadcast_to(bb[:, 0:1], (H, LANES))], axis=0)
    a = jnp.concatenate([b_last] * (L // LANES), axis=1) + r
    m_loc = jnp.broadcast_to(jnp.max(a, axis=1, keepdims=True), (N_UNITS, LANES))
    m_new = jnp.maximum(b_last + m0, m_loc)
    w = jnp.exp(a - jnp.concatenate([m_new] * (L // LANES), axis=1))
    s_old = jnp.exp(b_last + m0 - m_new)
    m_ref[...] = m_new
    r2 = r * LOG2E
    m2 = m0 * LOG2E
    rows = jnp.concatenate([g * LOG2E, jnp.exp(-(b + g)),
                            jnp.ones((LANES - 2 * N_UNITS, L), F32)], axis=0)
    cols = jnp.transpose(rows)

    row_i = lax.broadcasted_iota(jnp.int32, (L, L), 0)
    col_i = lax.broadcasted_iota(jnp.int32, (L, L), 1)
    lane_i = lax.broadcasted_iota(jnp.int32, (L, LANES), 1)
    ones = jnp.ones((L, M_DV), BF16)
    den = jnp.zeros((L, LANES), F32)
    for u in range(N_UNITS):
        sl = slice((u % H) * M_DK, (u % H + 1) * M_DK)
        if u < H:
            q, k, v, out, mask = (qf_ref[0, :, sl], kf_ref[0, :, sl], vf_ref[0, :, sl], hf_ref,
                                  row_i >= col_i)
        else:
            q, k, v, out, mask = (qb_ref[0, :, sl], kb_ref[0, :, sl], vb_ref[0, :, sl], hb_ref,
                                  row_i <= col_i)
        caug = c_ref[u]
        vaug = jnp.concatenate([v, ones], axis=1)
        g_col = cols[:, u:u + 1]
        x_intra = jnp.where(mask, _dot(q, k, _NT) * jnp.exp2(r2[u:u + 1] - g_col), 0.0)
        x_inter = q.astype(F32) * jnp.exp2(m2[u:u + 1] - g_col)
        x = jnp.concatenate([x_intra, x_inter], axis=1).astype(BF16)
        tot = _dot(x, jnp.concatenate([vaug, caug.astype(BF16)], axis=0))
        out[0, :, sl] = tot[:, :M_DV]
        den = jnp.where(lane_i == N_UNITS + u, tot[:, M_DV:], den)
        c_ref[u] = _state_update(jnp.transpose(k.astype(F32)), w[u:u + 1], vaug,
                                 s_old[u:u + 1], caug)
    inv = 1.0 / jnp.maximum(jnp.abs(den), cols)
    for u in range(N_UNITS):
        sl = slice((u % H) * M_DV, (u % H + 1) * M_DV)
        out = hf_ref if u < H else hb_ref
        out[0, :, sl] = out[0, :, sl] * inv[:, N_UNITS + u:N_UNITS + u + 1]


def _mlstm(qkv, gates, meta_qkv, meta_gates, chunk):
    B, S, _ = qkv.shape
    L = chunk
    assert S % L == 0 and L % LANES == 0
    NC = S // L
    gates_t = jnp.transpose(gates, (0, 2, 1))
    meta_gates_t = jnp.transpose(meta_gates)
    triu = jnp.asarray(np.triu(np.ones((L, L), np.float32)), BF16)
    trim = jnp.asarray(np.triu(np.ones((META_PAD, META_PAD), np.float32)), BF16)
    G = 4 * M_HEADS
    fwd = lambda j: pl.BlockSpec((1, L, M_WIDTH), lambda b, c: (b, c, j))
    bwd = lambda j: pl.BlockSpec((1, L, M_WIDTH), lambda b, c: (b, NC - 1 - c, j))
    in_specs = [
        fwd(0), fwd(1), fwd(2), bwd(0), bwd(1), bwd(2),
        pl.BlockSpec((1, G, L), lambda b, c: (b, 0, c)),
        pl.BlockSpec((1, G, L), lambda b, c: (b, 0, NC - 1 - c)),
        pl.BlockSpec((META_PAD, M_WIDTH), lambda b, c: (0, 1)),
        pl.BlockSpec((META_PAD, M_WIDTH), lambda b, c: (0, 2)),
        pl.BlockSpec((G, META_PAD), lambda b, c: (0, 0)),
        pl.BlockSpec((L, L), lambda b, c: (0, 0)),
        pl.BlockSpec((META_PAD, META_PAD), lambda b, c: (0, 0)),
    ]
    out_specs = (pl.BlockSpec((1, L, M_WIDTH), lambda b, c: (b, c, 0)),
                 pl.BlockSpec((1, L, M_WIDTH), lambda b, c: (b, NC - 1 - c, 0)))
    return pl.pallas_call(
        _mlstm_kernel,
        grid=(B, NC),
        in_specs=in_specs,
        out_specs=out_specs,
        out_shape=(jax.ShapeDtypeStruct((B, S, M_WIDTH), F32),
                   jax.ShapeDtypeStruct((B, S, M_WIDTH), F32)),
        scratch_shapes=[pltpu.VMEM((N_UNITS, M_DK, 2 * M_DV), F32),
                        pltpu.VMEM((N_UNITS, LANES), F32)],
        compiler_params=pltpu.CompilerParams(dimension_semantics=("parallel", "arbitrary"),
                                             vmem_limit_bytes=VMEM_LIMIT_BYTES),
        name="mlstm",
    )(qkv, qkv, qkv, qkv, qkv, qkv, gates_t, gates_t, meta_qkv, meta_qkv, meta_gates_t, triu, trim)


def _attn_kernel(q_ref, kp_ref, kc_ref, kn_ref, km_ref, bias_ref, o_ref):
    GW = A_GROUP * A_DH
    lane = lax.broadcasted_iota(jnp.int32, (BLOCK, GW), 1)
    head_masks = [((lane >= g * A_DH) & (lane < (g + 1) * A_DH)).astype(F32)
                  for g in range(A_GROUP)]
    pad = jnp.zeros((N_KEYS - 3 * BLOCK - N_META, LANES), BF16)
    for j in range(A_KV_HEADS):
        ksl = slice(j * LANES, (j + 1) * LANES)
        vsl = slice((A_KV_HEADS + j) * LANES, (A_KV_HEADS + j + 1) * LANES)
        qg = q_ref[0, :, j * GW:(j + 1) * GW].astype(F32)
        qs = jnp.concatenate([(qg * m).astype(BF16) for m in head_masks], axis=0)
        k = jnp.concatenate([kp_ref[0, :, ksl], kc_ref[0, :, ksl], kn_ref[0, :, ksl],
                             km_ref[:, ksl], pad], axis=0)
        v = jnp.concatenate([kp_ref[0, :, vsl], kc_ref[0, :, vsl], kn_ref[0, :, vsl],
                             km_ref[:, vsl], pad], axis=0)
        k = jnp.concatenate([k, k], axis=1)
        v = jnp.concatenate([v, v], axis=1)
        s = _dot(qs, k, _NT) + bias_ref[0, j]
        e = jnp.exp(s - jnp.max(s, axis=1, keepdims=True))
        den = jnp.sum(e, axis=1, keepdims=True)
        o = _dot(e.astype(BF16), v) / den
        og = head_masks[0] * o[:BLOCK]
        for g in range(1, A_GROUP):
            og = og + head_masks[g] * o[g * BLOCK:(g + 1) * BLOCK]
        o_ref[0, :, j * GW:(j + 1) * GW] = og.astype(BF16)


def _attention(qa, kva, meta_kva, bias):
    B, S, _ = qa.shape
    NB = S // BLOCK
    assert S % BLOCK == 0 and NB >= 2
    W = 4 * LANES
    variant = lambda n: jnp.where(n == 0, 0, jnp.where(n == NB - 1, 2, 1))
    in_specs = [
        pl.BlockSpec((1, BLOCK, A_WIDTH), lambda b, n: (b, n, 0)),
        pl.BlockSpec((1, BLOCK, W), lambda b, n: (b, jnp.maximum(n - 1, 0), 0)),
        pl.BlockSpec((1, BLOCK, W), lambda b, n: (b, n, 0)),
        pl.BlockSpec((1, BLOCK, W), lambda b, n: (b, jnp.minimum(n + 1, NB - 1), 0)),
        pl.BlockSpec((N_META, W), lambda b, n: (0, 0)),
        pl.BlockSpec((1, A_KV_HEADS, A_GROUP * BLOCK, N_KEYS), lambda b, n: (variant(n), 0, 0, 0)),
    ]
    return pl.pallas_call(
        _attn_kernel,
        grid=(B, NB),
        in_specs=in_specs,
        out_specs=pl.BlockSpec((1, BLOCK, A_WIDTH), lambda b, n: (b, n, 0)),
        out_shape=jax.ShapeDtypeStruct((B, S, A_WIDTH), BF16),
        compiler_params=pltpu.CompilerParams(dimension_semantics=("parallel", "parallel"),
                                             vmem_limit_bytes=VMEM_LIMIT_BYTES),
        name="window_attn",
    )(qa, kva, kva, kva, meta_kva, bias)


def _t5_bucket_np(rel):
    nb = N_BUCKETS // 2
    max_exact = nb // 2
    ret = np.where(rel > 0, nb, 0)
    n = np.abs(rel)
    nf = np.maximum(n, 1).astype(np.float32)
    large = max_exact + (np.log(nf / max_exact) / math.log(MAX_DIST / max_exact)
                         * (nb - max_exact)).astype(np.int32)
    large = np.minimum(large, nb - 1)
    return ret + np.where(n < max_exact, n, large)


def _attention_bias(rel_bias, sink_logits, n_blocks):
    t = np.arange(BLOCK)
    s_off = np.arange(3 * BLOCK) - BLOCK
    rel_band = s_off[None, :] - t[:, None]
    band_ok = np.abs(rel_band) <= WINDOW
    band_bucket = _t5_bucket_np(rel_band)
    q_pos = N_META + np.arange(n_blocks * BLOCK).reshape(n_blocks, BLOCK)
    meta_bucket = _t5_bucket_np(np.arange(N_META)[None, None, :] - q_pos[..., None])
    assert (meta_bucket[1:] == meta_bucket[1:2]).all()
    table = rel_bias.astype(F32)
    lookup = lambda bucket: jnp.einsum(
        '...k,kh->...h', jnp.asarray(np.eye(N_BUCKETS, dtype=np.float32)[bucket]), table,
        precision=lax.Precision.HIGHEST)
    neg = jnp.full((BLOCK, 3 * BLOCK, A_HEADS), NEG, F32)
    variants = []
    for var in range(3):
        ok = band_ok.copy()
        if var == 0:
            ok[:, :BLOCK] = False
        if var == 2:
            ok[:, 2 * BLOCK:] = False
        band = jnp.where(jnp.asarray(ok)[:, :, None], lookup(band_bucket), neg)
        meta = lookup(meta_bucket[0 if var == 0 else 1])
        sink = jnp.broadcast_to(sink_logits.astype(F32)[None, None, :], (BLOCK, 1, A_HEADS))
        padc = jnp.full((BLOCK, N_KEYS - 3 * BLOCK - N_META - 1, A_HEADS), NEG, F32)
        full = jnp.concatenate([band, meta, sink, padc], axis=1)
        full = jnp.transpose(full, (2, 0, 1)).reshape(A_KV_HEADS, A_GROUP * BLOCK, N_KEYS)
        variants.append(full)
    return jnp.stack(variants)


def _stage_d_kernel(x1_ref, hf_ref, hb_ref, at_ref, nm_ref, wo_ref, mg_ref, wga_ref, wgb_ref,
                    wupm_ref, wupa_ref, wout_ref, nf_ref, wgu_ref, wd_ref, out_ref, a_scr, hm_scr):
    x1 = x1_ref[...]
    hn = _rms(x1, nm_ref[...]).astype(BF16)
    o = _dot(hn, wo_ref[...])
    for h in range(M_HEADS):
        sl = slice(h * M_DV, (h + 1) * M_DV)
        hs = hf_ref[:, sl] + hb_ref[:, sl]
        hm_scr[:, sl] = (_rms(hs, mg_ref[:, sl]) * _sigmoid(o[:, sl])).astype(BF16)
    mixed = _sigmoid(_dot(hn, wga_ref[...])) * _dot(hm_scr[...], wupm_ref[...])
    mixed = mixed + _sigmoid(_dot(hn, wgb_ref[...])) * _dot(at_ref[...], wupa_ref[...])
    x2 = x1 + _dot(mixed.astype(BF16), wout_ref[...])
    out_ref[...] = _ffn_residual(x2, nf_ref, wgu_ref, wd_ref, a_scr)


def _stage_d(x1, hf, hb, at, w, tile):
    n = x1.shape[0]
    row = lambda width: pl.BlockSpec((tile, width), lambda i: (i, 0))
    consts = (w['nm'], w['wo'], w['mg'], w['wga'], w['wgb'], w['wupm'], w['wupa'], w['wout'],
              w['nf2'], w['wgu2'], w['wd2'])
    return pl.pallas_call(
        _stage_d_kernel,
        grid=(n // tile,),
        in_specs=[row(D_MODEL), row(M_WIDTH), row(M_WIDTH), row(A_WIDTH)]
                 + [_const_spec(c.shape) for c in consts],
        out_specs=row(D_MODEL),
        out_shape=jax.ShapeDtypeStruct((n, D_MODEL), F32),
        scratch_shapes=[pltpu.VMEM((tile, D_FF), BF16), pltpu.VMEM((tile, M_WIDTH), BF16)],
        compiler_params=pltpu.CompilerParams(dimension_semantics=("parallel",),
                                             vmem_limit_bytes=VMEM_LIMIT_BYTES),
        name="stage_d",
    )(x1, hf, hb, at, *consts)


def _prep_weights(norm_ffn1, w_ffn1_gu, w_ffn1_down, norm_mix, w_in, b_gates, m_out_gain,
                  q_norm_gain, k_norm_gain, w_up_m, w_up_a, w_out, norm_ffn2, w_ffn2_gu,
                  w_ffn2_down):
    def gu_chunks(w_gu):
        g = w_gu[:, :D_FF].reshape(D_MODEL, N_FF_CHUNKS, FF_CHUNK)
        u = w_gu[:, D_FF:].reshape(D_MODEL, N_FF_CHUNKS, FF_CHUNK)
        return jnp.transpose(jnp.concatenate([g, u], axis=2), (1, 0, 2)).astype(BF16)

    row = lambda v: v.reshape(1, -1).astype(F32)
    o_qm, o_om = 0, 3 * M_WIDTH
    o_gm = o_om + M_WIDTH
    o_qa = o_gm + 4 * M_HEADS
    o_ka = o_qa + A_WIDTH
    o_va = o_ka + A_KV_HEADS * A_DH
    o_ga = o_va + A_KV_HEADS * A_DH
    o_gb = o_ga + D_MODEL
    cols = lambda a, b: w_in[:, a:b]
    heads = lambda base, j: cols(base + j * A_DH, base + (j + 1) * A_DH)
    wkv = jnp.concatenate([heads(o_ka, 0), heads(o_ka, 0), heads(o_ka, 1), heads(o_ka, 1),
                           heads(o_va, 0), heads(o_va, 0), heads(o_va, 1), heads(o_va, 1)], axis=1)
    n_g = 4 * M_HEADS
    gsum = np.kron(np.eye(LANES // A_DH, dtype=np.float32), np.ones((A_DH, A_DH), np.float32))
    return {
        'nf1': row(norm_ffn1), 'wgu1': gu_chunks(w_ffn1_gu), 'wd1': w_ffn1_down.astype(BF16),
        'nm': row(norm_mix),
        'wqkv': cols(o_qm, o_om).astype(BF16),
        'wo': cols(o_om, o_gm).astype(BF16),
        'wg': jnp.pad(cols(o_gm, o_qa), ((0, 0), (0, MXU_DIM - n_g))).astype(BF16),
        'bg': jnp.pad(row(b_gates), ((0, 0), (0, MXU_DIM - n_g))),
        'wqa': cols(o_qa, o_ka).astype(BF16),
        'wkv': wkv.astype(BF16),
        'qg': jnp.tile(row(q_norm_gain), (1, A_HEADS)),
        'kg': jnp.tile(row(k_norm_gain), (1, LANES // A_DH)),
        'gsum': jnp.asarray(np.concatenate([gsum, gsum], axis=0), BF16),
        'wga': cols(o_ga, o_gb).astype(BF16), 'wgb': cols(o_gb, o_gb + D_MODEL).astype(BF16),
        'mg': row(m_out_gain),
        'wupm': w_up_m.astype(BF16), 'wupa': w_up_a.astype(BF16), 'wout': w_out.astype(BF16),
        'nf2': row(norm_ffn2), 'wgu2': gu_chunks(w_ffn2_gu), 'wd2': w_ffn2_down.astype(BF16),
    }


def _encode(x, w, meta, bias_fn, tile, chunk):
    B, S, _ = x.shape
    meta_qkv, meta_gates, meta_kva = meta
    x1, qkv, gates, qa, kva = _stage_a(x.reshape(B * S, D_MODEL), w, tile)
    hf, hb = _mlstm(qkv.reshape(B, S, -1), gates.reshape(B, S, -1), meta_qkv, meta_gates, chunk)
    at = _attention(qa.reshape(B, S, -1), kva.reshape(B, S, -1), meta_kva, bias_fn(S // BLOCK))
    y = _stage_d(x1, hf.reshape(B * S, -1), hb.reshape(B * S, -1), at.reshape(B * S, -1), w, tile)
    return y.reshape(B, S, D_MODEL)


def _layer(x_groups, meta_tokens, rel_bias, norm_ffn1, w_ffn1_gu, w_ffn1_down, norm_mix, w_in,
           b_gates, m_out_gain, q_norm_gain, k_norm_gain, sink_logits, w_up_m, w_up_a, w_out,
           norm_ffn2, w_ffn2_gu, w_ffn2_down, tile=TOKEN_TILE, chunk=M_CHUNK):
    assert norm_ffn1.shape[0] == 1, "single layer"
    w = _prep_weights(norm_ffn1[0], w_ffn1_gu[0], w_ffn1_down[0], norm_mix[0], w_in[0], b_gates[0],
                      m_out_gain[0], q_norm_gain[0], k_norm_gain[0], w_up_m[0], w_up_a[0],
                      w_out[0], norm_ffn2[0], w_ffn2_gu[0], w_ffn2_down[0])
    _, m_qkv, m_gates, _, m_kva = _stage_a(meta_tokens.astype(F32), w, N_META)
    pad_rows = lambda a: jnp.pad(a, ((0, META_PAD - N_META), (0, 0)))
    meta = (pad_rows(m_qkv), pad_rows(m_gates), m_kva)
    bias_fn = functools.lru_cache(None)(
        lambda nb: _attention_bias(rel_bias, sink_logits[0], nb))
    return tuple(_encode(x, w, meta, bias_fn, tile, chunk) for x in x_groups)


def kernel(x_prompt, x_sample, meta_tokens, rel_bias, norm_ffn1, w_ffn1_gu, w_ffn1_down, norm_mix,
           w_in, b_gates, m_out_gain, q_norm_gain, k_norm_gain, sink_logits, w_up_m, w_up_a, w_out,
           norm_ffn2, w_ffn2_gu, w_ffn2_down):
    return _layer((x_prompt, x_sample), meta_tokens, rel_bias, norm_ffn1, w_ffn1_gu, w_ffn1_down,
                  norm_mix, w_in, b_gates, m_out_gain, q_norm_gain, k_norm_gain, sink_logits,
                  w_up_m, w_up_a, w_out, norm_ffn2, w_ffn2_gu, w_ffn2_down)
```

```python
import functools
import math

import numpy as np
import jax
import jax.numpy as jnp
from jax import lax
from jax.experimental import pallas as pl
from jax.experimental.pallas import tpu as pltpu

F32 = jnp.float32
BF16 = jnp.bfloat16

D_MODEL = 1024
D_FF = 2816
N_META = 16
M_HEADS = 4
M_DK = 128
M_DV = 128
M_WIDTH = M_HEADS * M_DV
A_HEADS = 8
A_KV_HEADS = 2
A_GROUP = A_HEADS // A_KV_HEADS
A_DH = 64
A_WIDTH = A_HEADS * A_DH
WINDOW = 128
BLOCK = 128
N_BUCKETS = 32
MAX_DIST = 128
EPS = 1e-6
NEG = -1e30
LOG2E = 1.4426950408889634

LANES = 128
MXU_DIM = 256
VMEM_LIMIT_BYTES = 58 * 1024 * 1024

FF_CHUNK = MXU_DIM
N_FF_CHUNKS = D_FF // FF_CHUNK
TOKEN_TILE = 512
M_CHUNK = 256
META_PAD = 128
PREP_CHUNKS = 8
N_KEYS = 3 * BLOCK + N_META + 16

_NN = (((1,), (0,)), ((), ()))
_NT = (((1,), (1,)), ((), ()))


def _dot(a, b, dims=_NN):
    return lax.dot_general(a, b, dims, preferred_element_type=F32)


def _split_dot_lhs(x_f32, a_bf16):
    hi = x_f32.astype(BF16)
    lo = (x_f32 - hi.astype(F32)).astype(BF16)
    return _dot(hi, a_bf16) + _dot(lo, a_bf16)


def _rms(x, g):
    ms = jnp.mean(x * x, axis=-1, keepdims=True)
    return x * lax.rsqrt(ms + EPS) * g


def _sigmoid(x):
    return 1.0 / (1.0 + jnp.exp(-x))


def _log_sigmoid(x):
    return jnp.minimum(x, 0.0) - jnp.log1p(jnp.exp(-jnp.abs(x)))


def _const_spec(shape):
    nd = len(shape)
    return pl.BlockSpec(shape, lambda *_: (0,) * nd, pipeline_mode=pl.Buffered(1))


def _ffn_residual(x, nf_ref, wgu_ref, wd_ref, a_scr):
    xn = _rms(x, nf_ref[...]).astype(BF16)
    for j in range(N_FF_CHUNKS):
        gu = _dot(xn, wgu_ref[j])
        g = gu[:, :FF_CHUNK]
        u = gu[:, FF_CHUNK:]
        a_scr[:, j * FF_CHUNK:(j + 1) * FF_CHUNK] = (g * _sigmoid(g) * u).astype(BF16)
    y = _dot(a_scr[...], wd_ref[...])
    return x + 0.5 * y


def _group_sumsq(x, gsum_ref):
    sq = x * x
    hi = sq.astype(BF16)
    lo = (sq - hi.astype(F32)).astype(BF16)
    return _dot(jnp.concatenate([hi, lo], axis=1), gsum_ref[...])


def _stage_a_kernel(x_ref, nf_ref, wgu_ref, wd_ref, nm_ref, wqkv_ref, wg_ref, bg_ref, wqa_ref,
                    wkv_ref, qg_ref, kg_ref, gsum_ref,
                    x1_ref, qkv_ref, gate_ref, qa_ref, kva_ref, a_scr):
    x1 = _ffn_residual(x_ref[...], nf_ref, wgu_ref, wd_ref, a_scr)
    x1_ref[...] = x1
    hn = _rms(x1, nm_ref[...]).astype(BF16)
    qkv_ref[:, :M_WIDTH] = (_dot(hn, wqkv_ref[:, :M_WIDTH]) * (M_DK ** -0.5)).astype(BF16)
    qkv_ref[:, M_WIDTH:] = _dot(hn, wqkv_ref[:, M_WIDTH:]).astype(BF16)
    gates = _dot(hn, wg_ref[...]) + bg_ref[...]
    gate_ref[...] = gates[:, :4 * M_HEADS]
    q_all = _dot(hn, wqa_ref[...])
    for t in range(A_WIDTH // LANES):
        sl = slice(t * LANES, (t + 1) * LANES)
        q = q_all[:, sl]
        ss = _group_sumsq(q, gsum_ref)
        qn = q * lax.rsqrt(ss * (1.0 / A_DH) + EPS) * qg_ref[:, sl] * (A_DH ** -0.5 * LOG2E)
        qa_ref[:, sl] = qn.astype(BF16)
    kv = _dot(hn, wkv_ref[...])
    for t in range(A_KV_HEADS):
        sl = slice(t * LANES, (t + 1) * LANES)
        k = kv[:, sl]
        ss = _group_sumsq(k, gsum_ref)
        kva_ref[:, sl] = (k * lax.rsqrt(ss * (1.0 / A_DH) + EPS) * kg_ref[...]).astype(BF16)
    vsl = slice(A_KV_HEADS * LANES, 2 * A_KV_HEADS * LANES)
    kva_ref[:, vsl] = kv[:, vsl].astype(BF16)


def _stage_a(x2d, w, tile):
    n = x2d.shape[0]
    assert n % tile == 0
    row = lambda width: pl.BlockSpec((tile, width), lambda i: (i, 0))
    out_shape = (
        jax.ShapeDtypeStruct((n, D_MODEL), F32),
        jax.ShapeDtypeStruct((n, 3 * M_WIDTH), BF16),
        jax.ShapeDtypeStruct((n, 4 * M_HEADS), F32),
        jax.ShapeDtypeStruct((n, A_WIDTH), BF16),
        jax.ShapeDtypeStruct((n, 4 * LANES), BF16),
    )
    consts = (w['nf1'], w['wgu1'], w['wd1'], w['nm'], w['wqkv'], w['wg'], w['bg'], w['wqa'],
              w['wkv'], w['qg'], w['kg'], w['gsum'])
    return pl.pallas_call(
        _stage_a_kernel,
        grid=(n // tile,),
        in_specs=[row(D_MODEL)] + [_const_spec(c.shape) for c in consts],
        out_specs=(row(D_MODEL), row(3 * M_WIDTH), row(4 * M_HEADS), row(A_WIDTH), row(4 * LANES)),
        out_shape=out_shape,
        scratch_shapes=[pltpu.VMEM((tile, D_FF), BF16)],
        compiler_params=pltpu.CompilerParams(dimension_semantics=("parallel",),
                                             vmem_limit_bytes=VMEM_LIMIT_BYTES),
        name="stage_a",
    )(x2d, *consts)


N_UNITS = 2 * M_HEADS


def _running_max(x, L):
    row = lax.broadcasted_iota(jnp.int32, x.shape, 0)
    lane = lax.broadcasted_iota(jnp.int32, x.shape, 1)
    fwd = row < M_HEADS
    k = 1
    while k < L:
        prev = jnp.where(lane >= k, pltpu.roll(x, k, axis=1), NEG)
        nxt = jnp.where(lane < L - k, pltpu.roll(x, L - k, axis=1), NEG)
        x = jnp.maximum(x, jnp.where(fwd, prev, nxt))
        k *= 2
    return x


def _state_update(kt_f32, w_row, vaug, s_old_row, caug):
    wkt = (kt_f32 * w_row).astype(BF16)
    s_old = jnp.concatenate([s_old_row, s_old_row], axis=1)
    return s_old * caug + _dot(wkt, vaug)


def _gate_prep_kernel(g_ref, triu_ref, rows_ref, cols_ref, sc_ref):
    H = M_HEADS
    L = triu_ref.shape[0]
    eye = (lax.broadcasted_iota(jnp.int32, (N_UNITS, LANES), 0)
           == lax.broadcasted_iota(jnp.int32, (N_UNITS, LANES), 1))
    lane_vec = lambda x: jnp.broadcast_to(
        jnp.sum(jnp.where(eye, x, 0.0), axis=0, keepdims=True), (N_UNITS, LANES))
    for j in range(g_ref.shape[2] // L):
        g = g_ref[0, :, j * L:(j + 1) * L]
        lff = _log_sigmoid(g[H:2 * H])
        lfb = _log_sigmoid(g[3 * H:4 * H])
        cs = _split_dot_lhs(jnp.concatenate([lff, lfb], axis=0), triu_ref[...])
        bf = cs[0:H]
        bb = cs[H:2 * H, L - 1:L] - cs[H:2 * H] + lfb
        b = jnp.concatenate([bf, bb], axis=0)
        r = jnp.concatenate([g[0:H], g[2 * H:3 * H]], axis=0) - b
        b_last = jnp.concatenate([jnp.broadcast_to(bf[:, L - 1:L], (H, LANES)),
                                  jnp.broadcast_to(bb[:, 0:1], (H, LANES))], axis=0)
        a = jnp.concatenate([b_last] * (L // LANES), axis=1) + r
        m_loc = jnp.broadcast_to(jnp.max(a, axis=1, keepdims=True), (N_UNITS, LANES))
        rows_ref[0, :, j * L:(j + 1) * L] = jnp.concatenate([r * LOG2E, a], axis=0)
        packed = jnp.concatenate([_running_max(r, L) * LOG2E, b * LOG2E,
                                  jnp.zeros((LANES - 2 * N_UNITS, L), F32)], axis=0)
        cols_ref[0, j * L:(j + 1) * L, :] = jnp.transpose(packed)
        sc_ref[0, j] = jnp.concatenate([b_last, m_loc, lane_vec(b_last), lane_vec(m_loc)], axis=0)


def _gate_prep(gates, chunk, chunks_per_step):
    B, S, G = gates.shape
    L = chunk
    NC = S // L
    J = math.gcd(chunks_per_step, NC)
    gates_t = jnp.transpose(gates, (0, 2, 1))
    triu = jnp.asarray(np.triu(np.ones((L, L), np.float32)), BF16)
    return pl.pallas_call(
        _gate_prep_kernel,
        grid=(B, NC // J),
        in_specs=[pl.BlockSpec((1, G, J * L), lambda b, i: (b, 0, i)),
                  pl.BlockSpec((L, L), lambda b, i: (0, 0))],
        out_specs=(pl.BlockSpec((1, 2 * N_UNITS, J * L), lambda b, i: (b, 0, i)),
                   pl.BlockSpec((1, J * L, LANES), lambda b, i: (b, i, 0)),
                   pl.BlockSpec((1, J, 4 * N_UNITS, LANES), lambda b, i: (b, i, 0, 0))),
        out_shape=(jax.ShapeDtypeStruct((B, 2 * N_UNITS, S), F32),
                   jax.ShapeDtypeStruct((B, S, LANES), F32),
                   jax.ShapeDtypeStruct((B, NC, 4 * N_UNITS, LANES), F32)),
        compiler_params=pltpu.CompilerParams(dimension_semantics=("parallel", "parallel"),
                                             vmem_limit_bytes=VMEM_LIMIT_BYTES),
        name="gate_prep",
    )(gates_t, triu)


def _mlstm_kernel(qf_ref, kf_ref, vf_ref, qb_ref, kb_ref, vb_ref, rf_ref, rb_ref, cf_ref, cb_ref,
                  sf_ref, sb_ref, mk_ref, mv_ref, mg_ref, trim_ref,
                  hf_ref, hb_ref, c_ref, m_ref):
    L = qf_ref.shape[1]
    H = M_HEADS
    U = N_UNITS
    c = pl.program_id(1)

    @pl.when(c == 0)
    def _init():
        c_ref[...] = jnp.zeros_like(c_ref)
        g = mg_ref[...]
        valid = lax.broadcasted_iota(jnp.int32, g.shape, 1) < N_META
        lf = jnp.where(valid, _log_sigmoid(g), 0.0)
        li = jnp.where(valid, g, NEG)
        cs = _split_dot_lhs(lf, trim_ref[...])
        tot = cs[H:2 * H, META_PAD - 1:META_PAD]
        a = tot - cs[H:2 * H] + li[0:H]
        m_new = jnp.maximum(tot, jnp.max(a, axis=1, keepdims=True))
        w = jnp.exp(a - m_new)
        ones_m = jnp.ones((META_PAD, M_DV), BF16)
        for h in range(H):
            sl = slice(h * M_DK, (h + 1) * M_DK)
            vaug = jnp.concatenate([mv_ref[:, sl], ones_m], axis=1)
            kt = jnp.transpose(mk_ref[:, sl].astype(F32))
            c_ref[h] = _state_update(kt, w[h:h + 1], vaug, jnp.zeros((1, LANES), F32),
                                     jnp.zeros(c_ref.shape[1:], F32))
        m_rows = jnp.concatenate([jnp.broadcast_to(m_new, (H, LANES)),
                                  jnp.zeros((H, LANES), F32)], axis=0)
        eye = (lax.broadcasted_iota(jnp.int32, (U, LANES), 0)
               == lax.broadcasted_iota(jnp.int32, (U, LANES), 1))
        m_lanes = jnp.sum(jnp.where(eye, m_rows, 0.0), axis=0, keepdims=True)
        m_ref[...] = jnp.concatenate([m_rows, jnp.broadcast_to(m_lanes, (U, LANES))], axis=0)

    fwd_row = lax.broadcasted_iota(jnp.int32, (U, LANES), 0) < H
    fwd_lane = lax.broadcasted_iota(jnp.int32, (U, LANES), 1) % U < H
    pick_r = lambda i: jnp.where(fwd_row, sf_ref[0, 0, i * U:(i + 1) * U], sb_ref[0, 0, i * U:(i + 1) * U])
    pick_l = lambda i: jnp.where(fwd_lane, sf_ref[0, 0, i * U:(i + 1) * U], sb_ref[0, 0, i * U:(i + 1) * U])
    m0r = m_ref[0:U]
    m0l = m_ref[U:2 * U]
    m_new_r = jnp.maximum(pick_r(0) + m0r, pick_r(1))
    m_new_l = jnp.maximum(pick_l(2) + m0l, pick_l(3))
    s_old = jnp.exp(pick_r(0) + m0r - m_new_r)
    m_ref[...] = jnp.concatenate([m_new_r, m_new_l], axis=0)
    tile_l = lambda x: jnp.concatenate([x] * (L // LANES), axis=1)
    fwd_row_l = tile_l(fwd_row)
    r2 = jnp.where(fwd_row_l, rf_ref[0, 0:U], rb_ref[0, 0:U])
    w = jnp.exp(jnp.where(fwd_row_l, rf_ref[0, U:2 * U], rb_ref[0, U:2 * U]) - tile_l(m_new_r))
    m2 = m0r * LOG2E
    lane_i = lax.broadcasted_iota(jnp.int32, (L, LANES), 1)
    colsx = jnp.where(lane_i % U < H, cf_ref[0], cb_ref[0])
    g2 = jnp.maximum(colsx, m0l[0:1] * LOG2E)
    e_cols = jnp.exp2(-(pltpu.roll(colsx, LANES - U, axis=1) + g2))

    row_i = lax.broadcasted_iota(jnp.int32, (L, L), 0)
    col_i = lax.broadcasted_iota(jnp.int32, (L, L), 1)
    ones = jnp.ones((L, M_DV), BF16)
    den = jnp.zeros((L, LANES), F32)
    for u in range(U):
        sl = slice((u % H) * M_DK, (u % H + 1) * M_DK)
        if u < H:
            q, k, v, out, mask = (qf_ref[0, :, sl], kf_ref[0, :, sl], vf_ref[0, :, sl], hf_ref,
                                  row_i >= col_i)
        else:
            q, k, v, out, mask = (qb_ref[0, :, sl], kb_ref[0, :, sl], vb_ref[0, :, sl], hb_ref,
                                  row_i <= col_i)
        caug = c_ref[u]
        vaug = jnp.concatenate([v, ones], axis=1)
        g_col = g2[:, u:u + 1]
        x_intra = jnp.where(mask, _dot(q, k, _NT) * jnp.exp2(r2[u:u + 1] - g_col), 0.0)
        x_inter = q.astype(F32) * jnp.exp2(m2[u:u + 1] - g_col)
        x = jnp.concatenate([x_intra, x_inter], axis=1).astype(BF16)
        tot = _dot(x, jnp.concatenate([vaug, caug.astype(BF16)], axis=0))
        out[0, :, sl] = tot[:, :M_DV]
        den = jnp.where(lane_i == u, tot[:, M_DV:], den)
        c_ref[u] = _state_update(jnp.transpose(k.astype(F32)), w[u:u + 1], vaug,
                                 s_old[u:u + 1], caug)
    inv = 1.0 / jnp.maximum(jnp.abs(den), e_cols)
    for u in range(U):
        sl = slice((u % H) * M_DV, (u % H + 1) * M_DV)
        out = hf_ref if u < H else hb_ref
        out[0, :, sl] = out[0, :, sl] * inv[:, u:u + 1]


def _mlstm(qkv, gates, meta_qkv, meta_gates, chunk):
    B, S, _ = qkv.shape
    L = chunk
    assert S % L == 0 and L % LANES == 0
    NC = S // L
    rows, cols, sc = _gate_prep(gates, L, PREP_CHUNKS)
    meta_gates_t = jnp.transpose(meta_gates)
    trim = jnp.asarray(np.triu(np.ones((META_PAD, META_PAD), np.float32)), BF16)
    G = 4 * M_HEADS
    fwd = lambda j: pl.BlockSpec((1, L, M_WIDTH), lambda b, c: (b, c, j))
    bwd = lambda j: pl.BlockSpec((1, L, M_WIDTH), lambda b, c: (b, NC - 1 - c, j))
    in_specs = [
        fwd(0), fwd(1), fwd(2), bwd(0), bwd(1), bwd(2),
        pl.BlockSpec((1, 2 * N_UNITS, L), lambda b, c: (b, 0, c)),
        pl.BlockSpec((1, 2 * N_UNITS, L), lambda b, c: (b, 0, NC - 1 - c)),
        pl.BlockSpec((1, L, LANES), lambda b, c: (b, c, 0)),
        pl.BlockSpec((1, L, LANES), lambda b, c: (b, NC - 1 - c, 0)),
        pl.BlockSpec((1, 1, 4 * N_UNITS, LANES), lambda b, c: (b, c, 0, 0)),
        pl.BlockSpec((1, 1, 4 * N_UNITS, LANES), lambda b, c: (b, NC - 1 - c, 0, 0)),
        pl.BlockSpec((META_PAD, M_WIDTH), lambda b, c: (0, 1)),
        pl.BlockSpec((META_PAD, M_WIDTH), lambda b, c: (0, 2)),
        pl.BlockSpec((G, META_PAD), lambda b, c: (0, 0)),
        pl.BlockSpec((META_PAD, META_PAD), lambda b, c: (0, 0)),
    ]
    out_specs = (pl.BlockSpec((1, L, M_WIDTH), lambda b, c: (b, c, 0)),
                 pl.BlockSpec((1, L, M_WIDTH), lambda b, c: (b, NC - 1 - c, 0)))
    return pl.pallas_call(
        _mlstm_kernel,
        grid=(B, NC),
        in_specs=in_specs,
        out_specs=out_specs,
        out_shape=(jax.ShapeDtypeStruct((B, S, M_WIDTH), F32),
                   jax.ShapeDtypeStruct((B, S, M_WIDTH), F32)),
        scratch_shapes=[pltpu.VMEM((N_UNITS, M_DK, 2 * M_DV), F32),
                        pltpu.VMEM((2 * N_UNITS, LANES), F32)],
        compiler_params=pltpu.CompilerParams(dimension_semantics=("parallel", "arbitrary"),
                                             vmem_limit_bytes=VMEM_LIMIT_BYTES),
        name="mlstm",
    )(qkv, qkv, qkv, qkv, qkv, qkv, rows, rows, cols, cols, sc, sc,
      meta_qkv, meta_qkv, meta_gates_t, trim)


def _attn_kernel(q_ref, kp_ref, kc_ref, kn_ref, km_ref, bias_ref, o_ref):
    GW = A_GROUP * A_DH
    lane = lax.broadcasted_iota(jnp.int32, (BLOCK, GW), 1)
    head_masks = [((lane >= g * A_DH) & (lane < (g + 1) * A_DH)).astype(F32)
                  for g in range(A_GROUP)]
    pad = jnp.zeros((N_KEYS - 3 * BLOCK - N_META, LANES), BF16)
    for j in range(A_KV_HEADS):
        ksl = slice(j * LANES, (j + 1) * LANES)
        vsl = slice((A_KV_HEADS + j) * LANES, (A_KV_HEADS + j + 1) * LANES)
        qg = q_ref[0, :, j * GW:(j + 1) * GW].astype(F32)
        qs = jnp.concatenate([(qg * m).astype(BF16) for m in head_masks], axis=0)
        k = jnp.concatenate([kp_ref[0, :, ksl], kc_ref[0, :, ksl], kn_ref[0, :, ksl],
                             km_ref[:, ksl], pad], axis=0)
        v = jnp.concatenate([kp_ref[0, :, vsl], kc_ref[0, :, vsl], kn_ref[0, :, vsl],
                             km_ref[:, vsl], pad], axis=0)
        k = jnp.concatenate([k, k], axis=1)
        v = jnp.concatenate([v, v], axis=1)
        s = _dot(qs, k, _NT) + bias_ref[0, j]
        e = jnp.exp2(s - jnp.max(s, axis=1, keepdims=True))
        den = jnp.sum(e, axis=1, keepdims=True)
        o = _dot(e.astype(BF16), v) / den
        og = head_masks[0] * o[:BLOCK]
        for g in range(1, A_GROUP):
            og = og + head_masks[g] * o[g * BLOCK:(g + 1) * BLOCK]
        o_ref[0, :, j * GW:(j + 1) * GW] = og.astype(BF16)


def _attention(qa, kva, meta_kva, bias):
    B, S, _ = qa.shape
    NB = S // BLOCK
    assert S % BLOCK == 0 and NB >= 2
    W = 4 * LANES
    variant = lambda n: jnp.where(n == 0, 0, jnp.where(n == NB - 1, 2, 1))
    in_specs = [
        pl.BlockSpec((1, BLOCK, A_WIDTH), lambda b, n: (b, n, 0)),
        pl.BlockSpec((1, BLOCK, W), lambda b, n: (b, jnp.maximum(n - 1, 0), 0)),
        pl.BlockSpec((1, BLOCK, W), lambda b, n: (b, n, 0)),
        pl.BlockSpec((1, BLOCK, W), lambda b, n: (b, jnp.minimum(n + 1, NB - 1), 0)),
        pl.BlockSpec((N_META, W), lambda b, n: (0, 0)),
        pl.BlockSpec((1, A_KV_HEADS, A_GROUP * BLOCK, N_KEYS), lambda b, n: (variant(n), 0, 0, 0)),
    ]
    return pl.pallas_call(
        _attn_kernel,
        grid=(B, NB),
        in_specs=in_specs,
        out_specs=pl.BlockSpec((1, BLOCK, A_WIDTH), lambda b, n: (b, n, 0)),
        out_shape=jax.ShapeDtypeStruct((B, S, A_WIDTH), BF16),
        compiler_params=pltpu.CompilerParams(dimension_semantics=("parallel", "parallel"),
                                             vmem_limit_bytes=VMEM_LIMIT_BYTES),
        name="window_attn",
    )(qa, kva, kva, kva, meta_kva, bias)


def _t5_bucket_np(rel):
    nb = N_BUCKETS // 2
    max_exact = nb // 2
    ret = np.where(rel > 0, nb, 0)
    n = np.abs(rel)
    nf = np.maximum(n, 1).astype(np.float32)
    large = max_exact + (np.log(nf / max_exact) / math.log(MAX_DIST / max_exact)
                         * (nb - max_exact)).astype(np.int32)
    large = np.minimum(large, nb - 1)
    return ret + np.where(n < max_exact, n, large)


def _attention_bias(rel_bias, sink_logits, n_blocks):
    t = np.arange(BLOCK)
    s_off = np.arange(3 * BLOCK) - BLOCK
    rel_band = s_off[None, :] - t[:, None]
    band_ok = np.abs(rel_band) <= WINDOW
    band_bucket = _t5_bucket_np(rel_band)
    q_pos = N_META + np.arange(n_blocks * BLOCK).reshape(n_blocks, BLOCK)
    meta_bucket = _t5_bucket_np(np.arange(N_META)[None, None, :] - q_pos[..., None])
    assert (meta_bucket[1:] == meta_bucket[1:2]).all()
    table = rel_bias.astype(F32)
    lookup = lambda bucket: jnp.einsum(
        '...k,kh->...h', jnp.asarray(np.eye(N_BUCKETS, dtype=np.float32)[bucket]), table,
        precision=lax.Precision.HIGHEST)
    neg = jnp.full((BLOCK, 3 * BLOCK, A_HEADS), NEG, F32)
    variants = []
    for var in range(3):
        ok = band_ok.copy()
        if var == 0:
            ok[:, :BLOCK] = False
        if var == 2:
            ok[:, 2 * BLOCK:] = False
        band = jnp.where(jnp.asarray(ok)[:, :, None], lookup(band_bucket), neg)
        meta = lookup(meta_bucket[0 if var == 0 else 1])
        sink = jnp.broadcast_to(sink_logits.astype(F32)[None, None, :], (BLOCK, 1, A_HEADS))
        padc = jnp.full((BLOCK, N_KEYS - 3 * BLOCK - N_META - 1, A_HEADS), NEG, F32)
        full = jnp.concatenate([band, meta, sink, padc], axis=1)
        full = jnp.transpose(full, (2, 0, 1)).reshape(A_KV_HEADS, A_GROUP * BLOCK, N_KEYS)
        variants.append(full)
    return jnp.stack(variants) * LOG2E


def _stage_d_kernel(x1_ref, hf_ref, hb_ref, at_ref, nm_ref, wo_ref, mg_ref, wga_ref, wgb_ref,
                    wupm_ref, wupa_ref, wout_ref, nf_ref, wgu_ref, wd_ref, out_ref, a_scr, hm_scr):
    x1 = x1_ref[...]
    hn = _rms(x1, nm_ref[...]).astype(BF16)
    o = _dot(hn, wo_ref[...])
    for h in range(M_HEADS):
        sl = slice(h * M_DV, (h + 1) * M_DV)
        hs = hf_ref[:, sl] + hb_ref[:, sl]
        hm_scr[:, sl] = (_rms(hs, mg_ref[:, sl]) * _sigmoid(o[:, sl])).astype(BF16)
    mixed = _sigmoid(_dot(hn, wga_ref[...])) * _dot(hm_scr[...], wupm_ref[...])
    mixed = mixed + _sigmoid(_dot(hn, wgb_ref[...])) * _dot(at_ref[...], wupa_ref[...])
    x2 = x1 + _dot(mixed.astype(BF16), wout_ref[...])
    out_ref[...] = _ffn_residual(x2, nf_ref, wgu_ref, wd_ref, a_scr)


def _stage_d(x1, hf, hb, at, w, tile):
    n = x1.shape[0]
    row = lambda width: pl.BlockSpec((tile, width), lambda i: (i, 0))
    consts = (w['nm'], w['wo'], w['mg'], w['wga'], w['wgb'], w['wupm'], w['wupa'], w['wout'],
              w['nf2'], w['wgu2'], w['wd2'])
    return pl.pallas_call(
        _stage_d_kernel,
        grid=(n // tile,),
        in_specs=[row(D_MODEL), row(M_WIDTH), row(M_WIDTH), row(A_WIDTH)]
                 + [_const_spec(c.shape) for c in consts],
        out_specs=row(D_MODEL),
        out_shape=jax.ShapeDtypeStruct((n, D_MODEL), F32),
        scratch_shapes=[pltpu.VMEM((tile, D_FF), BF16), pltpu.VMEM((tile, M_WIDTH), BF16)],
        compiler_params=pltpu.CompilerParams(dimension_semantics=("parallel",),
                                             vmem_limit_bytes=VMEM_LIMIT_BYTES),
        name="stage_d",
    )(x1, hf, hb, at, *consts)


def _prep_weights(norm_ffn1, w_ffn1_gu, w_ffn1_down, norm_mix, w_in, b_gates, m_out_gain,
                  q_norm_gain, k_norm_gain, w_up_m, w_up_a, w_out, norm_ffn2, w_ffn2_gu,
                  w_ffn2_down):
    def gu_chunks(w_gu):
        g = w_gu[:, :D_FF].reshape(D_MODEL, N_FF_CHUNKS, FF_CHUNK)
        u = w_gu[:, D_FF:].reshape(D_MODEL, N_FF_CHUNKS, FF_CHUNK)
        return jnp.transpose(jnp.concatenate([g, u], axis=2), (1, 0, 2)).astype(BF16)

    row = lambda v: v.reshape(1, -1).astype(F32)
    o_qm, o_om = 0, 3 * M_WIDTH
    o_gm = o_om + M_WIDTH
    o_qa = o_gm + 4 * M_HEADS
    o_ka = o_qa + A_WIDTH
    o_va = o_ka + A_KV_HEADS * A_DH
    o_ga = o_va + A_KV_HEADS * A_DH
    o_gb = o_ga + D_MODEL
    cols = lambda a, b: w_in[:, a:b]
    heads = lambda base, j: cols(base + j * A_DH, base + (j + 1) * A_DH)
    wkv = jnp.concatenate([heads(o_ka, 0), heads(o_ka, 0), heads(o_ka, 1), heads(o_ka, 1),
                           heads(o_va, 0), heads(o_va, 0), heads(o_va, 1), heads(o_va, 1)], axis=1)
    n_g = 4 * M_HEADS
    gsum = np.kron(np.eye(LANES // A_DH, dtype=np.float32), np.ones((A_DH, A_DH), np.float32))
    return {
        'nf1': row(norm_ffn1), 'wgu1': gu_chunks(w_ffn1_gu), 'wd1': w_ffn1_down.astype(BF16),
        'nm': row(norm_mix),
        'wqkv': cols(o_qm, o_om).astype(BF16),
        'wo': cols(o_om, o_gm).astype(BF16),
        'wg': jnp.pad(cols(o_gm, o_qa), ((0, 0), (0, MXU_DIM - n_g))).astype(BF16),
        'bg': jnp.pad(row(b_gates), ((0, 0), (0, MXU_DIM - n_g))),
        'wqa': cols(o_qa, o_ka).astype(BF16),
        'wkv': wkv.astype(BF16),
        'qg': jnp.tile(row(q_norm_gain), (1, A_HEADS)),
        'kg': jnp.tile(row(k_norm_gain), (1, LANES // A_DH)),
        'gsum': jnp.asarray(np.concatenate([gsum, gsum], axis=0), BF16),
        'wga': cols(o_ga, o_gb).astype(BF16), 'wgb': cols(o_gb, o_gb + D_MODEL).astype(BF16),
        'mg': row(m_out_gain),
        'wupm': w_up_m.astype(BF16), 'wupa': w_up_a.astype(BF16), 'wout': w_out.astype(BF16),
        'nf2': row(norm_ffn2), 'wgu2': gu_chunks(w_ffn2_gu), 'wd2': w_ffn2_down.astype(BF16),
    }


def _encode(x, w, meta, bias_fn, tile, chunk):
    B, S, _ = x.shape
    meta_qkv, meta_gates, meta_kva = meta
    x1, qkv, gates, qa, kva = _stage_a(x.reshape(B * S, D_MODEL), w, tile)
    hf, hb = _mlstm(qkv.reshape(B, S, -1), gates.reshape(B, S, -1), meta_qkv, meta_gates, chunk)
    at = _attention(qa.reshape(B, S, -1), kva.reshape(B, S, -1), meta_kva, bias_fn(S // BLOCK))
    y = _stage_d(x1, hf.reshape(B * S, -1), hb.reshape(B * S, -1), at.reshape(B * S, -1), w, tile)
    return y.reshape(B, S, D_MODEL)


def _layer(x_groups, meta_tokens, rel_bias, norm_ffn1, w_ffn1_gu, w_ffn1_down, norm_mix, w_in,
           b_gates, m_out_gain, q_norm_gain, k_norm_gain, sink_logits, w_up_m, w_up_a, w_out,
           norm_ffn2, w_ffn2_gu, w_ffn2_down, tile=TOKEN_TILE, chunk=M_CHUNK):
    assert norm_ffn1.shape[0] == 1, "single layer"
    w = _prep_weights(norm_ffn1[0], w_ffn1_gu[0], w_ffn1_down[0], norm_mix[0], w_in[0], b_gates[0],
                      m_out_gain[0], q_norm_gain[0], k_norm_gain[0], w_up_m[0], w_up_a[0],
                      w_out[0], norm_ffn2[0], w_ffn2_gu[0], w_ffn2_down[0])
    _, m_qkv, m_gates, _, m_kva = _stage_a(meta_tokens.astype(F32), w, N_META)
    pad_rows = lambda a: jnp.pad(a, ((0, META_PAD - N_META), (0, 0)))
    meta = (pad_rows(m_qkv), pad_rows(m_gates), m_kva)
    bias_fn = functools.lru_cache(None)(
        lambda nb: _attention_bias(rel_bias, sink_logits[0], nb))
    return tuple(_encode(x, w, meta, bias_fn, tile, chunk) for x in x_groups)


def kernel(x_prompt, x_sample, meta_tokens, rel_bias, norm_ffn1, w_ffn1_gu, w_ffn1_down, norm_mix,
           w_in, b_gates, m_out_gain, q_norm_gain, k_norm_gain, sink_logits, w_up_m, w_up_a, w_out,
           norm_ffn2, w_ffn2_gu, w_ffn2_down):
    return _layer((x_prompt, x_sample), meta_tokens, rel_bias, norm_ffn1, w_ffn1_gu, w_ffn1_down,
                  norm_mix, w_in, b_gates, m_out_gain, q_norm_gain, k_norm_gain, sink_logits,
                  w_up_m, w_up_a, w_out, norm_ffn2, w_ffn2_gu, w_ffn2_down)
```

```python
import functools
import math

import numpy as np
import jax
import jax.numpy as jnp
from jax import lax
from jax.experimental import pallas as pl
from jax.experimental.pallas import tpu as pltpu

F32 = jnp.float32
BF16 = jnp.bfloat16

D_MODEL = 1024
D_FF = 2816
N_META = 16
M_HEADS = 4
M_DK = 128
M_DV = 128
M_WIDTH = M_HEADS * M_DV
A_HEADS = 8
A_KV_HEADS = 2
A_GROUP = A_HEADS // A_KV_HEADS
A_DH = 64
A_WIDTH = A_HEADS * A_DH
WINDOW = 128
BLOCK = 128
N_BUCKETS = 32
MAX_DIST = 128
EPS = 1e-6
NEG = -1e30
LOG2E = 1.4426950408889634

LANES = 128
MXU_DIM = 256
VMEM_LIMIT_BYTES = 58 * 1024 * 1024

FF_CHUNK = MXU_DIM
N_FF_CHUNKS = D_FF // FF_CHUNK
TOKEN_TILE = 512
M_CHUNK = 256
META_PAD = 128
PREP_CHUNKS = 8
M_CHUNKS_PER_STEP = 2
N_KEYS = 3 * BLOCK + N_META + 16

_NN = (((1,), (0,)), ((), ()))
_NT = (((1,), (1,)), ((), ()))


def _dot(a, b, dims=_NN):
    return lax.dot_general(a, b, dims, preferred_element_type=F32)


def _split_dot_lhs(x_f32, a_bf16):
    hi = x_f32.astype(BF16)
    lo = (x_f32 - hi.astype(F32)).astype(BF16)
    return _dot(hi, a_bf16) + _dot(lo, a_bf16)


def _rms(x, g):
    ms = jnp.mean(x * x, axis=-1, keepdims=True)
    return x * lax.rsqrt(ms + EPS) * g


def _sigmoid(x):
    return 1.0 / (1.0 + jnp.exp(-x))


def _log_sigmoid(x):
    return jnp.minimum(x, 0.0) - jnp.log1p(jnp.exp(-jnp.abs(x)))


def _const_spec(shape):
    nd = len(shape)
    return pl.BlockSpec(shape, lambda *_: (0,) * nd, pipeline_mode=pl.Buffered(1))


def _ffn_residual(x, nf_ref, wgu_ref, wd_ref, a_scr):
    xn = _rms(x, nf_ref[...]).astype(BF16)
    for j in range(N_FF_CHUNKS):
        sl = slice(j * FF_CHUNK, (j + 1) * FF_CHUNK)
        g = _dot(xn, wgu_ref[:, sl])
        u = _dot(xn, wgu_ref[:, D_FF + j * FF_CHUNK:D_FF + (j + 1) * FF_CHUNK])
        a_scr[:, sl] = (g * _sigmoid(g) * u).astype(BF16)
    y = _dot(a_scr[...], wd_ref[...])
    return x + 0.5 * y


def _group_sumsq(x, gsum_ref):
    sq = x * x
    hi = sq.astype(BF16)
    lo = (sq - hi.astype(F32)).astype(BF16)
    return _dot(jnp.concatenate([hi, lo], axis=1), gsum_ref[...])


def _lane_halves_tiled(x):
    swapped = pltpu.roll(x, LANES // 2, axis=1)
    low = lax.broadcasted_iota(jnp.int32, x.shape, 1) < LANES // 2
    return jnp.where(low, x, swapped), jnp.where(low, swapped, x)


def _stage_a_kernel(x_ref, nf_ref, wgu_ref, wd_ref, nm_ref, wqkv_ref, bg_ref, wqa_ref,
                    wkvg_ref, qg_ref, kg_ref, gsum_ref,
                    x1_ref, qkv_ref, gate_ref, qa_ref, kva_ref, a_scr):
    x1 = _ffn_residual(x_ref[...], nf_ref, wgu_ref, wd_ref, a_scr)
    x1_ref[...] = x1
    hn = _rms(x1, nm_ref[...]).astype(BF16)
    qkv_ref[:, :M_WIDTH] = (_dot(hn, wqkv_ref[:, :M_WIDTH]) * (M_DK ** -0.5)).astype(BF16)
    qkv_ref[:, M_WIDTH:] = _dot(hn, wqkv_ref[:, M_WIDTH:]).astype(BF16)
    kvg = _dot(hn, wkvg_ref[...])
    gate_ref[...] = (kvg[:, 2 * LANES:3 * LANES] + bg_ref[...])[:, :4 * M_HEADS]
    k = kvg[:, :LANES]
    kn = k * lax.rsqrt(_group_sumsq(k, gsum_ref) * (1.0 / A_DH) + EPS) * kg_ref[...]
    for t, tiled in enumerate(_lane_halves_tiled(kn) + _lane_halves_tiled(kvg[:, LANES:2 * LANES])):
        kva_ref[:, t * LANES:(t + 1) * LANES] = tiled.astype(BF16)
    q_all = _dot(hn, wqa_ref[...])
    for t in range(A_WIDTH // LANES):
        sl = slice(t * LANES, (t + 1) * LANES)
        q = q_all[:, sl]
        ss = _group_sumsq(q, gsum_ref)
        qn = q * lax.rsqrt(ss * (1.0 / A_DH) + EPS) * qg_ref[:, sl] * (A_DH ** -0.5 * LOG2E)
        qa_ref[:, sl] = qn.astype(BF16)


def _stage_a(x2d, w, tile):
    n = x2d.shape[0]
    assert n % tile == 0
    row = lambda width: pl.BlockSpec((tile, width), lambda i: (i, 0))
    out_shape = (
        jax.ShapeDtypeStruct((n, D_MODEL), F32),
        jax.ShapeDtypeStruct((n, 3 * M_WIDTH), BF16),
        jax.ShapeDtypeStruct((n, 4 * M_HEADS), F32),
        jax.ShapeDtypeStruct((n, A_WIDTH), BF16),
        jax.ShapeDtypeStruct((n, 4 * LANES), BF16),
    )
    consts = (w['nf1'], w['wgu1'], w['wd1'], w['nm'], w['wqkv'], w['bg'], w['wqa'],
              w['wkvg'], w['qg'], w['kg'], w['gsum'])
    return pl.pallas_call(
        _stage_a_kernel,
        grid=(n // tile,),
        in_specs=[row(D_MODEL)] + [_const_spec(c.shape) for c in consts],
        out_specs=(row(D_MODEL), row(3 * M_WIDTH), row(4 * M_HEADS), row(A_WIDTH), row(4 * LANES)),
        out_shape=out_shape,
        scratch_shapes=[pltpu.VMEM((tile, D_FF), BF16)],
        compiler_params=pltpu.CompilerParams(dimension_semantics=("parallel",),
                                             vmem_limit_bytes=VMEM_LIMIT_BYTES),
        name="stage_a",
    )(x2d, *consts)


N_UNITS = 2 * M_HEADS


def _running_max(x, L):
    row = lax.broadcasted_iota(jnp.int32, x.shape, 0)
    lane = lax.broadcasted_iota(jnp.int32, x.shape, 1)
    fwd = row < M_HEADS
    k = 1
    while k < L:
        prev = jnp.where(lane >= k, pltpu.roll(x, k, axis=1), NEG)
        nxt = jnp.where(lane < L - k, pltpu.roll(x, L - k, axis=1), NEG)
        x = jnp.maximum(x, jnp.where(fwd, prev, nxt))
        k *= 2
    return x


def _state_update(kt_f32, w_row, vaug, s_old_row, caug):
    wkt = (kt_f32 * w_row).astype(BF16)
    s_old = jnp.concatenate([s_old_row, s_old_row], axis=1)
    return s_old * caug + _dot(wkt, vaug)


def _gate_prep_kernel(g_ref, triu_ref, rows_ref, cols_ref, sc_ref):
    H = M_HEADS
    L = triu_ref.shape[0]
    eye = (lax.broadcasted_iota(jnp.int32, (N_UNITS, LANES), 0)
           == lax.broadcasted_iota(jnp.int32, (N_UNITS, LANES), 1))
    lane_vec = lambda x: jnp.broadcast_to(
        jnp.sum(jnp.where(eye, x, 0.0), axis=0, keepdims=True), (N_UNITS, LANES))
    for j in range(g_ref.shape[2] // L):
        g = g_ref[0, :, j * L:(j + 1) * L]
        lff = _log_sigmoid(g[H:2 * H])
        lfb = _log_sigmoid(g[3 * H:4 * H])
        cs = _split_dot_lhs(jnp.concatenate([lff, lfb], axis=0), triu_ref[...])
        bf = cs[0:H]
        bb = cs[H:2 * H, L - 1:L] - cs[H:2 * H] + lfb
        b = jnp.concatenate([bf, bb], axis=0)
        r = jnp.concatenate([g[0:H], g[2 * H:3 * H]], axis=0) - b
        b_last = jnp.concatenate([jnp.broadcast_to(bf[:, L - 1:L], (H, LANES)),
                                  jnp.broadcast_to(bb[:, 0:1], (H, LANES))], axis=0)
        a = jnp.concatenate([b_last] * (L // LANES), axis=1) + r
        m_loc = jnp.broadcast_to(jnp.max(a, axis=1, keepdims=True), (N_UNITS, LANES))
        rows_ref[0, :, j * L:(j + 1) * L] = jnp.concatenate([r * LOG2E, a], axis=0)
        packed = jnp.concatenate([_running_max(r, L) * LOG2E, b * LOG2E,
                                  jnp.zeros((LANES - 2 * N_UNITS, L), F32)], axis=0)
        cols_ref[0, j * L:(j + 1) * L, :] = jnp.transpose(packed)
        sc_ref[0, j] = jnp.concatenate([b_last, m_loc, lane_vec(b_last), lane_vec(m_loc)], axis=0)


def _gate_prep(gates, chunk, chunks_per_step):
    B, S, G = gates.shape
    L = chunk
    NC = S // L
    J = math.gcd(chunks_per_step, NC)
    gates_t = jnp.transpose(gates, (0, 2, 1))
    triu = jnp.asarray(np.triu(np.ones((L, L), np.float32)), BF16)
    return pl.pallas_call(
        _gate_prep_kernel,
        grid=(B, NC // J),
        in_specs=[pl.BlockSpec((1, G, J * L), lambda b, i: (b, 0, i)),
                  pl.BlockSpec((L, L), lambda b, i: (0, 0))],
        out_specs=(pl.BlockSpec((1, 2 * N_UNITS, J * L), lambda b, i: (b, 0, i)),
                   pl.BlockSpec((1, J * L, LANES), lambda b, i: (b, i, 0)),
                   pl.BlockSpec((1, J, 4 * N_UNITS, LANES), lambda b, i: (b, i, 0, 0))),
        out_shape=(jax.ShapeDtypeStruct((B, 2 * N_UNITS, S), F32),
                   jax.ShapeDtypeStruct((B, S, LANES), F32),
                   jax.ShapeDtypeStruct((B, NC, 4 * N_UNITS, LANES), F32)),
        compiler_params=pltpu.CompilerParams(dimension_semantics=("parallel", "parallel"),
                                             vmem_limit_bytes=VMEM_LIMIT_BYTES),
        name="gate_prep",
    )(gates_t, triu)


def _mlstm_kernel(qf_ref, kf_ref, vf_ref, qb_ref, kb_ref, vb_ref, rf_ref, rb_ref, cf_ref, cb_ref,
                  sf_ref, sb_ref, mk_ref, mv_ref, mg_ref, trim_ref,
                  hf_ref, hb_ref, c_ref, m_ref, *, L):
    H = M_HEADS
    U = N_UNITS
    n_sub = qf_ref.shape[1] // L
    c = pl.program_id(1)

    @pl.when(c == 0)
    def _init():
        c_ref[...] = jnp.zeros_like(c_ref)
        g = mg_ref[...]
        valid = lax.broadcasted_iota(jnp.int32, g.shape, 1) < N_META
        lf = jnp.where(valid, _log_sigmoid(g), 0.0)
        li = jnp.where(valid, g, NEG)
        cs = _split_dot_lhs(lf, trim_ref[...])
        tot = cs[H:2 * H, META_PAD - 1:META_PAD]
        a = tot - cs[H:2 * H] + li[0:H]
        m_new = jnp.maximum(tot, jnp.max(a, axis=1, keepdims=True))
        w = jnp.exp(a - m_new)
        ones_m = jnp.ones((META_PAD, M_DV), BF16)
        for h in range(H):
            sl = slice(h * M_DK, (h + 1) * M_DK)
            vaug = jnp.concatenate([mv_ref[:, sl], ones_m], axis=1)
            kt = jnp.transpose(mk_ref[:, sl].astype(F32))
            c_ref[h] = _state_update(kt, w[h:h + 1], vaug, jnp.zeros((1, LANES), F32),
                                     jnp.zeros(c_ref.shape[1:], F32))
        m_rows = jnp.concatenate([jnp.broadcast_to(m_new, (H, LANES)),
                                  jnp.zeros((H, LANES), F32)], axis=0)
        eye = (lax.broadcasted_iota(jnp.int32, (U, LANES), 0)
               == lax.broadcasted_iota(jnp.int32, (U, LANES), 1))
        m_lanes = jnp.sum(jnp.where(eye, m_rows, 0.0), axis=0, keepdims=True)
        m_ref[...] = jnp.concatenate([m_rows, jnp.broadcast_to(m_lanes, (U, LANES))], axis=0)

    fwd_row = lax.broadcasted_iota(jnp.int32, (U, LANES), 0) < H
    fwd_lane = lax.broadcasted_iota(jnp.int32, (U, LANES), 1) % U < H
    tile_l = lambda x: jnp.concatenate([x] * (L // LANES), axis=1)
    fwd_row_l = tile_l(fwd_row)
    lane_i = lax.broadcasted_iota(jnp.int32, (L, LANES), 1)
    row_i = lax.broadcasted_iota(jnp.int32, (L, L), 0)
    col_i = lax.broadcasted_iota(jnp.int32, (L, L), 1)
    ones = jnp.ones((L, M_DV), BF16)
    m0r = m_ref[0:U]
    m0l = m_ref[U:2 * U]
    for i in range(n_sub):
        jf, jb = i, n_sub - 1 - i
        tf, tb = slice(jf * L, (jf + 1) * L), slice(jb * L, (jb + 1) * L)
        pick_r = lambda n: jnp.where(fwd_row, sf_ref[0, jf, n * U:(n + 1) * U],
                                     sb_ref[0, jb, n * U:(n + 1) * U])
        pick_l = lambda n: jnp.where(fwd_lane, sf_ref[0, jf, n * U:(n + 1) * U],
                                     sb_ref[0, jb, n * U:(n + 1) * U])
        m_new_r = jnp.maximum(pick_r(0) + m0r, pick_r(1))
        m_new_l = jnp.maximum(pick_l(2) + m0l, pick_l(3))
        s_old = jnp.exp(pick_r(0) + m0r - m_new_r)
        r2 = jnp.where(fwd_row_l, rf_ref[0, 0:U, tf], rb_ref[0, 0:U, tb])
        w = jnp.exp(jnp.where(fwd_row_l, rf_ref[0, U:2 * U, tf], rb_ref[0, U:2 * U, tb])
                    - tile_l(m_new_r))
        m2 = m0r * LOG2E
        colsx = jnp.where(lane_i % U < H, cf_ref[0, tf], cb_ref[0, tb])
        g2 = jnp.maximum(colsx, m0l[0:1] * LOG2E)
        e_cols = jnp.exp2(-(pltpu.roll(colsx, LANES - U, axis=1) + g2))
        den = jnp.zeros((L, LANES), F32)
        for u in range(U):
            sl = slice((u % H) * M_DK, (u % H + 1) * M_DK)
            if u < H:
                q, k, v, mask = qf_ref[0, tf, sl], kf_ref[0, tf, sl], vf_ref[0, tf, sl], row_i >= col_i
            else:
                q, k, v, mask = qb_ref[0, tb, sl], kb_ref[0, tb, sl], vb_ref[0, tb, sl], row_i <= col_i
            out, ts = (hf_ref, tf) if u < H else (hb_ref, tb)
            caug = c_ref[u]
            vaug = jnp.concatenate([v, ones], axis=1)
            g_col = g2[:, u:u + 1]
            x_intra = jnp.where(mask, _dot(q, k, _NT) * jnp.exp2(r2[u:u + 1] - g_col), 0.0)
            x_inter = q.astype(F32) * jnp.exp2(m2[u:u + 1] - g_col)
            x = jnp.concatenate([x_intra, x_inter], axis=1).astype(BF16)
            tot = _dot(x, jnp.concatenate([vaug, caug.astype(BF16)], axis=0))
            out[0, ts, sl] = tot[:, :M_DV]
            den = jnp.where(lane_i == u, tot[:, M_DV:], den)
            c_ref[u] = _state_update(jnp.transpose(k.astype(F32)), w[u:u + 1], vaug,
                                     s_old[u:u + 1], caug)
        inv = 1.0 / jnp.maximum(jnp.abs(den), e_cols)
        for u in range(U):
            sl = slice((u % H) * M_DV, (u % H + 1) * M_DV)
            out, ts = (hf_ref, tf) if u < H else (hb_ref, tb)
            out[0, ts, sl] = out[0, ts, sl] * inv[:, u:u + 1]
        m0r, m0l = m_new_r, m_new_l
    m_ref[...] = jnp.concatenate([m0r, m0l], axis=0)


def _mlstm(qkv, gates, meta_qkv, meta_gates, chunk):
    B, S, _ = qkv.shape
    L = chunk
    assert S % L == 0 and L % LANES == 0
    NC = S // L
    rows, cols, sc = _gate_prep(gates, L, PREP_CHUNKS)
    meta_gates_t = jnp.transpose(meta_gates)
    trim = jnp.asarray(np.triu(np.ones((META_PAD, META_PAD), np.float32)), BF16)
    G = 4 * M_HEADS
    J = math.gcd(M_CHUNKS_PER_STEP, NC)
    T = J * L
    NS = NC // J
    fwd = lambda j: pl.BlockSpec((1, T, M_WIDTH), lambda b, c: (b, c, j))
    bwd = lambda j: pl.BlockSpec((1, T, M_WIDTH), lambda b, c: (b, NS - 1 - c, j))
    in_specs = [
        fwd(0), fwd(1), fwd(2), bwd(0), bwd(1), bwd(2),
        pl.BlockSpec((1, 2 * N_UNITS, T), lambda b, c: (b, 0, c)),
        pl.BlockSpec((1, 2 * N_UNITS, T), lambda b, c: (b, 0, NS - 1 - c)),
        pl.BlockSpec((1, T, LANES), lambda b, c: (b, c, 0)),
        pl.BlockSpec((1, T, LANES), lambda b, c: (b, NS - 1 - c, 0)),
        pl.BlockSpec((1, J, 4 * N_UNITS, LANES), lambda b, c: (b, c, 0, 0)),
        pl.BlockSpec((1, J, 4 * N_UNITS, LANES), lambda b, c: (b, NS - 1 - c, 0, 0)),
        pl.BlockSpec((META_PAD, M_WIDTH), lambda b, c: (0, 1)),
        pl.BlockSpec((META_PAD, M_WIDTH), lambda b, c: (0, 2)),
        pl.BlockSpec((G, META_PAD), lambda b, c: (0, 0)),
        pl.BlockSpec((META_PAD, META_PAD), lambda b, c: (0, 0)),
    ]
    out_specs = (pl.BlockSpec((1, T, M_WIDTH), lambda b, c: (b, c, 0)),
                 pl.BlockSpec((1, T, M_WIDTH), lambda b, c: (b, NS - 1 - c, 0)))
    return pl.pallas_call(
        functools.partial(_mlstm_kernel, L=L),
        grid=(B, NS),
        in_specs=in_specs,
        out_specs=out_specs,
        out_shape=(jax.ShapeDtypeStruct((B, S, M_WIDTH), F32),
                   jax.ShapeDtypeStruct((B, S, M_WIDTH), F32)),
        scratch_shapes=[pltpu.VMEM((N_UNITS, M_DK, 2 * M_DV), F32),
                        pltpu.VMEM((2 * N_UNITS, LANES), F32)],
        compiler_params=pltpu.CompilerParams(dimension_semantics=("parallel", "arbitrary"),
                                             vmem_limit_bytes=VMEM_LIMIT_BYTES),
        name="mlstm",
    )(qkv, qkv, qkv, qkv, qkv, qkv, rows, rows, cols, cols, sc, sc,
      meta_qkv, meta_qkv, meta_gates_t, trim)


def _attn_kernel(q_ref, kp_ref, kc_ref, kn_ref, km_ref, bias_ref, o_ref):
    GW = A_GROUP * A_DH
    lane = lax.broadcasted_iota(jnp.int32, (BLOCK, GW), 1)
    head_masks = [((lane >= g * A_DH) & (lane < (g + 1) * A_DH)).astype(F32)
                  for g in range(A_GROUP)]
    pad = jnp.zeros((N_KEYS - 3 * BLOCK - N_META, LANES), BF16)
    for j in range(A_KV_HEADS):
        ksl = slice(j * LANES, (j + 1) * LANES)
        vsl = slice((A_KV_HEADS + j) * LANES, (A_KV_HEADS + j + 1) * LANES)
        qg = q_ref[0, :, j * GW:(j + 1) * GW].astype(F32)
        qs = jnp.concatenate([(qg * m).astype(BF16) for m in head_masks], axis=0)
        k = jnp.concatenate([kp_ref[0, :, ksl], kc_ref[0, :, ksl], kn_ref[0, :, ksl],
                             km_ref[:, ksl], pad], axis=0)
        v = jnp.concatenate([kp_ref[0, :, vsl], kc_ref[0, :, vsl], kn_ref[0, :, vsl],
                             km_ref[:, vsl], pad], axis=0)
        k = jnp.concatenate([k, k], axis=1)
        v = jnp.concatenate([v, v], axis=1)
        s = _dot(qs, k, _NT) + bias_ref[0, j]
        e = jnp.exp2(s - jnp.max(s, axis=1, keepdims=True))
        den = jnp.sum(e, axis=1, keepdims=True)
        o = _dot(e.astype(BF16), v) / den
        og = head_masks[0] * o[:BLOCK]
        for g in range(1, A_GROUP):
            og = og + head_masks[g] * o[g * BLOCK:(g + 1) * BLOCK]
        o_ref[0, :, j * GW:(j + 1) * GW] = og.astype(BF16)


def _attention(qa, kva, meta_kva, bias):
    B, S, _ = qa.shape
    NB = S // BLOCK
    assert S % BLOCK == 0 and NB >= 2
    W = 4 * LANES
    variant = lambda n: jnp.where(n == 0, 0, jnp.where(n == NB - 1, 2, 1))
    in_specs = [
        pl.BlockSpec((1, BLOCK, A_WIDTH), lambda b, n: (b, n, 0)),
        pl.BlockSpec((1, BLOCK, W), lambda b, n: (b, jnp.maximum(n - 1, 0), 0)),
        pl.BlockSpec((1, BLOCK, W), lambda b, n: (b, n, 0)),
        pl.BlockSpec((1, BLOCK, W), lambda b, n: (b, jnp.minimum(n + 1, NB - 1), 0)),
        pl.BlockSpec((N_META, W), lambda b, n: (0, 0)),
        pl.BlockSpec((1, A_KV_HEADS, A_GROUP * BLOCK, N_KEYS), lambda b, n: (variant(n), 0, 0, 0)),
    ]
    return pl.pallas_call(
        _attn_kernel,
        grid=(B, NB),
        in_specs=in_specs,
        out_specs=pl.BlockSpec((1, BLOCK, A_WIDTH), lambda b, n: (b, n, 0)),
        out_shape=jax.ShapeDtypeStruct((B, S, A_WIDTH), BF16),
        compiler_params=pltpu.CompilerParams(dimension_semantics=("parallel", "parallel"),
                                             vmem_limit_bytes=VMEM_LIMIT_BYTES),
        name="window_attn",
    )(qa, kva, kva, kva, meta_kva, bias)


def _t5_bucket_np(rel):
    nb = N_BUCKETS // 2
    max_exact = nb // 2
    ret = np.where(rel > 0, nb, 0)
    n = np.abs(rel)
    nf = np.maximum(n, 1).astype(np.float32)
    large = max_exact + (np.log(nf / max_exact) / math.log(MAX_DIST / max_exact)
                         * (nb - max_exact)).astype(np.int32)
    large = np.minimum(large, nb - 1)
    return ret + np.where(n < max_exact, n, large)


def _attention_bias(rel_bias, sink_logits, n_blocks):
    t = np.arange(BLOCK)
    s_off = np.arange(3 * BLOCK) - BLOCK
    rel_band = s_off[None, :] - t[:, None]
    band_ok = np.abs(rel_band) <= WINDOW
    band_bucket = _t5_bucket_np(rel_band)
    q_pos = N_META + np.arange(n_blocks * BLOCK).reshape(n_blocks, BLOCK)
    meta_bucket = _t5_bucket_np(np.arange(N_META)[None, None, :] - q_pos[..., None])
    assert (meta_bucket[1:] == meta_bucket[1:2]).all()
    table = rel_bias.astype(F32)
    lookup = lambda bucket: jnp.einsum(
        '...k,kh->...h', jnp.asarray(np.eye(N_BUCKETS, dtype=np.float32)[bucket]), table,
        precision=lax.Precision.HIGHEST)
    neg = jnp.full((BLOCK, 3 * BLOCK, A_HEADS), NEG, F32)
    variants = []
    for var in range(3):
        ok = band_ok.copy()
        if var == 0:
            ok[:, :BLOCK] = False
        if var == 2:
            ok[:, 2 * BLOCK:] = False
        band = jnp.where(jnp.asarray(ok)[:, :, None], lookup(band_bucket), neg)
        meta = lookup(meta_bucket[0 if var == 0 else 1])
        sink = jnp.broadcast_to(sink_logits.astype(F32)[None, None, :], (BLOCK, 1, A_HEADS))
        padc = jnp.full((BLOCK, N_KEYS - 3 * BLOCK - N_META - 1, A_HEADS), NEG, F32)
        full = jnp.concatenate([band, meta, sink, padc], axis=1)
        full = jnp.transpose(full, (2, 0, 1)).reshape(A_KV_HEADS, A_GROUP * BLOCK, N_KEYS)
        variants.append(full)
    return jnp.stack(variants) * LOG2E


def _stage_d_kernel(x1_ref, hf_ref, hb_ref, at_ref, nm_ref, wo_ref, mg_ref, wga_ref, wgb_ref,
                    wupm_ref, wupa_ref, wout_ref, nf_ref, wgu_ref, wd_ref, out_ref, a_scr, hm_scr):
    x1 = x1_ref[...]
    hn = _rms(x1, nm_ref[...]).astype(BF16)
    o = _dot(hn, wo_ref[...])
    for h in range(M_HEADS):
        sl = slice(h * M_DV, (h + 1) * M_DV)
        hs = hf_ref[:, sl] + hb_ref[:, sl]
        hm_scr[:, sl] = (_rms(hs, mg_ref[:, sl]) * _sigmoid(o[:, sl])).astype(BF16)
    mixed = _sigmoid(_dot(hn, wga_ref[...])) * _dot(hm_scr[...], wupm_ref[...])
    mixed = mixed + _sigmoid(_dot(hn, wgb_ref[...])) * _dot(at_ref[...], wupa_ref[...])
    x2 = x1 + _dot(mixed.astype(BF16), wout_ref[...])
    out_ref[...] = _ffn_residual(x2, nf_ref, wgu_ref, wd_ref, a_scr)


def _stage_d(x1, hf, hb, at, w, tile):
    n = x1.shape[0]
    row = lambda width: pl.BlockSpec((tile, width), lambda i: (i, 0))
    consts = (w['nm'], w['wo'], w['mg'], w['wga'], w['wgb'], w['wupm'], w['wupa'], w['wout'],
              w['nf2'], w['wgu2'], w['wd2'])
    return pl.pallas_call(
        _stage_d_kernel,
        grid=(n // tile,),
        in_specs=[row(D_MODEL), row(M_WIDTH), row(M_WIDTH), row(A_WIDTH)]
                 + [_const_spec(c.shape) for c in consts],
        out_specs=row(D_MODEL),
        out_shape=jax.ShapeDtypeStruct((n, D_MODEL), F32),
        scratch_shapes=[pltpu.VMEM((tile, D_FF), BF16), pltpu.VMEM((tile, M_WIDTH), BF16)],
        compiler_params=pltpu.CompilerParams(dimension_semantics=("parallel",),
                                             vmem_limit_bytes=VMEM_LIMIT_BYTES),
        name="stage_d",
    )(x1, hf, hb, at, *consts)


def _prep_weights(norm_ffn1, w_ffn1_gu, w_ffn1_down, norm_mix, w_in, b_gates, m_out_gain,
                  q_norm_gain, k_norm_gain, w_up_m, w_up_a, w_out, norm_ffn2, w_ffn2_gu,
                  w_ffn2_down):
    row = lambda v: v.reshape(1, -1).astype(F32)
    o_qm, o_om = 0, 3 * M_WIDTH
    o_gm = o_om + M_WIDTH
    o_qa = o_gm + 4 * M_HEADS
    o_ka = o_qa + A_WIDTH
    o_va = o_ka + A_KV_HEADS * A_DH
    o_ga = o_va + A_KV_HEADS * A_DH
    o_gb = o_ga + D_MODEL
    cols = lambda a, b: w_in[:, a:b]
    n_g = 4 * M_HEADS
    wkvg = jnp.pad(jnp.concatenate([cols(o_ka, o_ga), cols(o_gm, o_qa)], axis=1),
                   ((0, 0), (0, MXU_DIM - n_g)))
    gsum = np.kron(np.eye(LANES // A_DH, dtype=np.float32), np.ones((A_DH, A_DH), np.float32))
    return {
        'nf1': row(norm_ffn1), 'wgu1': w_ffn1_gu.astype(BF16), 'wd1': w_ffn1_down.astype(BF16),
        'nm': row(norm_mix),
        'wqkv': cols(o_qm, o_om).astype(BF16),
        'wo': cols(o_om, o_gm).astype(BF16),
        'bg': jnp.pad(row(b_gates), ((0, 0), (0, LANES - n_g))),
        'wqa': cols(o_qa, o_ka).astype(BF16),
        'wkvg': wkvg.astype(BF16),
        'qg': jnp.tile(row(q_norm_gain), (1, A_HEADS)),
        'kg': jnp.tile(row(k_norm_gain), (1, LANES // A_DH)),
        'gsum': jnp.asarray(np.concatenate([gsum, gsum], axis=0), BF16),
        'wga': cols(o_ga, o_gb).astype(BF16), 'wgb': cols(o_gb, o_gb + D_MODEL).astype(BF16),
        'mg': row(m_out_gain),
        'wupm': w_up_m.astype(BF16), 'wupa': w_up_a.astype(BF16), 'wout': w_out.astype(BF16),
        'nf2': row(norm_ffn2), 'wgu2': w_ffn2_gu.astype(BF16), 'wd2': w_ffn2_down.astype(BF16),
    }


def _encode(x, w, meta, bias_fn, tile, chunk):
    B, S, _ = x.shape
    meta_qkv, meta_gates, meta_kva = meta
    x1, qkv, gates, qa, kva = _stage_a(x.reshape(B * S, D_MODEL), w, tile)
    hf, hb = _mlstm(qkv.reshape(B, S, -1), gates.reshape(B, S, -1), meta_qkv, meta_gates, chunk)
    at = _attention(qa.reshape(B, S, -1), kva.reshape(B, S, -1), meta_kva, bias_fn(S // BLOCK))
    y = _stage_d(x1, hf.reshape(B * S, -1), hb.reshape(B * S, -1), at.reshape(B * S, -1), w, tile)
    return y.reshape(B, S, D_MODEL)


def _layer(x_groups, meta_tokens, rel_bias, norm_ffn1, w_ffn1_gu, w_ffn1_down, norm_mix, w_in,
           b_gates, m_out_gain, q_norm_gain, k_norm_gain, sink_logits, w_up_m, w_up_a, w_out,
           norm_ffn2, w_ffn2_gu, w_ffn2_down, tile=TOKEN_TILE, chunk=M_CHUNK):
    assert norm_ffn1.shape[0] == 1, "single layer"
    w = _prep_weights(norm_ffn1[0], w_ffn1_gu[0], w_ffn1_down[0], norm_mix[0], w_in[0], b_gates[0],
                      m_out_gain[0], q_norm_gain[0], k_norm_gain[0], w_up_m[0], w_up_a[0],
                      w_out[0], norm_ffn2[0], w_ffn2_gu[0], w_ffn2_down[0])
    _, m_qkv, m_gates, _, m_kva = _stage_a(meta_tokens.astype(F32), w, N_META)
    pad_rows = lambda a: jnp.pad(a, ((0, META_PAD - N_META), (0, 0)))
    meta = (pad_rows(m_qkv), pad_rows(m_gates), m_kva)
    bias_fn = functools.lru_cache(None)(
        lambda nb: _attention_bias(rel_bias, sink_logits[0], nb))
    return tuple(_encode(x, w, meta, bias_fn, tile, chunk) for x in x_groups)


def kernel(x_prompt, x_sample, meta_tokens, rel_bias, norm_ffn1, w_ffn1_gu, w_ffn1_down, norm_mix,
           w_in, b_gates, m_out_gain, q_norm_gain, k_norm_gain, sink_logits, w_up_m, w_up_a, w_out,
           norm_ffn2, w_ffn2_gu, w_ffn2_down):
    return _layer((x_prompt, x_sample), meta_tokens, rel_bias, norm_ffn1, w_ffn1_gu, w_ffn1_down,
                  norm_mix, w_in, b_gates, m_out_gain, q_norm_gain, k_norm_gain, sink_logits,
                  w_up_m, w_up_a, w_out, norm_ffn2, w_ffn2_gu, w_ffn2_down)
```

```python
import functools
import math

import numpy as np
import jax
import jax.numpy as jnp
from jax import lax
from jax.experimental import pallas as pl
from jax.experimental.pallas import tpu as pltpu

F32 = jnp.float32
BF16 = jnp.bfloat16

D_MODEL = 1024
D_FF = 2816
N_META = 16
M_HEADS = 4
M_DK = 128
M_DV = 128
M_WIDTH = M_HEADS * M_DV
A_HEADS = 8
A_KV_HEADS = 2
A_GROUP = A_HEADS // A_KV_HEADS
A_DH = 64
A_WIDTH = A_HEADS * A_DH
WINDOW = 128
BLOCK = 128
N_BUCKETS = 32
MAX_DIST = 128
EPS = 1e-6
NEG = -1e30
LOG2E = 1.4426950408889634

LANES = 128
MXU_DIM = 256
VMEM_LIMIT_BYTES = 58 * 1024 * 1024

FF_CHUNK = MXU_DIM
N_FF_CHUNKS = D_FF // FF_CHUNK
TOKEN_TILE = 512
M_CHUNK = 256
META_PAD = 128
PREP_CHUNKS = 8
M_CHUNKS_PER_STEP = 2
A_BLOCKS_PER_STEP = 2

_NN = (((1,), (0,)), ((), ()))
_NT = (((1,), (1,)), ((), ()))


def _dot(a, b, dims=_NN):
    return lax.dot_general(a, b, dims, preferred_element_type=F32)


def _split_dot_lhs(x_f32, a_bf16):
    hi = x_f32.astype(BF16)
    lo = (x_f32 - hi.astype(F32)).astype(BF16)
    return _dot(hi, a_bf16) + _dot(lo, a_bf16)


def _rms(x, g):
    ms = jnp.mean(x * x, axis=-1, keepdims=True)
    return x * lax.rsqrt(ms + EPS) * g


def _sigmoid(x):
    return 1.0 / (1.0 + jnp.exp(-x))


def _log_sigmoid(x):
    return jnp.minimum(x, 0.0) - jnp.log1p(jnp.exp(-jnp.abs(x)))


def _const_spec(shape):
    nd = len(shape)
    return pl.BlockSpec(shape, lambda *_: (0,) * nd, pipeline_mode=pl.Buffered(1))


def _ffn_residual(x, nf_ref, wgu_ref, wd_ref, a_scr):
    xn = _rms(x, nf_ref[...]).astype(BF16)
    for j in range(N_FF_CHUNKS):
        sl = slice(j * FF_CHUNK, (j + 1) * FF_CHUNK)
        g = _dot(xn, wgu_ref[:, sl])
        u = _dot(xn, wgu_ref[:, D_FF + j * FF_CHUNK:D_FF + (j + 1) * FF_CHUNK])
        a_scr[:, sl] = (g * _sigmoid(g) * u).astype(BF16)
    y = _dot(a_scr[...], wd_ref[...])
    return x + 0.5 * y


def _group_sumsq(x, gsum_ref):
    sq = x * x
    hi = sq.astype(BF16)
    lo = (sq - hi.astype(F32)).astype(BF16)
    return _dot(jnp.concatenate([hi, lo], axis=1), gsum_ref[...])


def _lane_halves_tiled(x):
    swapped = pltpu.roll(x, LANES // 2, axis=1)
    low = lax.broadcasted_iota(jnp.int32, x.shape, 1) < LANES // 2
    return jnp.where(low, x, swapped), jnp.where(low, swapped, x)


def _stage_a_kernel(x_ref, nf_ref, wgu_ref, wd_ref, nm_ref, wqkv_ref, bg_ref, wqa_ref,
                    wkvg_ref, qg_ref, kg_ref, gsum_ref,
                    x1_ref, qkv_ref, gate_ref, qa_ref, kva_ref, a_scr):
    x1 = _ffn_residual(x_ref[...], nf_ref, wgu_ref, wd_ref, a_scr)
    x1_ref[...] = x1
    hn = _rms(x1, nm_ref[...]).astype(BF16)
    qkv_ref[:, :M_WIDTH] = (_dot(hn, wqkv_ref[:, :M_WIDTH]) * (M_DK ** -0.5)).astype(BF16)
    qkv_ref[:, M_WIDTH:] = _dot(hn, wqkv_ref[:, M_WIDTH:]).astype(BF16)
    kvg = _dot(hn, wkvg_ref[...])
    gate_ref[...] = (kvg[:, 2 * LANES:3 * LANES] + bg_ref[...])[:, :4 * M_HEADS]
    k = kvg[:, :LANES]
    kn = k * lax.rsqrt(_group_sumsq(k, gsum_ref) * (1.0 / A_DH) + EPS) * kg_ref[...]
    for t, tiled in enumerate(_lane_halves_tiled(kn) + _lane_halves_tiled(kvg[:, LANES:2 * LANES])):
        kva_ref[:, t * LANES:(t + 1) * LANES] = tiled.astype(BF16)
    q_all = _dot(hn, wqa_ref[...])
    for t in range(A_WIDTH // LANES):
        sl = slice(t * LANES, (t + 1) * LANES)
        q = q_all[:, sl]
        ss = _group_sumsq(q, gsum_ref)
        qn = q * lax.rsqrt(ss * (1.0 / A_DH) + EPS) * qg_ref[:, sl] * (A_DH ** -0.5 * LOG2E)
        qa_ref[:, sl] = qn.astype(BF16)


def _stage_a(x2d, w, tile):
    n = x2d.shape[0]
    assert n % tile == 0
    row = lambda width: pl.BlockSpec((tile, width), lambda i: (i, 0))
    out_shape = (
        jax.ShapeDtypeStruct((n, D_MODEL), F32),
        jax.ShapeDtypeStruct((n, 3 * M_WIDTH), BF16),
        jax.ShapeDtypeStruct((n, 4 * M_HEADS), F32),
        jax.ShapeDtypeStruct((n, A_WIDTH), BF16),
        jax.ShapeDtypeStruct((n, 4 * LANES), BF16),
    )
    consts = (w['nf1'], w['wgu1'], w['wd1'], w['nm'], w['wqkv'], w['bg'], w['wqa'],
              w['wkvg'], w['qg'], w['kg'], w['gsum'])
    return pl.pallas_call(
        _stage_a_kernel,
        grid=(n // tile,),
        in_specs=[row(D_MODEL)] + [_const_spec(c.shape) for c in consts],
        out_specs=(row(D_MODEL), row(3 * M_WIDTH), row(4 * M_HEADS), row(A_WIDTH), row(4 * LANES)),
        out_shape=out_shape,
        scratch_shapes=[pltpu.VMEM((tile, D_FF), BF16)],
        compiler_params=pltpu.CompilerParams(dimension_semantics=("parallel",),
                                             vmem_limit_bytes=VMEM_LIMIT_BYTES),
        name="stage_a",
    )(x2d, *consts)


N_UNITS = 2 * M_HEADS


def _running_max(x, L):
    row = lax.broadcasted_iota(jnp.int32, x.shape, 0)
    lane = lax.broadcasted_iota(jnp.int32, x.shape, 1)
    fwd = row < M_HEADS
    k = 1
    while k < L:
        prev = jnp.where(lane >= k, pltpu.roll(x, k, axis=1), NEG)
        nxt = jnp.where(lane < L - k, pltpu.roll(x, L - k, axis=1), NEG)
        x = jnp.maximum(x, jnp.where(fwd, prev, nxt))
        k *= 2
    return x


def _state_update(kt_f32, w_row, vaug, s_old_row, caug):
    wkt = (kt_f32 * w_row).astype(BF16)
    s_old = jnp.concatenate([s_old_row, s_old_row], axis=1)
    return s_old * caug + _dot(wkt, vaug)


def _gate_prep_kernel(g_ref, triu_ref, rows_ref, cols_ref, sc_ref):
    H = M_HEADS
    L = triu_ref.shape[0]
    eye = (lax.broadcasted_iota(jnp.int32, (N_UNITS, LANES), 0)
           == lax.broadcasted_iota(jnp.int32, (N_UNITS, LANES), 1))
    lane_vec = lambda x: jnp.broadcast_to(
        jnp.sum(jnp.where(eye, x, 0.0), axis=0, keepdims=True), (N_UNITS, LANES))
    for j in range(g_ref.shape[2] // L):
        g = g_ref[0, :, j * L:(j + 1) * L]
        lff = _log_sigmoid(g[H:2 * H])
        lfb = _log_sigmoid(g[3 * H:4 * H])
        cs = _split_dot_lhs(jnp.concatenate([lff, lfb], axis=0), triu_ref[...])
        bf = cs[0:H]
        bb = cs[H:2 * H, L - 1:L] - cs[H:2 * H] + lfb
        b = jnp.concatenate([bf, bb], axis=0)
        r = jnp.concatenate([g[0:H], g[2 * H:3 * H]], axis=0) - b
        b_last = jnp.concatenate([jnp.broadcast_to(bf[:, L - 1:L], (H, LANES)),
                                  jnp.broadcast_to(bb[:, 0:1], (H, LANES))], axis=0)
        a = jnp.concatenate([b_last] * (L // LANES), axis=1) + r
        m_loc = jnp.broadcast_to(jnp.max(a, axis=1, keepdims=True), (N_UNITS, LANES))
        rows_ref[0, :, j * L:(j + 1) * L] = jnp.concatenate([r * LOG2E, a], axis=0)
        packed = jnp.concatenate([_running_max(r, L) * LOG2E, b * LOG2E,
                                  jnp.zeros((LANES - 2 * N_UNITS, L), F32)], axis=0)
        cols_ref[0, j * L:(j + 1) * L, :] = jnp.transpose(packed)
        sc_ref[0, j] = jnp.concatenate([b_last, m_loc, lane_vec(b_last), lane_vec(m_loc)], axis=0)


def _gate_prep(gates, chunk, chunks_per_step):
    B, S, G = gates.shape
    L = chunk
    NC = S // L
    J = math.gcd(chunks_per_step, NC)
    gates_t = jnp.transpose(gates, (0, 2, 1))
    triu = jnp.asarray(np.triu(np.ones((L, L), np.float32)), BF16)
    return pl.pallas_call(
        _gate_prep_kernel,
        grid=(B, NC // J),
        in_specs=[pl.BlockSpec((1, G, J * L), lambda b, i: (b, 0, i)),
                  pl.BlockSpec((L, L), lambda b, i: (0, 0))],
        out_specs=(pl.BlockSpec((1, 2 * N_UNITS, J * L), lambda b, i: (b, 0, i)),
                   pl.BlockSpec((1, J * L, LANES), lambda b, i: (b, i, 0)),
                   pl.BlockSpec((1, J, 4 * N_UNITS, LANES), lambda b, i: (b, i, 0, 0))),
        out_shape=(jax.ShapeDtypeStruct((B, 2 * N_UNITS, S), F32),
                   jax.ShapeDtypeStruct((B, S, LANES), F32),
                   jax.ShapeDtypeStruct((B, NC, 4 * N_UNITS, LANES), F32)),
        compiler_params=pltpu.CompilerParams(dimension_semantics=("parallel", "parallel"),
                                             vmem_limit_bytes=VMEM_LIMIT_BYTES),
        name="gate_prep",
    )(gates_t, triu)


def _mlstm_kernel(qf_ref, kf_ref, vf_ref, qb_ref, kb_ref, vb_ref, rf_ref, rb_ref, cf_ref, cb_ref,
                  sf_ref, sb_ref, mk_ref, mv_ref, mg_ref, trim_ref,
                  hf_ref, hb_ref, c_ref, m_ref, *, L):
    H = M_HEADS
    U = N_UNITS
    n_sub = qf_ref.shape[1] // L
    c = pl.program_id(1)

    @pl.when(c == 0)
    def _init():
        c_ref[...] = jnp.zeros_like(c_ref)
        g = mg_ref[...]
        valid = lax.broadcasted_iota(jnp.int32, g.shape, 1) < N_META
        lf = jnp.where(valid, _log_sigmoid(g), 0.0)
        li = jnp.where(valid, g, NEG)
        cs = _split_dot_lhs(lf, trim_ref[...])
        tot = cs[H:2 * H, META_PAD - 1:META_PAD]
        a = tot - cs[H:2 * H] + li[0:H]
        m_new = jnp.maximum(tot, jnp.max(a, axis=1, keepdims=True))
        w = jnp.exp(a - m_new)
        ones_m = jnp.ones((META_PAD, M_DV), BF16)
        for h in range(H):
            sl = slice(h * M_DK, (h + 1) * M_DK)
            vaug = jnp.concatenate([mv_ref[:, sl], ones_m], axis=1)
            kt = jnp.transpose(mk_ref[:, sl].astype(F32))
            c_ref[h] = _state_update(kt, w[h:h + 1], vaug, jnp.zeros((1, LANES), F32),
                                     jnp.zeros(c_ref.shape[1:], F32))
        m_rows = jnp.concatenate([jnp.broadcast_to(m_new, (H, LANES)),
                                  jnp.zeros((H, LANES), F32)], axis=0)
        eye = (lax.broadcasted_iota(jnp.int32, (U, LANES), 0)
               == lax.broadcasted_iota(jnp.int32, (U, LANES), 1))
        m_lanes = jnp.sum(jnp.where(eye, m_rows, 0.0), axis=0, keepdims=True)
        m_ref[...] = jnp.concatenate([m_rows, jnp.broadcast_to(m_lanes, (U, LANES))], axis=0)

    fwd_row = lax.broadcasted_iota(jnp.int32, (U, LANES), 0) < H
    fwd_lane = lax.broadcasted_iota(jnp.int32, (U, LANES), 1) % U < H
    tile_l = lambda x: jnp.concatenate([x] * (L // LANES), axis=1)
    fwd_row_l = tile_l(fwd_row)
    lane_i = lax.broadcasted_iota(jnp.int32, (L, LANES), 1)
    row_i = lax.broadcasted_iota(jnp.int32, (L, L), 0)
    col_i = lax.broadcasted_iota(jnp.int32, (L, L), 1)
    ones = jnp.ones((L, M_DV), BF16)
    m0r = m_ref[0:U]
    m0l = m_ref[U:2 * U]
    for i in range(n_sub):
        jf, jb = i, n_sub - 1 - i
        tf, tb = slice(jf * L, (jf + 1) * L), slice(jb * L, (jb + 1) * L)
        pick_r = lambda n: jnp.where(fwd_row, sf_ref[0, jf, n * U:(n + 1) * U],
                                     sb_ref[0, jb, n * U:(n + 1) * U])
        pick_l = lambda n: jnp.where(fwd_lane, sf_ref[0, jf, n * U:(n + 1) * U],
                                     sb_ref[0, jb, n * U:(n + 1) * U])
        m_new_r = jnp.maximum(pick_r(0) + m0r, pick_r(1))
        m_new_l = jnp.maximum(pick_l(2) + m0l, pick_l(3))
        s_old = jnp.exp(pick_r(0) + m0r - m_new_r)
        r2 = jnp.where(fwd_row_l, rf_ref[0, 0:U, tf], rb_ref[0, 0:U, tb])
        w = jnp.exp(jnp.where(fwd_row_l, rf_ref[0, U:2 * U, tf], rb_ref[0, U:2 * U, tb])
                    - tile_l(m_new_r))
        m2 = m0r * LOG2E
        colsx = jnp.where(lane_i % U < H, cf_ref[0, tf], cb_ref[0, tb])
        g2 = jnp.maximum(colsx, m0l[0:1] * LOG2E)
        e_cols = jnp.exp2(-(pltpu.roll(colsx, LANES - U, axis=1) + g2))
        den = jnp.zeros((L, LANES), F32)
        for u in range(U):
            sl = slice((u % H) * M_DK, (u % H + 1) * M_DK)
            if u < H:
                q, k, v, mask = qf_ref[0, tf, sl], kf_ref[0, tf, sl], vf_ref[0, tf, sl], row_i >= col_i
            else:
                q, k, v, mask = qb_ref[0, tb, sl], kb_ref[0, tb, sl], vb_ref[0, tb, sl], row_i <= col_i
            out, ts = (hf_ref, tf) if u < H else (hb_ref, tb)
            caug = c_ref[u]
            vaug = jnp.concatenate([v, ones], axis=1)
            g_col = g2[:, u:u + 1]
            x_intra = jnp.where(mask, _dot(q, k, _NT) * jnp.exp2(r2[u:u + 1] - g_col), 0.0)
            x_inter = q.astype(F32) * jnp.exp2(m2[u:u + 1] - g_col)
            x = jnp.concatenate([x_intra, x_inter], axis=1).astype(BF16)
            tot = _dot(x, jnp.concatenate([vaug, caug.astype(BF16)], axis=0))
            out[0, ts, sl] = tot[:, :M_DV]
            den = jnp.where(lane_i == u, tot[:, M_DV:], den)
            c_ref[u] = _state_update(jnp.transpose(k.astype(F32)), w[u:u + 1], vaug,
                                     s_old[u:u + 1], caug)
        inv = 1.0 / jnp.maximum(jnp.abs(den), e_cols)
        for u in range(U):
            sl = slice((u % H) * M_DV, (u % H + 1) * M_DV)
            out, ts = (hf_ref, tf) if u < H else (hb_ref, tb)
            out[0, ts, sl] = out[0, ts, sl] * inv[:, u:u + 1]
        m0r, m0l = m_new_r, m_new_l
    m_ref[...] = jnp.concatenate([m0r, m0l], axis=0)


def _mlstm(qkv, gates, meta_qkv, meta_gates, chunk):
    B, S, _ = qkv.shape
    L = chunk
    assert S % L == 0 and L % LANES == 0
    NC = S // L
    rows, cols, sc = _gate_prep(gates, L, PREP_CHUNKS)
    meta_gates_t = jnp.transpose(meta_gates)
    trim = jnp.asarray(np.triu(np.ones((META_PAD, META_PAD), np.float32)), BF16)
    G = 4 * M_HEADS
    J = math.gcd(M_CHUNKS_PER_STEP, NC)
    T = J * L
    NS = NC // J
    fwd = lambda j: pl.BlockSpec((1, T, M_WIDTH), lambda b, c: (b, c, j))
    bwd = lambda j: pl.BlockSpec((1, T, M_WIDTH), lambda b, c: (b, NS - 1 - c, j))
    in_specs = [
        fwd(0), fwd(1), fwd(2), bwd(0), bwd(1), bwd(2),
        pl.BlockSpec((1, 2 * N_UNITS, T), lambda b, c: (b, 0, c)),
        pl.BlockSpec((1, 2 * N_UNITS, T), lambda b, c: (b, 0, NS - 1 - c)),
        pl.BlockSpec((1, T, LANES), lambda b, c: (b, c, 0)),
        pl.BlockSpec((1, T, LANES), lambda b, c: (b, NS - 1 - c, 0)),
        pl.BlockSpec((1, J, 4 * N_UNITS, LANES), lambda b, c: (b, c, 0, 0)),
        pl.BlockSpec((1, J, 4 * N_UNITS, LANES), lambda b, c: (b, NS - 1 - c, 0, 0)),
        pl.BlockSpec((META_PAD, M_WIDTH), lambda b, c: (0, 1)),
        pl.BlockSpec((META_PAD, M_WIDTH), lambda b, c: (0, 2)),
        pl.BlockSpec((G, META_PAD), lambda b, c: (0, 0)),
        pl.BlockSpec((META_PAD, META_PAD), lambda b, c: (0, 0)),
    ]
    out_specs = (pl.BlockSpec((1, T, M_WIDTH), lambda b, c: (b, c, 0)),
                 pl.BlockSpec((1, T, M_WIDTH), lambda b, c: (b, NS - 1 - c, 0)))
    return pl.pallas_call(
        functools.partial(_mlstm_kernel, L=L),
        grid=(B, NS),
        in_specs=in_specs,
        out_specs=out_specs,
        out_shape=(jax.ShapeDtypeStruct((B, S, M_WIDTH), F32),
                   jax.ShapeDtypeStruct((B, S, M_WIDTH), F32)),
        scratch_shapes=[pltpu.VMEM((N_UNITS, M_DK, 2 * M_DV), F32),
                        pltpu.VMEM((2 * N_UNITS, LANES), F32)],
        compiler_params=pltpu.CompilerParams(dimension_semantics=("parallel", "arbitrary"),
                                             vmem_limit_bytes=VMEM_LIMIT_BYTES),
        name="mlstm",
    )(qkv, qkv, qkv, qkv, qkv, qkv, rows, rows, cols, cols, sc, sc,
      meta_qkv, meta_qkv, meta_gates_t, trim)


def _attn_kernel(q_ref, kp_ref, kc_ref, kn_ref, km_ref, bias_ref, o_ref, *, n_blocks):
    GW = A_GROUP * A_DH
    HB = BLOCK // 2
    n_q = q_ref.shape[1] // BLOCK
    lane = lax.broadcasted_iota(jnp.int32, (HB, GW), 1)
    head_masks = [((lane >= g * A_DH) & (lane < (g + 1) * A_DH)).astype(F32)
                  for g in range(A_GROUP)]
    wide = lambda x: jnp.concatenate([x, x], axis=1)
    zeros = jnp.zeros((HB - N_META, LANES), BF16)
    for i in range(n_q):
        blk = pl.program_id(1) * n_q + i
        var = jnp.where(blk == 0, 0, jnp.where(blk == n_blocks - 1, 2, 1))
        cur = kc_ref[0, i * BLOCK:(i + 1) * BLOCK]
        prev = kp_ref[0] if i == 0 else kc_ref[0, (i - 1) * BLOCK:i * BLOCK]
        nxt = kn_ref[0] if i == n_q - 1 else kc_ref[0, (i + 1) * BLOCK:(i + 2) * BLOCK]
        for j in range(A_KV_HEADS):
            for kind in range(2):
                sl = slice((kind * A_KV_HEADS + j) * LANES, (kind * A_KV_HEADS + j + 1) * LANES)
                extra = jnp.concatenate([km_ref[:, sl], zeros], axis=0)
                first = wide(jnp.concatenate([prev[:, sl], cur[:, sl], nxt[:HB, sl], extra], axis=0))
                second = wide(jnp.concatenate([extra, prev[HB:, sl], cur[:, sl], nxt[:, sl]], axis=0))
                if kind == 0:
                    keys = (first, second)
                else:
                    vals = (first, second)
            for h in range(2):
                rows = slice(i * BLOCK + h * HB, i * BLOCK + (h + 1) * HB)
                qg = q_ref[0, rows, j * GW:(j + 1) * GW].astype(F32)
                qs = jnp.concatenate([(qg * m).astype(BF16) for m in head_masks], axis=0)
                s = _dot(qs, keys[h], _NT) + bias_ref[var, j, h]
                e = jnp.exp2(s - jnp.max(s, axis=1, keepdims=True))
                den = jnp.sum(e, axis=1, keepdims=True)
                o = _dot(e.astype(BF16), vals[h]) / den
                og = o[(A_GROUP - 1) * HB:]
                for g in range(A_GROUP - 2, -1, -1):
                    og = jnp.where(lane < (g + 1) * A_DH, o[g * HB:(g + 1) * HB], og)
                o_ref[0, rows, j * GW:(j + 1) * GW] = og.astype(BF16)


def _attention(qa, kva, meta_kva, bias):
    B, S, _ = qa.shape
    NB = S // BLOCK
    assert S % BLOCK == 0 and NB >= 2
    W = 4 * LANES
    Q = math.gcd(A_BLOCKS_PER_STEP, NB)
    in_specs = [
        pl.BlockSpec((1, Q * BLOCK, A_WIDTH), lambda b, n: (b, n, 0)),
        pl.BlockSpec((1, BLOCK, W), lambda b, n: (b, jnp.maximum(n * Q - 1, 0), 0)),
        pl.BlockSpec((1, Q * BLOCK, W), lambda b, n: (b, n, 0)),
        pl.BlockSpec((1, BLOCK, W), lambda b, n: (b, jnp.minimum((n + 1) * Q, NB - 1), 0)),
        pl.BlockSpec((N_META, W), lambda b, n: (0, 0)),
        _const_spec(bias.shape),
    ]
    return pl.pallas_call(
        functools.partial(_attn_kernel, n_blocks=NB),
        grid=(B, NB // Q),
        in_specs=in_specs,
        out_specs=pl.BlockSpec((1, Q * BLOCK, A_WIDTH), lambda b, n: (b, n, 0)),
        out_shape=jax.ShapeDtypeStruct((B, S, A_WIDTH), BF16),
        compiler_params=pltpu.CompilerParams(dimension_semantics=("parallel", "parallel"),
                                             vmem_limit_bytes=VMEM_LIMIT_BYTES),
        name="window_attn",
    )(qa, kva, kva, kva, meta_kva, bias)


def _t5_bucket_np(rel):
    nb = N_BUCKETS // 2
    max_exact = nb // 2
    ret = np.where(rel > 0, nb, 0)
    n = np.abs(rel)
    nf = np.maximum(n, 1).astype(np.float32)
    large = max_exact + (np.log(nf / max_exact) / math.log(MAX_DIST / max_exact)
                         * (nb - max_exact)).astype(np.int32)
    large = np.minimum(large, nb - 1)
    return ret + np.where(n < max_exact, n, large)


def _attention_bias(rel_bias, sink_logits, n_blocks):
    HB = BLOCK // 2
    t = np.arange(BLOCK)
    s_off = np.arange(3 * BLOCK) - BLOCK
    rel_band = s_off[None, :] - t[:, None]
    band_ok = np.abs(rel_band) <= WINDOW
    band_bucket = _t5_bucket_np(rel_band)
    q_pos = N_META + np.arange(n_blocks * BLOCK).reshape(n_blocks, BLOCK)
    meta_bucket = _t5_bucket_np(np.arange(N_META)[None, None, :] - q_pos[..., None])
    assert (meta_bucket[1:] == meta_bucket[1:2]).all()
    table = rel_bias.astype(F32)
    lookup = lambda bucket: jnp.einsum(
        '...k,kh->...h', jnp.asarray(np.eye(N_BUCKETS, dtype=np.float32)[bucket]), table,
        precision=lax.Precision.HIGHEST)
    neg = jnp.full((BLOCK, 3 * BLOCK, A_HEADS), NEG, F32)
    variants = []
    for var in range(3):
        ok = band_ok.copy()
        if var == 0:
            ok[:, :BLOCK] = False
        if var == 2:
            ok[:, 2 * BLOCK:] = False
        band = jnp.where(jnp.asarray(ok)[:, :, None], lookup(band_bucket), neg)
        meta = lookup(meta_bucket[0 if var == 0 else 1])
        sink = jnp.broadcast_to(sink_logits.astype(F32)[None, None, :], (BLOCK, 1, A_HEADS))
        padc = jnp.full((BLOCK, HB - N_META - 1, A_HEADS), NEG, F32)
        extra = jnp.concatenate([meta, sink, padc], axis=1)
        halves = [jnp.concatenate([band[:HB, :3 * BLOCK - HB], extra[:HB]], axis=1),
                  jnp.concatenate([extra[HB:], band[HB:, HB:]], axis=1)]
        full = jnp.transpose(jnp.stack(halves), (3, 0, 1, 2))
        full = full.reshape(A_KV_HEADS, A_GROUP, 2, HB, 3 * BLOCK)
        full = jnp.transpose(full, (0, 2, 1, 3, 4)).reshape(A_KV_HEADS, 2, A_GROUP * HB, 3 * BLOCK)
        variants.append(full)
    return jnp.stack(variants) * LOG2E


def _stage_d_kernel(x1_ref, hf_ref, hb_ref, at_ref, nm_ref, wo_ref, mg_ref, wga_ref, wgb_ref,
                    wupm_ref, wupa_ref, wout_ref, nf_ref, wgu_ref, wd_ref, out_ref, a_scr, hm_scr):
    x1 = x1_ref[...]
    hn = _rms(x1, nm_ref[...]).astype(BF16)
    o = _dot(hn, wo_ref[...])
    for h in range(M_HEADS):
        sl = slice(h * M_DV, (h + 1) * M_DV)
        hs = hf_ref[:, sl] + hb_ref[:, sl]
        hm_scr[:, sl] = (_rms(hs, mg_ref[:, sl]) * _sigmoid(o[:, sl])).astype(BF16)
    mixed = _sigmoid(_dot(hn, wga_ref[...])) * _dot(hm_scr[...], wupm_ref[...])
    mixed = mixed + _sigmoid(_dot(hn, wgb_ref[...])) * _dot(at_ref[...], wupa_ref[...])
    x2 = x1 + _dot(mixed.astype(BF16), wout_ref[...])
    out_ref[...] = _ffn_residual(x2, nf_ref, wgu_ref, wd_ref, a_scr)


def _stage_d(x1, hf, hb, at, w, tile):
    n = x1.shape[0]
    row = lambda width: pl.BlockSpec((tile, width), lambda i: (i, 0))
    consts = (w['nm'], w['wo'], w['mg'], w['wga'], w['wgb'], w['wupm'], w['wupa'], w['wout'],
              w['nf2'], w['wgu2'], w['wd2'])
    return pl.pallas_call(
        _stage_d_kernel,
        grid=(n // tile,),
        in_specs=[row(D_MODEL), row(M_WIDTH), row(M_WIDTH), row(A_WIDTH)]
                 + [_const_spec(c.shape) for c in consts],
        out_specs=row(D_MODEL),
        out_shape=jax.ShapeDtypeStruct((n, D_MODEL), F32),
        scratch_shapes=[pltpu.VMEM((tile, D_FF), BF16), pltpu.VMEM((tile, M_WIDTH), BF16)],
        compiler_params=pltpu.CompilerParams(dimension_semantics=("parallel",),
                                             vmem_limit_bytes=VMEM_LIMIT_BYTES),
        name="stage_d",
    )(x1, hf, hb, at, *consts)


def _prep_weights(norm_ffn1, w_ffn1_gu, w_ffn1_down, norm_mix, w_in, b_gates, m_out_gain,
                  q_norm_gain, k_norm_gain, w_up_m, w_up_a, w_out, norm_ffn2, w_ffn2_gu,
                  w_ffn2_down):
    row = lambda v: v.reshape(1, -1).astype(F32)
    o_qm, o_om = 0, 3 * M_WIDTH
    o_gm = o_om + M_WIDTH
    o_qa = o_gm + 4 * M_HEADS
    o_ka = o_qa + A_WIDTH
    o_va = o_ka + A_KV_HEADS * A_DH
    o_ga = o_va + A_KV_HEADS * A_DH
    o_gb = o_ga + D_MODEL
    cols = lambda a, b: w_in[:, a:b]
    n_g = 4 * M_HEADS
    wkvg = jnp.pad(jnp.concatenate([cols(o_ka, o_ga), cols(o_gm, o_qa)], axis=1),
                   ((0, 0), (0, MXU_DIM - n_g)))
    gsum = np.kron(np.eye(LANES // A_DH, dtype=np.float32), np.ones((A_DH, A_DH), np.float32))
    return {
        'nf1': row(norm_ffn1), 'wgu1': w_ffn1_gu.astype(BF16), 'wd1': w_ffn1_down.astype(BF16),
        'nm': row(norm_mix),
        'wqkv': cols(o_qm, o_om).astype(BF16),
        'wo': cols(o_om, o_gm).astype(BF16),
        'bg': jnp.pad(row(b_gates), ((0, 0), (0, LANES - n_g))),
        'wqa': cols(o_qa, o_ka).astype(BF16),
        'wkvg': wkvg.astype(BF16),
        'qg': jnp.tile(row(q_norm_gain), (1, A_HEADS)),
        'kg': jnp.tile(row(k_norm_gain), (1, LANES // A_DH)),
        'gsum': jnp.asarray(np.concatenate([gsum, gsum], axis=0), BF16),
        'wga': cols(o_ga, o_gb).astype(BF16), 'wgb': cols(o_gb, o_gb + D_MODEL).astype(BF16),
        'mg': row(m_out_gain),
        'wupm': w_up_m.astype(BF16), 'wupa': w_up_a.astype(BF16), 'wout': w_out.astype(BF16),
        'nf2': row(norm_ffn2), 'wgu2': w_ffn2_gu.astype(BF16), 'wd2': w_ffn2_down.astype(BF16),
    }


def _encode(x, w, meta, bias_fn, tile, chunk):
    B, S, _ = x.shape
    meta_qkv, meta_gates, meta_kva = meta
    x1, qkv, gates, qa, kva = _stage_a(x.reshape(B * S, D_MODEL), w, tile)
    hf, hb = _mlstm(qkv.reshape(B, S, -1), gates.reshape(B, S, -1), meta_qkv, meta_gates, chunk)
    at = _attention(qa.reshape(B, S, -1), kva.reshape(B, S, -1), meta_kva, bias_fn(S // BLOCK))
    y = _stage_d(x1, hf.reshape(B * S, -1), hb.reshape(B * S, -1), at.reshape(B * S, -1), w, tile)
    return y.reshape(B, S, D_MODEL)


def _layer(x_groups, meta_tokens, rel_bias, norm_ffn1, w_ffn1_gu, w_ffn1_down, norm_mix, w_in,
           b_gates, m_out_gain, q_norm_gain, k_norm_gain, sink_logits, w_up_m, w_up_a, w_out,
           norm_ffn2, w_ffn2_gu, w_ffn2_down, tile=TOKEN_TILE, chunk=M_CHUNK):
    assert norm_ffn1.shape[0] == 1, "single layer"
    w = _prep_weights(norm_ffn1[0], w_ffn1_gu[0], w_ffn1_down[0], norm_mix[0], w_in[0], b_gates[0],
                      m_out_gain[0], q_norm_gain[0], k_norm_gain[0], w_up_m[0], w_up_a[0],
                      w_out[0], norm_ffn2[0], w_ffn2_gu[0], w_ffn2_down[0])
    _, m_qkv, m_gates, _, m_kva = _stage_a(meta_tokens.astype(F32), w, N_META)
    pad_rows = lambda a: jnp.pad(a, ((0, META_PAD - N_META), (0, 0)))
    meta = (pad_rows(m_qkv), pad_rows(m_gates), m_kva)
    bias_fn = functools.lru_cache(None)(
        lambda nb: _attention_bias(rel_bias, sink_logits[0], nb))
    return tuple(_encode(x, w, meta, bias_fn, tile, chunk) for x in x_groups)


def kernel(x_prompt, x_sample, meta_tokens, rel_bias, norm_ffn1, w_ffn1_gu, w_ffn1_down, norm_mix,
           w_in, b_gates, m_out_gain, q_norm_gain, k_norm_gain, sink_logits, w_up_m, w_up_a, w_out,
           norm_ffn2, w_ffn2_gu, w_ffn2_down):
    return _layer((x_prompt, x_sample), meta_tokens, rel_bias, norm_ffn1, w_ffn1_gu, w_ffn1_down,
                  norm_mix, w_in, b_gates, m_out_gain, q_norm_gain, k_norm_gain, sink_logits,
                  w_up_m, w_up_a, w_out, norm_ffn2, w_ffn2_gu, w_ffn2_down)
```

```python
import functools
import math

import numpy as np
import jax
import jax.numpy as jnp
from jax import lax
from jax.experimental import pallas as pl
from jax.experimental.pallas import tpu as pltpu

F32 = jnp.float32
BF16 = jnp.bfloat16

D_MODEL = 1024
D_FF = 2816
N_META = 16
M_HEADS = 4
M_DK = 128
M_DV = 128
M_WIDTH = M_HEADS * M_DV
A_HEADS = 8
A_KV_HEADS = 2
A_GROUP = A_HEADS // A_KV_HEADS
A_DH = 64
A_WIDTH = A_HEADS * A_DH
WINDOW = 128
BLOCK = 128
N_BUCKETS = 32
MAX_DIST = 128
EPS = 1e-6
NEG = -1e30
LOG2E = 1.4426950408889634

LANES = 128
MXU_DIM = 256
VMEM_LIMIT_BYTES = 58 * 1024 * 1024

FF_CHUNK = MXU_DIM
N_FF_CHUNKS = D_FF // FF_CHUNK
TOKEN_TILE = 512
M_CHUNK = 256
META_PAD = 128
PREP_CHUNKS = 8
M_CHUNKS_PER_STEP = 4
A_BLOCKS_PER_STEP = 8
KVA_WIDTH = 2 * A_KV_HEADS * LANES

_NN = (((1,), (0,)), ((), ()))
_NT = (((1,), (1,)), ((), ()))


def _dot(a, b, dims=_NN):
    return lax.dot_general(a, b, dims, preferred_element_type=F32)


def _split_dot_lhs(x_f32, a_bf16):
    hi = x_f32.astype(BF16)
    lo = (x_f32 - hi.astype(F32)).astype(BF16)
    return _dot(hi, a_bf16) + _dot(lo, a_bf16)


def _rms(x, g):
    ms = jnp.mean(x * x, axis=-1, keepdims=True)
    return x * lax.rsqrt(ms + EPS) * g


def _sigmoid(x):
    return 1.0 / (1.0 + jnp.exp(-x))


def _log_sigmoid(x):
    return jnp.minimum(x, 0.0) - jnp.log1p(jnp.exp(-jnp.abs(x)))


def _const_spec(shape):
    nd = len(shape)
    return pl.BlockSpec(shape, lambda *_: (0,) * nd, pipeline_mode=pl.Buffered(1))


def _ffn_residual(x, nf_ref, wgu_ref, wd_ref, a_scr):
    xn = _rms(x, nf_ref[...]).astype(BF16)
    for j in range(N_FF_CHUNKS):
        sl = slice(j * FF_CHUNK, (j + 1) * FF_CHUNK)
        g = _dot(xn, wgu_ref[:, sl])
        u = _dot(xn, wgu_ref[:, D_FF + j * FF_CHUNK:D_FF + (j + 1) * FF_CHUNK])
        a_scr[:, sl] = (g * _sigmoid(g) * u).astype(BF16)
    y = _dot(a_scr[...], wd_ref[...])
    return x + 0.5 * y


def _group_sumsq(x, gsum_ref):
    sq = x * x
    hi = sq.astype(BF16)
    lo = (sq - hi.astype(F32)).astype(BF16)
    return _dot(jnp.concatenate([hi, lo], axis=1), gsum_ref[...])


def _lane_halves_tiled(x):
    swapped = pltpu.roll(x, LANES // 2, axis=1)
    low = lax.broadcasted_iota(jnp.int32, x.shape, 1) < LANES // 2
    return jnp.where(low, x, swapped), jnp.where(low, swapped, x)


def _stage_a_kernel(x_ref, nf_ref, wgu_ref, wd_ref, nm_ref, wqkv_ref, bg_ref, wqa_ref,
                    wkvg_ref, qg_ref, kg_ref, gsum_ref,
                    x1_ref, qkv_ref, gate_ref, qa_ref, kva_ref, a_scr):
    x1 = _ffn_residual(x_ref[...], nf_ref, wgu_ref, wd_ref, a_scr)
    x1_ref[...] = x1
    hn = _rms(x1, nm_ref[...]).astype(BF16)
    qkv_ref[:, :M_WIDTH] = (_dot(hn, wqkv_ref[:, :M_WIDTH]) * (M_DK ** -0.5)).astype(BF16)
    qkv_ref[:, M_WIDTH:] = _dot(hn, wqkv_ref[:, M_WIDTH:]).astype(BF16)
    kvg = _dot(hn, wkvg_ref[...])
    gate_ref[...] = (kvg[:, 2 * LANES:3 * LANES] + bg_ref[...])[:, :4 * M_HEADS]
    k = kvg[:, :LANES]
    kn = k * lax.rsqrt(_group_sumsq(k, gsum_ref) * (1.0 / A_DH) + EPS) * kg_ref[...]
    for t, tiled in enumerate(_lane_halves_tiled(kn) + _lane_halves_tiled(kvg[:, LANES:2 * LANES])):
        kva_ref[:, t * LANES:(t + 1) * LANES] = tiled.astype(BF16)
    q_all = _dot(hn, wqa_ref[...])
    for t in range(A_WIDTH // LANES):
        sl = slice(t * LANES, (t + 1) * LANES)
        q = q_all[:, sl]
        ss = _group_sumsq(q, gsum_ref)
        qn = q * lax.rsqrt(ss * (1.0 / A_DH) + EPS) * qg_ref[:, sl] * (A_DH ** -0.5 * LOG2E)
        qa_ref[:, sl] = qn.astype(BF16)


def _stage_a(x2d, w, tile):
    n = x2d.shape[0]
    assert n % tile == 0
    row = lambda width: pl.BlockSpec((tile, width), lambda i: (i, 0))
    out_shape = (
        jax.ShapeDtypeStruct((n, D_MODEL), F32),
        jax.ShapeDtypeStruct((n, 3 * M_WIDTH), BF16),
        jax.ShapeDtypeStruct((n, 4 * M_HEADS), F32),
        jax.ShapeDtypeStruct((n, A_WIDTH), BF16),
        jax.ShapeDtypeStruct((n, KVA_WIDTH), BF16),
    )
    consts = (w['nf1'], w['wgu1'], w['wd1'], w['nm'], w['wqkv'], w['bg'], w['wqa'],
              w['wkvg'], w['qg'], w['kg'], w['gsum'])
    return pl.pallas_call(
        _stage_a_kernel,
        grid=(n // tile,),
        in_specs=[row(D_MODEL)] + [_const_spec(c.shape) for c in consts],
        out_specs=(row(D_MODEL), row(3 * M_WIDTH), row(4 * M_HEADS), row(A_WIDTH), row(KVA_WIDTH)),
        out_shape=out_shape,
        scratch_shapes=[pltpu.VMEM((tile, D_FF), BF16)],
        compiler_params=pltpu.CompilerParams(dimension_semantics=("parallel",),
                                             vmem_limit_bytes=VMEM_LIMIT_BYTES),
        name="stage_a",
    )(x2d, *consts)


N_UNITS = 2 * M_HEADS


def _running_max(x, L):
    row = lax.broadcasted_iota(jnp.int32, x.shape, 0)
    lane = lax.broadcasted_iota(jnp.int32, x.shape, 1)
    fwd = row < M_HEADS
    k = 1
    while k < L:
        prev = jnp.where(lane >= k, pltpu.roll(x, k, axis=1), NEG)
        nxt = jnp.where(lane < L - k, pltpu.roll(x, L - k, axis=1), NEG)
        x = jnp.maximum(x, jnp.where(fwd, prev, nxt))
        k *= 2
    return x


def _state_update(kt_f32, w_row, vaug, s_old_row, caug):
    wkt = (kt_f32 * w_row).astype(BF16)
    s_old = jnp.concatenate([s_old_row, s_old_row], axis=1)
    return s_old * caug + _dot(wkt, vaug)


def _gate_prep_kernel(g_ref, triu_ref, rows_ref, cols_ref, sc_ref):
    H = M_HEADS
    L = triu_ref.shape[0]
    eye = (lax.broadcasted_iota(jnp.int32, (N_UNITS, LANES), 0)
           == lax.broadcasted_iota(jnp.int32, (N_UNITS, LANES), 1))
    lane_vec = lambda x: jnp.broadcast_to(
        jnp.sum(jnp.where(eye, x, 0.0), axis=0, keepdims=True), (N_UNITS, LANES))
    for j in range(g_ref.shape[2] // L):
        g = g_ref[0, :, j * L:(j + 1) * L]
        lff = _log_sigmoid(g[H:2 * H])
        lfb = _log_sigmoid(g[3 * H:4 * H])
        cs = _split_dot_lhs(jnp.concatenate([lff, lfb], axis=0), triu_ref[...])
        bf = cs[0:H]
        bb = cs[H:2 * H, L - 1:L] - cs[H:2 * H] + lfb
        b = jnp.concatenate([bf, bb], axis=0)
        r = jnp.concatenate([g[0:H], g[2 * H:3 * H]], axis=0) - b
        b_last = jnp.concatenate([jnp.broadcast_to(bf[:, L - 1:L], (H, LANES)),
                                  jnp.broadcast_to(bb[:, 0:1], (H, LANES))], axis=0)
        a = jnp.concatenate([b_last] * (L // LANES), axis=1) + r
        m_loc = jnp.broadcast_to(jnp.max(a, axis=1, keepdims=True), (N_UNITS, LANES))
        rows_ref[0, :, j * L:(j + 1) * L] = jnp.concatenate([r * LOG2E, a], axis=0)
        packed = jnp.concatenate([_running_max(r, L) * LOG2E, b * LOG2E,
                                  jnp.zeros((LANES - 2 * N_UNITS, L), F32)], axis=0)
        cols_ref[0, j * L:(j + 1) * L, :] = jnp.transpose(packed)
        sc_ref[0, j] = jnp.concatenate([b_last, m_loc, lane_vec(b_last), lane_vec(m_loc)], axis=0)


def _gate_prep(gates, chunk, chunks_per_step):
    B, S, G = gates.shape
    L = chunk
    NC = S // L
    J = math.gcd(chunks_per_step, NC)
    gates_t = jnp.transpose(gates, (0, 2, 1))
    triu = jnp.asarray(np.triu(np.ones((L, L), np.float32)), BF16)
    return pl.pallas_call(
        _gate_prep_kernel,
        grid=(B, NC // J),
        in_specs=[pl.BlockSpec((1, G, J * L), lambda b, i: (b, 0, i)),
                  pl.BlockSpec((L, L), lambda b, i: (0, 0))],
        out_specs=(pl.BlockSpec((1, 2 * N_UNITS, J * L), lambda b, i: (b, 0, i)),
                   pl.BlockSpec((1, J * L, LANES), lambda b, i: (b, i, 0)),
                   pl.BlockSpec((1, J, 4 * N_UNITS, LANES), lambda b, i: (b, i, 0, 0))),
        out_shape=(jax.ShapeDtypeStruct((B, 2 * N_UNITS, S), F32),
                   jax.ShapeDtypeStruct((B, S, LANES), F32),
                   jax.ShapeDtypeStruct((B, NC, 4 * N_UNITS, LANES), F32)),
        compiler_params=pltpu.CompilerParams(dimension_semantics=("parallel", "parallel"),
                                             vmem_limit_bytes=VMEM_LIMIT_BYTES),
        name="gate_prep",
    )(gates_t, triu)


def _mlstm_kernel(qf_ref, kf_ref, vf_ref, qb_ref, kb_ref, vb_ref, rf_ref, rb_ref, cf_ref, cb_ref,
                  sf_ref, sb_ref, mk_ref, mv_ref, mg_ref, trim_ref,
                  hf_ref, hb_ref, c_ref, m_ref, *, L):
    H = M_HEADS
    U = N_UNITS
    n_sub = qf_ref.shape[1] // L
    c = pl.program_id(1)

    @pl.when(c == 0)
    def _init():
        c_ref[...] = jnp.zeros_like(c_ref)
        g = mg_ref[...]
        valid = lax.broadcasted_iota(jnp.int32, g.shape, 1) < N_META
        lf = jnp.where(valid, _log_sigmoid(g), 0.0)
        li = jnp.where(valid, g, NEG)
        cs = _split_dot_lhs(lf, trim_ref[...])
        tot = cs[H:2 * H, META_PAD - 1:META_PAD]
        a = tot - cs[H:2 * H] + li[0:H]
        m_new = jnp.maximum(tot, jnp.max(a, axis=1, keepdims=True))
        w = jnp.exp(a - m_new)
        ones_m = jnp.ones((META_PAD, M_DV), BF16)
        for h in range(H):
            sl = slice(h * M_DK, (h + 1) * M_DK)
            vaug = jnp.concatenate([mv_ref[:, sl], ones_m], axis=1)
            kt = jnp.transpose(mk_ref[:, sl].astype(F32))
            c_ref[h] = _state_update(kt, w[h:h + 1], vaug, jnp.zeros((1, LANES), F32),
                                     jnp.zeros(c_ref.shape[1:], F32))
        m_rows = jnp.concatenate([jnp.broadcast_to(m_new, (H, LANES)),
                                  jnp.zeros((H, LANES), F32)], axis=0)
        eye = (lax.broadcasted_iota(jnp.int32, (U, LANES), 0)
               == lax.broadcasted_iota(jnp.int32, (U, LANES), 1))
        m_lanes = jnp.sum(jnp.where(eye, m_rows, 0.0), axis=0, keepdims=True)
        m_ref[...] = jnp.concatenate([m_rows, jnp.broadcast_to(m_lanes, (U, LANES))], axis=0)

    fwd_row = lax.broadcasted_iota(jnp.int32, (U, LANES), 0) < H
    fwd_lane = lax.broadcasted_iota(jnp.int32, (U, LANES), 1) % U < H
    tile_l = lambda x: jnp.concatenate([x] * (L // LANES), axis=1)
    fwd_row_l = tile_l(fwd_row)
    lane_i = lax.broadcasted_iota(jnp.int32, (L, LANES), 1)
    row_i = lax.broadcasted_iota(jnp.int32, (L, L), 0)
    col_i = lax.broadcasted_iota(jnp.int32, (L, L), 1)
    ones = jnp.ones((L, M_DV), BF16)
    m0r = m_ref[0:U]
    m0l = m_ref[U:2 * U]
    for i in range(n_sub):
        jf, jb = i, n_sub - 1 - i
        tf, tb = slice(jf * L, (jf + 1) * L), slice(jb * L, (jb + 1) * L)
        pick_r = lambda n: jnp.where(fwd_row, sf_ref[0, jf, n * U:(n + 1) * U],
                                     sb_ref[0, jb, n * U:(n + 1) * U])
        pick_l = lambda n: jnp.where(fwd_lane, sf_ref[0, jf, n * U:(n + 1) * U],
                                     sb_ref[0, jb, n * U:(n + 1) * U])
        m_new_r = jnp.maximum(pick_r(0) + m0r, pick_r(1))
        m_new_l = jnp.maximum(pick_l(2) + m0l, pick_l(3))
        s_old = jnp.exp(pick_r(0) + m0r - m_new_r)
        r2 = jnp.where(fwd_row_l, rf_ref[0, 0:U, tf], rb_ref[0, 0:U, tb])
        w = jnp.exp(jnp.where(fwd_row_l, rf_ref[0, U:2 * U, tf], rb_ref[0, U:2 * U, tb])
                    - tile_l(m_new_r))
        m2 = m0r * LOG2E
        colsx = jnp.where(lane_i % U < H, cf_ref[0, tf], cb_ref[0, tb])
        g2 = jnp.maximum(colsx, m0l[0:1] * LOG2E)
        e_cols = jnp.exp2(-(pltpu.roll(colsx, LANES - U, axis=1) + g2))
        den = jnp.zeros((L, LANES), F32)
        for u in range(U):
            sl = slice((u % H) * M_DK, (u % H + 1) * M_DK)
            if u < H:
                q, k, v, mask = qf_ref[0, tf, sl], kf_ref[0, tf, sl], vf_ref[0, tf, sl], row_i >= col_i
            else:
                q, k, v, mask = qb_ref[0, tb, sl], kb_ref[0, tb, sl], vb_ref[0, tb, sl], row_i <= col_i
            out, ts = (hf_ref, tf) if u < H else (hb_ref, tb)
            caug = c_ref[u]
            vaug = jnp.concatenate([v, ones], axis=1)
            g_col = g2[:, u:u + 1]
            x_intra = jnp.where(mask, _dot(q, k, _NT) * jnp.exp2(r2[u:u + 1] - g_col), 0.0)
            x_inter = q.astype(F32) * jnp.exp2(m2[u:u + 1] - g_col)
            x = jnp.concatenate([x_intra, x_inter], axis=1).astype(BF16)
            tot = _dot(x, jnp.concatenate([vaug, caug.astype(BF16)], axis=0))
            out[0, ts, sl] = tot[:, :M_DV]
            den = jnp.where(lane_i == u, tot[:, M_DV:], den)
            c_ref[u] = _state_update(jnp.transpose(k.astype(F32)), w[u:u + 1], vaug,
                                     s_old[u:u + 1], caug)
        inv = 1.0 / jnp.maximum(jnp.abs(den), e_cols)
        for u in range(U):
            sl = slice((u % H) * M_DV, (u % H + 1) * M_DV)
            out, ts = (hf_ref, tf) if u < H else (hb_ref, tb)
            out[0, ts, sl] = out[0, ts, sl] * inv[:, u:u + 1]
        m0r, m0l = m_new_r, m_new_l
    m_ref[...] = jnp.concatenate([m0r, m0l], axis=0)


def _mlstm(qkv, gates, meta_qkv, meta_gates, chunk):
    B, S, _ = qkv.shape
    L = chunk
    assert S % L == 0 and L % LANES == 0
    NC = S // L
    rows, cols, sc = _gate_prep(gates, L, PREP_CHUNKS)
    meta_gates_t = jnp.transpose(meta_gates)
    trim = jnp.asarray(np.triu(np.ones((META_PAD, META_PAD), np.float32)), BF16)
    G = 4 * M_HEADS
    J = math.gcd(M_CHUNKS_PER_STEP, NC)
    T = J * L
    NS = NC // J
    fwd = lambda j: pl.BlockSpec((1, T, M_WIDTH), lambda b, c: (b, c, j))
    bwd = lambda j: pl.BlockSpec((1, T, M_WIDTH), lambda b, c: (b, NS - 1 - c, j))
    in_specs = [
        fwd(0), fwd(1), fwd(2), bwd(0), bwd(1), bwd(2),
        pl.BlockSpec((1, 2 * N_UNITS, T), lambda b, c: (b, 0, c)),
        pl.BlockSpec((1, 2 * N_UNITS, T), lambda b, c: (b, 0, NS - 1 - c)),
        pl.BlockSpec((1, T, LANES), lambda b, c: (b, c, 0)),
        pl.BlockSpec((1, T, LANES), lambda b, c: (b, NS - 1 - c, 0)),
        pl.BlockSpec((1, J, 4 * N_UNITS, LANES), lambda b, c: (b, c, 0, 0)),
        pl.BlockSpec((1, J, 4 * N_UNITS, LANES), lambda b, c: (b, NS - 1 - c, 0, 0)),
        pl.BlockSpec((META_PAD, M_WIDTH), lambda b, c: (0, 1)),
        pl.BlockSpec((META_PAD, M_WIDTH), lambda b, c: (0, 2)),
        pl.BlockSpec((G, META_PAD), lambda b, c: (0, 0)),
        pl.BlockSpec((META_PAD, META_PAD), lambda b, c: (0, 0)),
    ]
    out_specs = (pl.BlockSpec((1, T, M_WIDTH), lambda b, c: (b, c, 0)),
                 pl.BlockSpec((1, T, M_WIDTH), lambda b, c: (b, NS - 1 - c, 0)))
    return pl.pallas_call(
        functools.partial(_mlstm_kernel, L=L),
        grid=(B, NS),
        in_specs=in_specs,
        out_specs=out_specs,
        out_shape=(jax.ShapeDtypeStruct((B, S, M_WIDTH), F32),
                   jax.ShapeDtypeStruct((B, S, M_WIDTH), F32)),
        scratch_shapes=[pltpu.VMEM((N_UNITS, M_DK, 2 * M_DV), F32),
                        pltpu.VMEM((2 * N_UNITS, LANES), F32)],
        compiler_params=pltpu.CompilerParams(dimension_semantics=("parallel", "arbitrary"),
                                             vmem_limit_bytes=VMEM_LIMIT_BYTES),
        name="mlstm",
    )(qkv, qkv, qkv, qkv, qkv, qkv, rows, rows, cols, cols, sc, sc,
      meta_qkv, meta_qkv, meta_gates_t, trim)


def _attn_kernel(q_ref, kp_ref, kc_ref, kn_ref, km_ref, bias_ref, o_ref, *, n_blocks):
    GW = A_GROUP * A_DH
    HB = BLOCK // 2
    n_q = q_ref.shape[1] // BLOCK
    lane = lax.broadcasted_iota(jnp.int32, (HB, GW), 1)
    head_masks = [((lane >= g * A_DH) & (lane < (g + 1) * A_DH)).astype(F32)
                  for g in range(A_GROUP)]
    wide = lambda x: jnp.concatenate([x, x], axis=1)
    zeros = jnp.zeros((HB - N_META, LANES), BF16)
    for i in range(n_q):
        blk = pl.program_id(1) * n_q + i
        var = jnp.where(blk == 0, 0, jnp.where(blk == n_blocks - 1, 2, 1))
        cur = kc_ref[0, i * BLOCK:(i + 1) * BLOCK]
        prev = kp_ref[0] if i == 0 else kc_ref[0, (i - 1) * BLOCK:i * BLOCK]
        nxt = kn_ref[0] if i == n_q - 1 else kc_ref[0, (i + 1) * BLOCK:(i + 2) * BLOCK]
        for j in range(A_KV_HEADS):
            for kind in range(2):
                sl = slice((kind * A_KV_HEADS + j) * LANES, (kind * A_KV_HEADS + j + 1) * LANES)
                extra = jnp.concatenate([km_ref[:, sl], zeros], axis=0)
                first = wide(jnp.concatenate([prev[:, sl], cur[:, sl], nxt[:HB, sl], extra], axis=0))
                second = wide(jnp.concatenate([extra, prev[HB:, sl], cur[:, sl], nxt[:, sl]], axis=0))
                if kind == 0:
                    keys = (first, second)
                else:
                    vals = (first, second)
            for h in range(2):
                rows = slice(i * BLOCK + h * HB, i * BLOCK + (h + 1) * HB)
                qg = q_ref[0, rows, j * GW:(j + 1) * GW].astype(F32)
                qs = jnp.concatenate([(qg * m).astype(BF16) for m in head_masks], axis=0)
                s = _dot(qs, keys[h], _NT) + bias_ref[var, j, h]
                e = jnp.exp2(s - jnp.max(s, axis=1, keepdims=True))
                den = jnp.sum(e, axis=1, keepdims=True)
                o = _dot(e.astype(BF16), vals[h]) / den
                og = o[(A_GROUP - 1) * HB:]
                for g in range(A_GROUP - 2, -1, -1):
                    og = jnp.where(lane < (g + 1) * A_DH, o[g * HB:(g + 1) * HB], og)
                o_ref[0, rows, j * GW:(j + 1) * GW] = og.astype(BF16)


def _attention(qa, kva, meta_kva, bias):
    B, S, _ = qa.shape
    NB = S // BLOCK
    assert S % BLOCK == 0 and NB >= 2
    W = KVA_WIDTH
    Q = math.gcd(A_BLOCKS_PER_STEP, NB)
    in_specs = [
        pl.BlockSpec((1, Q * BLOCK, A_WIDTH), lambda b, n: (b, n, 0)),
        pl.BlockSpec((1, BLOCK, W), lambda b, n: (b, jnp.maximum(n * Q - 1, 0), 0)),
        pl.BlockSpec((1, Q * BLOCK, W), lambda b, n: (b, n, 0)),
        pl.BlockSpec((1, BLOCK, W), lambda b, n: (b, jnp.minimum((n + 1) * Q, NB - 1), 0)),
        pl.BlockSpec((N_META, W), lambda b, n: (0, 0)),
        _const_spec(bias.shape),
    ]
    return pl.pallas_call(
        functools.partial(_attn_kernel, n_blocks=NB),
        grid=(B, NB // Q),
        in_specs=in_specs,
        out_specs=pl.BlockSpec((1, Q * BLOCK, A_WIDTH), lambda b, n: (b, n, 0)),
        out_shape=jax.ShapeDtypeStruct((B, S, A_WIDTH), BF16),
        compiler_params=pltpu.CompilerParams(dimension_semantics=("parallel", "parallel"),
                                             vmem_limit_bytes=VMEM_LIMIT_BYTES),
        name="window_attn",
    )(qa, kva, kva, kva, meta_kva, bias)


def _t5_bucket_np(rel):
    nb = N_BUCKETS // 2
    max_exact = nb // 2
    ret = np.where(rel > 0, nb, 0)
    n = np.abs(rel)
    nf = np.maximum(n, 1).astype(np.float32)
    large = max_exact + (np.log(nf / max_exact) / math.log(MAX_DIST / max_exact)
                         * (nb - max_exact)).astype(np.int32)
    large = np.minimum(large, nb - 1)
    return ret + np.where(n < max_exact, n, large)


def _attention_bias(rel_bias, sink_logits, n_blocks):
    HB = BLOCK // 2
    t = np.arange(BLOCK)
    s_off = np.arange(3 * BLOCK) - BLOCK
    rel_band = s_off[None, :] - t[:, None]
    band_ok = np.abs(rel_band) <= WINDOW
    band_bucket = _t5_bucket_np(rel_band)
    q_pos = N_META + np.arange(n_blocks * BLOCK).reshape(n_blocks, BLOCK)
    meta_bucket = _t5_bucket_np(np.arange(N_META)[None, None, :] - q_pos[..., None])
    assert (meta_bucket[1:] == meta_bucket[1:2]).all()
    table = rel_bias.astype(F32)
    lookup = lambda bucket: jnp.einsum(
        '...k,kh->...h', jnp.asarray(np.eye(N_BUCKETS, dtype=np.float32)[bucket]), table,
        precision=lax.Precision.HIGHEST)
    neg = jnp.full((BLOCK, 3 * BLOCK, A_HEADS), NEG, F32)
    variants = []
    for var in range(3):
        ok = band_ok.copy()
        if var == 0:
            ok[:, :BLOCK] = False
        if var == 2:
            ok[:, 2 * BLOCK:] = False
        band = jnp.where(jnp.asarray(ok)[:, :, None], lookup(band_bucket), neg)
        meta = lookup(meta_bucket[0 if var == 0 else 1])
        sink = jnp.broadcast_to(sink_logits.astype(F32)[None, None, :], (BLOCK, 1, A_HEADS))
        padc = jnp.full((BLOCK, HB - N_META - 1, A_HEADS), NEG, F32)
        extra = jnp.concatenate([meta, sink, padc], axis=1)
        halves = [jnp.concatenate([band[:HB, :3 * BLOCK - HB], extra[:HB]], axis=1),
                  jnp.concatenate([extra[HB:], band[HB:, HB:]], axis=1)]
        full = jnp.transpose(jnp.stack(halves), (3, 0, 1, 2))
        full = full.reshape(A_KV_HEADS, A_GROUP, 2, HB, 3 * BLOCK)
        full = jnp.transpose(full, (0, 2, 1, 3, 4)).reshape(A_KV_HEADS, 2, A_GROUP * HB, 3 * BLOCK)
        variants.append(full)
    return jnp.stack(variants) * LOG2E


def _stage_d_kernel(x1_ref, hf_ref, hb_ref, at_ref, nm_ref, wo_ref, mg_ref, wga_ref, wgb_ref,
                    wupm_ref, wupa_ref, wout_ref, nf_ref, wgu_ref, wd_ref, out_ref, a_scr, hm_scr):
    x1 = x1_ref[...]
    hn = _rms(x1, nm_ref[...]).astype(BF16)
    o = _dot(hn, wo_ref[...])
    for h in range(M_HEADS):
        sl = slice(h * M_DV, (h + 1) * M_DV)
        hs = hf_ref[:, sl] + hb_ref[:, sl]
        hm_scr[:, sl] = (_rms(hs, mg_ref[:, sl]) * _sigmoid(o[:, sl])).astype(BF16)
    mixed = _sigmoid(_dot(hn, wga_ref[...])) * _dot(hm_scr[...], wupm_ref[...])
    mixed = mixed + _sigmoid(_dot(hn, wgb_ref[...])) * _dot(at_ref[...], wupa_ref[...])
    x2 = x1 + _dot(mixed.astype(BF16), wout_ref[...])
    out_ref[...] = _ffn_residual(x2, nf_ref, wgu_ref, wd_ref, a_scr)


def _stage_d(x1, hf, hb, at, w, tile):
    n = x1.shape[0]
    row = lambda width: pl.BlockSpec((tile, width), lambda i: (i, 0))
    consts = (w['nm'], w['wo'], w['mg'], w['wga'], w['wgb'], w['wupm'], w['wupa'], w['wout'],
              w['nf2'], w['wgu2'], w['wd2'])
    return pl.pallas_call(
        _stage_d_kernel,
        grid=(n // tile,),
        in_specs=[row(D_MODEL), row(M_WIDTH), row(M_WIDTH), row(A_WIDTH)]
                 + [_const_spec(c.shape) for c in consts],
        out_specs=row(D_MODEL),
        out_shape=jax.ShapeDtypeStruct((n, D_MODEL), F32),
        scratch_shapes=[pltpu.VMEM((tile, D_FF), BF16), pltpu.VMEM((tile, M_WIDTH), BF16)],
        compiler_params=pltpu.CompilerParams(dimension_semantics=("parallel",),
                                             vmem_limit_bytes=VMEM_LIMIT_BYTES),
        name="stage_d",
    )(x1, hf, hb, at, *consts)


def _prep_weights(norm_ffn1, w_ffn1_gu, w_ffn1_down, norm_mix, w_in, b_gates, m_out_gain,
                  q_norm_gain, k_norm_gain, w_up_m, w_up_a, w_out, norm_ffn2, w_ffn2_gu,
                  w_ffn2_down):
    row = lambda v: v.reshape(1, -1).astype(F32)
    o_qm, o_om = 0, 3 * M_WIDTH
    o_gm = o_om + M_WIDTH
    o_qa = o_gm + 4 * M_HEADS
    o_ka = o_qa + A_WIDTH
    o_va = o_ka + A_KV_HEADS * A_DH
    o_ga = o_va + A_KV_HEADS * A_DH
    o_gb = o_ga + D_MODEL
    cols = lambda a, b: w_in[:, a:b]
    n_g = 4 * M_HEADS
    wkvg = jnp.pad(jnp.concatenate([cols(o_ka, o_ga), cols(o_gm, o_qa)], axis=1),
                   ((0, 0), (0, MXU_DIM - n_g)))
    gsum = np.kron(np.eye(LANES // A_DH, dtype=np.float32), np.ones((A_DH, A_DH), np.float32))
    return {
        'nf1': row(norm_ffn1), 'wgu1': w_ffn1_gu.astype(BF16), 'wd1': w_ffn1_down.astype(BF16),
        'nm': row(norm_mix),
        'wqkv': cols(o_qm, o_om).astype(BF16),
        'wo': cols(o_om, o_gm).astype(BF16),
        'bg': jnp.pad(row(b_gates), ((0, 0), (0, LANES - n_g))),
        'wqa': cols(o_qa, o_ka).astype(BF16),
        'wkvg': wkvg.astype(BF16),
        'qg': jnp.tile(row(q_norm_gain), (1, A_HEADS)),
        'kg': jnp.tile(row(k_norm_gain), (1, LANES // A_DH)),
        'gsum': jnp.asarray(np.concatenate([gsum, gsum], axis=0), BF16),
        'wga': cols(o_ga, o_gb).astype(BF16), 'wgb': cols(o_gb, o_gb + D_MODEL).astype(BF16),
        'mg': row(m_out_gain),
        'wupm': w_up_m.astype(BF16), 'wupa': w_up_a.astype(BF16), 'wout': w_out.astype(BF16),
        'nf2': row(norm_ffn2), 'wgu2': w_ffn2_gu.astype(BF16), 'wd2': w_ffn2_down.astype(BF16),
    }


def _encode(x, w, meta, bias_fn, tile, chunk):
    B, S, _ = x.shape
    meta_qkv, meta_gates, meta_kva = meta
    x1, qkv, gates, qa, kva = _stage_a(x.reshape(B * S, D_MODEL), w, tile)
    hf, hb = _mlstm(qkv.reshape(B, S, -1), gates.reshape(B, S, -1), meta_qkv, meta_gates, chunk)
    at = _attention(qa.reshape(B, S, -1), kva.reshape(B, S, -1), meta_kva, bias_fn(S // BLOCK))
    y = _stage_d(x1, hf.reshape(B * S, -1), hb.reshape(B * S, -1), at.reshape(B * S, -1), w, tile)
    return y.reshape(B, S, D_MODEL)


def _layer(x_groups, meta_tokens, rel_bias, norm_ffn1, w_ffn1_gu, w_ffn1_down, norm_mix, w_in,
           b_gates, m_out_gain, q_norm_gain, k_norm_gain, sink_logits, w_up_m, w_up_a, w_out,
           norm_ffn2, w_ffn2_gu, w_ffn2_down, tile=TOKEN_TILE, chunk=M_CHUNK):
    assert norm_ffn1.shape[0] == 1, "single layer"
    w = _prep_weights(norm_ffn1[0], w_ffn1_gu[0], w_ffn1_down[0], norm_mix[0], w_in[0], b_gates[0],
                      m_out_gain[0], q_norm_gain[0], k_norm_gain[0], w_up_m[0], w_up_a[0],
                      w_out[0], norm_ffn2[0], w_ffn2_gu[0], w_ffn2_down[0])
    _, m_qkv, m_gates, _, m_kva = _stage_a(meta_tokens.astype(F32), w, N_META)
    pad_rows = lambda a: jnp.pad(a, ((0, META_PAD - N_META), (0, 0)))
    meta = (pad_rows(m_qkv), pad_rows(m_gates), m_kva)
    bias_fn = functools.lru_cache(None)(
        lambda nb: _attention_bias(rel_bias, sink_logits[0], nb))
    return tuple(_encode(x, w, meta, bias_fn, tile, chunk) for x in x_groups)


def kernel(x_prompt, x_sample, meta_tokens, rel_bias, norm_ffn1, w_ffn1_gu, w_ffn1_down, norm_mix,
           w_in, b_gates, m_out_gain, q_norm_gain, k_norm_gain, sink_logits, w_up_m, w_up_a, w_out,
           norm_ffn2, w_ffn2_gu, w_ffn2_down):
    return _layer((x_prompt, x_sample), meta_tokens, rel_bias, norm_ffn1, w_ffn1_gu, w_ffn1_down,
                  norm_mix, w_in, b_gates, m_out_gain, q_norm_gain, k_norm_gain, sink_logits,
                  w_up_m, w_up_a, w_out, norm_ffn2, w_ffn2_gu, w_ffn2_down)
```

```python
import functools
import math

import numpy as np
import jax
import jax.numpy as jnp
from jax import lax
from jax.experimental import pallas as pl
from jax.experimental.pallas import tpu as pltpu

F32 = jnp.float32
BF16 = jnp.bfloat16

D_MODEL = 1024
D_FF = 2816
N_META = 16
M_HEADS = 4
M_DK = 128
M_DV = 128
M_WIDTH = M_HEADS * M_DV
A_HEADS = 8
A_KV_HEADS = 2
A_GROUP = A_HEADS // A_KV_HEADS
A_DH = 64
A_WIDTH = A_HEADS * A_DH
WINDOW = 128
BLOCK = 128
N_BUCKETS = 32
MAX_DIST = 128
EPS = 1e-6
NEG = -1e30
LOG2E = 1.4426950408889634

LANES = 128
MXU_DIM = 256
VMEM_LIMIT_BYTES = 58 * 1024 * 1024

FF_CHUNK = MXU_DIM
N_FF_CHUNKS = D_FF // FF_CHUNK
TOKEN_TILE = 512
M_CHUNK = 256
META_PAD = 128
PREP_CHUNKS = 8
M_CHUNKS_PER_STEP = 4
KVA_WIDTH = 2 * A_KV_HEADS * LANES

_NN = (((1,), (0,)), ((), ()))
_NT = (((1,), (1,)), ((), ()))


def _dot(a, b, dims=_NN):
    return lax.dot_general(a, b, dims, preferred_element_type=F32)


def _split_dot_lhs(x_f32, a_bf16):
    hi = x_f32.astype(BF16)
    lo = (x_f32 - hi.astype(F32)).astype(BF16)
    return _dot(hi, a_bf16) + _dot(lo, a_bf16)


def _rms(x, g):
    ms = jnp.mean(x * x, axis=-1, keepdims=True)
    return x * lax.rsqrt(ms + EPS) * g


def _sigmoid(x):
    return 1.0 / (1.0 + jnp.exp(-x))


def _log_sigmoid(x):
    return jnp.minimum(x, 0.0) - jnp.log1p(jnp.exp(-jnp.abs(x)))


def _const_spec(shape):
    nd = len(shape)
    return pl.BlockSpec(shape, lambda *_: (0,) * nd, pipeline_mode=pl.Buffered(1))


def _ffn_residual(x, nf_ref, wgu_ref, wd_ref, a_scr):
    xn = _rms(x, nf_ref[...]).astype(BF16)
    for j in range(N_FF_CHUNKS):
        sl = slice(j * FF_CHUNK, (j + 1) * FF_CHUNK)
        g = _dot(xn, wgu_ref[:, sl])
        u = _dot(xn, wgu_ref[:, D_FF + j * FF_CHUNK:D_FF + (j + 1) * FF_CHUNK])
        a_scr[:, sl] = (g * _sigmoid(g) * u).astype(BF16)
    y = _dot(a_scr[...], wd_ref[...])
    return x + 0.5 * y


def _group_sumsq(x, gsum_ref):
    sq = x * x
    hi = sq.astype(BF16)
    lo = (sq - hi.astype(F32)).astype(BF16)
    return _dot(jnp.concatenate([hi, lo], axis=1), gsum_ref[...])


def _lane_halves_tiled(x):
    swapped = pltpu.roll(x, LANES // 2, axis=1)
    low = lax.broadcasted_iota(jnp.int32, x.shape, 1) < LANES // 2
    return jnp.where(low, x, swapped), jnp.where(low, swapped, x)


def _stage_a_kernel(x_ref, nf_ref, wgu_ref, wd_ref, nm_ref, wqkv_ref, bg_ref, wqa_ref,
                    wkvg_ref, qg_ref, kg_ref, gsum_ref,
                    x1_ref, qkv_ref, gate_ref, qa_ref, kva_ref, a_scr):
    x1 = _ffn_residual(x_ref[...], nf_ref, wgu_ref, wd_ref, a_scr)
    x1_ref[...] = x1
    hn = _rms(x1, nm_ref[...]).astype(BF16)
    qkv_ref[:, :M_WIDTH] = (_dot(hn, wqkv_ref[:, :M_WIDTH]) * (M_DK ** -0.5)).astype(BF16)
    qkv_ref[:, M_WIDTH:] = _dot(hn, wqkv_ref[:, M_WIDTH:]).astype(BF16)
    kvg = _dot(hn, wkvg_ref[...])
    gate_ref[...] = (kvg[:, 2 * LANES:3 * LANES] + bg_ref[...])[:, :4 * M_HEADS]
    k = kvg[:, :LANES]
    kn = k * lax.rsqrt(_group_sumsq(k, gsum_ref) * (1.0 / A_DH) + EPS) * kg_ref[...]
    for t, tiled in enumerate(_lane_halves_tiled(kn) + _lane_halves_tiled(kvg[:, LANES:2 * LANES])):
        kva_ref[:, t * LANES:(t + 1) * LANES] = tiled.astype(BF16)
    q_all = _dot(hn, wqa_ref[...])
    for t in range(A_WIDTH // LANES):
        sl = slice(t * LANES, (t + 1) * LANES)
        q = q_all[:, sl]
        ss = _group_sumsq(q, gsum_ref)
        qn = q * lax.rsqrt(ss * (1.0 / A_DH) + EPS) * qg_ref[:, sl] * (A_DH ** -0.5 * LOG2E)
        qa_ref[:, sl] = qn.astype(BF16)


def _stage_a(x2d, w, tile):
    n = x2d.shape[0]
    assert n % tile == 0
    row = lambda width: pl.BlockSpec((tile, width), lambda i: (i, 0))
    out_shape = (
        jax.ShapeDtypeStruct((n, D_MODEL), F32),
        jax.ShapeDtypeStruct((n, 3 * M_WIDTH), BF16),
        jax.ShapeDtypeStruct((n, 4 * M_HEADS), F32),
        jax.ShapeDtypeStruct((n, A_WIDTH), BF16),
        jax.ShapeDtypeStruct((n, KVA_WIDTH), BF16),
    )
    consts = (w['nf1'], w['wgu1'], w['wd1'], w['nm'], w['wqkv'], w['bg'], w['wqa'],
              w['wkvg'], w['qg'], w['kg'], w['gsum'])
    return pl.pallas_call(
        _stage_a_kernel,
        grid=(n // tile,),
        in_specs=[row(D_MODEL)] + [_const_spec(c.shape) for c in consts],
        out_specs=(row(D_MODEL), row(3 * M_WIDTH), row(4 * M_HEADS), row(A_WIDTH), row(KVA_WIDTH)),
        out_shape=out_shape,
        scratch_shapes=[pltpu.VMEM((tile, D_FF), BF16)],
        compiler_params=pltpu.CompilerParams(dimension_semantics=("parallel",),
                                             vmem_limit_bytes=VMEM_LIMIT_BYTES),
        name="stage_a",
    )(x2d, *consts)


N_UNITS = 2 * M_HEADS


def _running_max(x, L):
    row = lax.broadcasted_iota(jnp.int32, x.shape, 0)
    lane = lax.broadcasted_iota(jnp.int32, x.shape, 1)
    fwd = row < M_HEADS
    k = 1
    while k < L:
        prev = jnp.where(lane >= k, pltpu.roll(x, k, axis=1), NEG)
        nxt = jnp.where(lane < L - k, pltpu.roll(x, L - k, axis=1), NEG)
        x = jnp.maximum(x, jnp.where(fwd, prev, nxt))
        k *= 2
    return x


def _state_update(kt_f32, w_row, vaug, s_old_row, caug):
    wkt = (kt_f32 * w_row).astype(BF16)
    s_old = jnp.concatenate([s_old_row, s_old_row], axis=1)
    return s_old * caug + _dot(wkt, vaug)


def _gate_prep_kernel(g_ref, triu_ref, rows_ref, cols_ref, sc_ref):
    H = M_HEADS
    L = triu_ref.shape[0]
    eye = (lax.broadcasted_iota(jnp.int32, (N_UNITS, LANES), 0)
           == lax.broadcasted_iota(jnp.int32, (N_UNITS, LANES), 1))
    lane_vec = lambda x: jnp.broadcast_to(
        jnp.sum(jnp.where(eye, x, 0.0), axis=0, keepdims=True), (N_UNITS, LANES))
    for j in range(g_ref.shape[2] // L):
        g = g_ref[0, :, j * L:(j + 1) * L]
        lff = _log_sigmoid(g[H:2 * H])
        lfb = _log_sigmoid(g[3 * H:4 * H])
        cs = _split_dot_lhs(jnp.concatenate([lff, lfb], axis=0), triu_ref[...])
        bf = cs[0:H]
        bb = cs[H:2 * H, L - 1:L] - cs[H:2 * H] + lfb
        b = jnp.concatenate([bf, bb], axis=0)
        r = jnp.concatenate([g[0:H], g[2 * H:3 * H]], axis=0) - b
        b_last = jnp.concatenate([jnp.broadcast_to(bf[:, L - 1:L], (H, LANES)),
                                  jnp.broadcast_to(bb[:, 0:1], (H, LANES))], axis=0)
        a = jnp.concatenate([b_last] * (L // LANES), axis=1) + r
        m_loc = jnp.broadcast_to(jnp.max(a, axis=1, keepdims=True), (N_UNITS, LANES))
        rows_ref[0, :, j * L:(j + 1) * L] = jnp.concatenate([r * LOG2E, a], axis=0)
        packed = jnp.concatenate([_running_max(r, L) * LOG2E, b * LOG2E,
                                  jnp.zeros((LANES - 2 * N_UNITS, L), F32)], axis=0)
        cols_ref[0, j * L:(j + 1) * L, :] = jnp.transpose(packed)
        sc_ref[0, j] = jnp.concatenate([b_last, m_loc, lane_vec(b_last), lane_vec(m_loc)], axis=0)


def _gate_prep(gates, chunk, chunks_per_step):
    B, S, G = gates.shape
    L = chunk
    NC = S // L
    J = math.gcd(chunks_per_step, NC)
    gates_t = jnp.transpose(gates, (0, 2, 1))
    triu = jnp.asarray(np.triu(np.ones((L, L), np.float32)), BF16)
    return pl.pallas_call(
        _gate_prep_kernel,
        grid=(B, NC // J),
        in_specs=[pl.BlockSpec((1, G, J * L), lambda b, i: (b, 0, i)),
                  pl.BlockSpec((L, L), lambda b, i: (0, 0))],
        out_specs=(pl.BlockSpec((1, 2 * N_UNITS, J * L), lambda b, i: (b, 0, i)),
                   pl.BlockSpec((1, J * L, LANES), lambda b, i: (b, i, 0)),
                   pl.BlockSpec((1, J, 4 * N_UNITS, LANES), lambda b, i: (b, i, 0, 0))),
        out_shape=(jax.ShapeDtypeStruct((B, 2 * N_UNITS, S), F32),
                   jax.ShapeDtypeStruct((B, S, LANES), F32),
                   jax.ShapeDtypeStruct((B, NC, 4 * N_UNITS, LANES), F32)),
        compiler_params=pltpu.CompilerParams(dimension_semantics=("parallel", "parallel"),
                                             vmem_limit_bytes=VMEM_LIMIT_BYTES),
        name="gate_prep",
    )(gates_t, triu)


def _mlstm_kernel(qf_ref, kf_ref, vf_ref, qb_ref, kb_ref, vb_ref, rf_ref, rb_ref, cf_ref, cb_ref,
                  sf_ref, sb_ref, mk_ref, mv_ref, mg_ref, trim_ref,
                  hf_ref, hb_ref, c_ref, m_ref, *, L):
    H = M_HEADS
    U = N_UNITS
    n_sub = qf_ref.shape[1] // L
    c = pl.program_id(1)

    @pl.when(c == 0)
    def _init():
        c_ref[...] = jnp.zeros_like(c_ref)
        g = mg_ref[...]
        valid = lax.broadcasted_iota(jnp.int32, g.shape, 1) < N_META
        lf = jnp.where(valid, _log_sigmoid(g), 0.0)
        li = jnp.where(valid, g, NEG)
        cs = _split_dot_lhs(lf, trim_ref[...])
        tot = cs[H:2 * H, META_PAD - 1:META_PAD]
        a = tot - cs[H:2 * H] + li[0:H]
        m_new = jnp.maximum(tot, jnp.max(a, axis=1, keepdims=True))
        w = jnp.exp(a - m_new)
        ones_m = jnp.ones((META_PAD, M_DV), BF16)
        for h in range(H):
            sl = slice(h * M_DK, (h + 1) * M_DK)
            vaug = jnp.concatenate([mv_ref[:, sl], ones_m], axis=1)
            kt = jnp.transpose(mk_ref[:, sl].astype(F32))
            c_ref[h] = _state_update(kt, w[h:h + 1], vaug, jnp.zeros((1, LANES), F32),
                                     jnp.zeros(c_ref.shape[1:], F32))
        m_rows = jnp.concatenate([jnp.broadcast_to(m_new, (H, LANES)),
                                  jnp.zeros((H, LANES), F32)], axis=0)
        eye = (lax.broadcasted_iota(jnp.int32, (U, LANES), 0)
               == lax.broadcasted_iota(jnp.int32, (U, LANES), 1))
        m_lanes = jnp.sum(jnp.where(eye, m_rows, 0.0), axis=0, keepdims=True)
        m_ref[...] = jnp.concatenate([m_rows, jnp.broadcast_to(m_lanes, (U, LANES))], axis=0)

    fwd_row = lax.broadcasted_iota(jnp.int32, (U, LANES), 0) < H
    fwd_lane = lax.broadcasted_iota(jnp.int32, (U, LANES), 1) % U < H
    tile_l = lambda x: jnp.concatenate([x] * (L // LANES), axis=1)
    fwd_row_l = tile_l(fwd_row)
    lane_i = lax.broadcasted_iota(jnp.int32, (L, LANES), 1)
    row_i = lax.broadcasted_iota(jnp.int32, (L, L), 0)
    col_i = lax.broadcasted_iota(jnp.int32, (L, L), 1)
    ones = jnp.ones((L, M_DV), BF16)
    m0r = m_ref[0:U]
    m0l = m_ref[U:2 * U]
    for i in range(n_sub):
        jf, jb = i, n_sub - 1 - i
        tf, tb = slice(jf * L, (jf + 1) * L), slice(jb * L, (jb + 1) * L)
        pick_r = lambda n: jnp.where(fwd_row, sf_ref[0, jf, n * U:(n + 1) * U],
                                     sb_ref[0, jb, n * U:(n + 1) * U])
        pick_l = lambda n: jnp.where(fwd_lane, sf_ref[0, jf, n * U:(n + 1) * U],
                                     sb_ref[0, jb, n * U:(n + 1) * U])
        m_new_r = jnp.maximum(pick_r(0) + m0r, pick_r(1))
        m_new_l = jnp.maximum(pick_l(2) + m0l, pick_l(3))
        s_old = jnp.exp(pick_r(0) + m0r - m_new_r)
        r2 = jnp.where(fwd_row_l, rf_ref[0, 0:U, tf], rb_ref[0, 0:U, tb])
        w = jnp.exp(jnp.where(fwd_row_l, rf_ref[0, U:2 * U, tf], rb_ref[0, U:2 * U, tb])
                    - tile_l(m_new_r))
        m2 = m0r * LOG2E
        colsx = jnp.where(lane_i % U < H, cf_ref[0, tf], cb_ref[0, tb])
        g2 = jnp.maximum(colsx, m0l[0:1] * LOG2E)
        e_cols = jnp.exp2(-(pltpu.roll(colsx, LANES - U, axis=1) + g2))
        den = jnp.zeros((L, LANES), F32)
        for u in range(U):
            sl = slice((u % H) * M_DK, (u % H + 1) * M_DK)
            if u < H:
                q, k, v, mask = qf_ref[0, tf, sl], kf_ref[0, tf, sl], vf_ref[0, tf, sl], row_i >= col_i
            else:
                q, k, v, mask = qb_ref[0, tb, sl], kb_ref[0, tb, sl], vb_ref[0, tb, sl], row_i <= col_i
            out, ts = (hf_ref, tf) if u < H else (hb_ref, tb)
            caug = c_ref[u]
            vaug = jnp.concatenate([v, ones], axis=1)
            g_col = g2[:, u:u + 1]
            x_intra = jnp.where(mask, _dot(q, k, _NT) * jnp.exp2(r2[u:u + 1] - g_col), 0.0)
            x_inter = q.astype(F32) * jnp.exp2(m2[u:u + 1] - g_col)
            x = jnp.concatenate([x_intra, x_inter], axis=1).astype(BF16)
            tot = _dot(x, jnp.concatenate([vaug, caug.astype(BF16)], axis=0))
            out[0, ts, sl] = tot[:, :M_DV]
            den = jnp.where(lane_i == u, tot[:, M_DV:], den)
            c_ref[u] = _state_update(jnp.transpose(k.astype(F32)), w[u:u + 1], vaug,
                                     s_old[u:u + 1], caug)
        inv = 1.0 / jnp.maximum(jnp.abs(den), e_cols)
        for u in range(U):
            sl = slice((u % H) * M_DV, (u % H + 1) * M_DV)
            out, ts = (hf_ref, tf) if u < H else (hb_ref, tb)
            out[0, ts, sl] = out[0, ts, sl] * inv[:, u:u + 1]
        m0r, m0l = m_new_r, m_new_l
    m_ref[...] = jnp.concatenate([m0r, m0l], axis=0)


def _mlstm(qkv, gates, meta_qkv, meta_gates, chunk):
    B, S, _ = qkv.shape
    L = chunk
    assert S % L == 0 and L % LANES == 0
    NC = S // L
    rows, cols, sc = _gate_prep(gates, L, PREP_CHUNKS)
    meta_gates_t = jnp.transpose(meta_gates)
    trim = jnp.asarray(np.triu(np.ones((META_PAD, META_PAD), np.float32)), BF16)
    G = 4 * M_HEADS
    J = math.gcd(M_CHUNKS_PER_STEP, NC)
    T = J * L
    NS = NC // J
    fwd = lambda j: pl.BlockSpec((1, T, M_WIDTH), lambda b, c: (b, c, j))
    bwd = lambda j: pl.BlockSpec((1, T, M_WIDTH), lambda b, c: (b, NS - 1 - c, j))
    in_specs = [
        fwd(0), fwd(1), fwd(2), bwd(0), bwd(1), bwd(2),
        pl.BlockSpec((1, 2 * N_UNITS, T), lambda b, c: (b, 0, c)),
        pl.BlockSpec((1, 2 * N_UNITS, T), lambda b, c: (b, 0, NS - 1 - c)),
        pl.BlockSpec((1, T, LANES), lambda b, c: (b, c, 0)),
        pl.BlockSpec((1, T, LANES), lambda b, c: (b, NS - 1 - c, 0)),
        pl.BlockSpec((1, J, 4 * N_UNITS, LANES), lambda b, c: (b, c, 0, 0)),
        pl.BlockSpec((1, J, 4 * N_UNITS, LANES), lambda b, c: (b, NS - 1 - c, 0, 0)),
        pl.BlockSpec((META_PAD, M_WIDTH), lambda b, c: (0, 1)),
        pl.BlockSpec((META_PAD, M_WIDTH), lambda b, c: (0, 2)),
        pl.BlockSpec((G, META_PAD), lambda b, c: (0, 0)),
        pl.BlockSpec((META_PAD, META_PAD), lambda b, c: (0, 0)),
    ]
    out_specs = (pl.BlockSpec((1, T, M_WIDTH), lambda b, c: (b, c, 0)),
                 pl.BlockSpec((1, T, M_WIDTH), lambda b, c: (b, NS - 1 - c, 0)))
    return pl.pallas_call(
        functools.partial(_mlstm_kernel, L=L),
        grid=(B, NS),
        in_specs=in_specs,
        out_specs=out_specs,
        out_shape=(jax.ShapeDtypeStruct((B, S, M_WIDTH), F32),
                   jax.ShapeDtypeStruct((B, S, M_WIDTH), F32)),
        scratch_shapes=[pltpu.VMEM((N_UNITS, M_DK, 2 * M_DV), F32),
                        pltpu.VMEM((2 * N_UNITS, LANES), F32)],
        compiler_params=pltpu.CompilerParams(dimension_semantics=("parallel", "arbitrary"),
                                             vmem_limit_bytes=VMEM_LIMIT_BYTES),
        name="mlstm",
    )(qkv, qkv, qkv, qkv, qkv, qkv, rows, rows, cols, cols, sc, sc,
      meta_qkv, meta_qkv, meta_gates_t, trim)


def _window_attention(q_ref, kp_ref, kc_ref, kn_ref, km_ref, bias_ref, o_ref, first_block, n_blocks):
    GW = A_GROUP * A_DH
    HB = BLOCK // 2
    n_q = q_ref.shape[0] // BLOCK
    lane = lax.broadcasted_iota(jnp.int32, (HB, GW), 1)
    head_masks = [((lane >= g * A_DH) & (lane < (g + 1) * A_DH)).astype(F32)
                  for g in range(A_GROUP)]
    wide = lambda x: jnp.concatenate([x, x], axis=1)
    zeros = jnp.zeros((HB - N_META, LANES), BF16)
    for i in range(n_q):
        blk = first_block + i
        var = jnp.where(blk == 0, 0, jnp.where(blk == n_blocks - 1, 2, 1))
        cur = kc_ref[i * BLOCK:(i + 1) * BLOCK]
        prev = kp_ref[...] if i == 0 else kc_ref[(i - 1) * BLOCK:i * BLOCK]
        nxt = kn_ref[...] if i == n_q - 1 else kc_ref[(i + 1) * BLOCK:(i + 2) * BLOCK]
        for j in range(A_KV_HEADS):
            for kind in range(2):
                sl = slice((kind * A_KV_HEADS + j) * LANES, (kind * A_KV_HEADS + j + 1) * LANES)
                extra = jnp.concatenate([km_ref[:, sl], zeros], axis=0)
                first = wide(jnp.concatenate([prev[:, sl], cur[:, sl], nxt[:HB, sl], extra], axis=0))
                second = wide(jnp.concatenate([extra, prev[HB:, sl], cur[:, sl], nxt[:, sl]], axis=0))
                if kind == 0:
                    keys = (first, second)
                else:
                    vals = (first, second)
            for h in range(2):
                rows = slice(i * BLOCK + h * HB, i * BLOCK + (h + 1) * HB)
                qg = q_ref[rows, j * GW:(j + 1) * GW].astype(F32)
                qs = jnp.concatenate([(qg * m).astype(BF16) for m in head_masks], axis=0)
                s = _dot(qs, keys[h], _NT) + bias_ref[var, j, h]
                e = jnp.exp2(s - jnp.max(s, axis=1, keepdims=True))
                den = jnp.sum(e, axis=1, keepdims=True)
                o = _dot(e.astype(BF16), vals[h]) / den
                og = o[(A_GROUP - 1) * HB:]
                for g in range(A_GROUP - 2, -1, -1):
                    og = jnp.where(lane < (g + 1) * A_DH, o[g * HB:(g + 1) * HB], og)
                o_ref[rows, j * GW:(j + 1) * GW] = og.astype(BF16)


def _t5_bucket_np(rel):
    nb = N_BUCKETS // 2
    max_exact = nb // 2
    ret = np.where(rel > 0, nb, 0)
    n = np.abs(rel)
    nf = np.maximum(n, 1).astype(np.float32)
    large = max_exact + (np.log(nf / max_exact) / math.log(MAX_DIST / max_exact)
                         * (nb - max_exact)).astype(np.int32)
    large = np.minimum(large, nb - 1)
    return ret + np.where(n < max_exact, n, large)


def _attention_bias(rel_bias, sink_logits, n_blocks):
    HB = BLOCK // 2
    t = np.arange(BLOCK)
    s_off = np.arange(3 * BLOCK) - BLOCK
    rel_band = s_off[None, :] - t[:, None]
    band_ok = np.abs(rel_band) <= WINDOW
    band_bucket = _t5_bucket_np(rel_band)
    q_pos = N_META + np.arange(n_blocks * BLOCK).reshape(n_blocks, BLOCK)
    meta_bucket = _t5_bucket_np(np.arange(N_META)[None, None, :] - q_pos[..., None])
    assert (meta_bucket[1:] == meta_bucket[1:2]).all()
    table = rel_bias.astype(F32)
    lookup = lambda bucket: jnp.einsum(
        '...k,kh->...h', jnp.asarray(np.eye(N_BUCKETS, dtype=np.float32)[bucket]), table,
        precision=lax.Precision.HIGHEST)
    neg = jnp.full((BLOCK, 3 * BLOCK, A_HEADS), NEG, F32)
    variants = []
    for var in range(3):
        ok = band_ok.copy()
        if var == 0:
            ok[:, :BLOCK] = False
        if var == 2:
            ok[:, 2 * BLOCK:] = False
        band = jnp.where(jnp.asarray(ok)[:, :, None], lookup(band_bucket), neg)
        meta = lookup(meta_bucket[0 if var == 0 else 1])
        sink = jnp.broadcast_to(sink_logits.astype(F32)[None, None, :], (BLOCK, 1, A_HEADS))
        padc = jnp.full((BLOCK, HB - N_META - 1, A_HEADS), NEG, F32)
        extra = jnp.concatenate([meta, sink, padc], axis=1)
        halves = [jnp.concatenate([band[:HB, :3 * BLOCK - HB], extra[:HB]], axis=1),
                  jnp.concatenate([extra[HB:], band[HB:, HB:]], axis=1)]
        full = jnp.transpose(jnp.stack(halves), (3, 0, 1, 2))
        full = full.reshape(A_KV_HEADS, A_GROUP, 2, HB, 3 * BLOCK)
        full = jnp.transpose(full, (0, 2, 1, 3, 4)).reshape(A_KV_HEADS, 2, A_GROUP * HB, 3 * BLOCK)
        variants.append(full)
    return jnp.stack(variants) * LOG2E


def _stage_d_kernel(x1_ref, hf_ref, hb_ref, qa_ref, kp_ref, kc_ref, kn_ref, km_ref, bias_ref,
                    nm_ref, wo_ref, mg_ref, wga_ref, wgb_ref, wupm_ref, wupa_ref, wout_ref,
                    nf_ref, wgu_ref, wd_ref, out_ref, a_scr, hm_scr, at_scr, *, tiles_per_seq):
    blocks_per_tile = x1_ref.shape[0] // BLOCK
    first_block = (pl.program_id(0) % tiles_per_seq) * blocks_per_tile
    _window_attention(qa_ref, kp_ref, kc_ref, kn_ref, km_ref, bias_ref, at_scr, first_block,
                      tiles_per_seq * blocks_per_tile)
    x1 = x1_ref[...]
    hn = _rms(x1, nm_ref[...]).astype(BF16)
    o = _dot(hn, wo_ref[...])
    for h in range(M_HEADS):
        sl = slice(h * M_DV, (h + 1) * M_DV)
        hs = hf_ref[:, sl] + hb_ref[:, sl]
        hm_scr[:, sl] = (_rms(hs, mg_ref[:, sl]) * _sigmoid(o[:, sl])).astype(BF16)
    mixed = _sigmoid(_dot(hn, wga_ref[...])) * _dot(hm_scr[...], wupm_ref[...])
    mixed = mixed + _sigmoid(_dot(hn, wgb_ref[...])) * _dot(at_scr[...], wupa_ref[...])
    x2 = x1 + _dot(mixed.astype(BF16), wout_ref[...])
    out_ref[...] = _ffn_residual(x2, nf_ref, wgu_ref, wd_ref, a_scr)


def _stage_d(x1, hf, hb, qa, kva, meta_kva, bias, w, tile, seq_len):
    n = x1.shape[0]
    assert seq_len % tile == 0 and tile % BLOCK == 0 and seq_len // BLOCK >= 2
    R = tile // BLOCK
    last = n // BLOCK - 1
    row = lambda width: pl.BlockSpec((tile, width), lambda i: (i, 0))
    consts = (w['nm'], w['wo'], w['mg'], w['wga'], w['wgb'], w['wupm'], w['wupa'], w['wout'],
              w['nf2'], w['wgu2'], w['wd2'])
    in_specs = [row(D_MODEL), row(M_WIDTH), row(M_WIDTH), row(A_WIDTH),
                pl.BlockSpec((BLOCK, KVA_WIDTH), lambda i: (jnp.maximum(i * R - 1, 0), 0)),
                row(KVA_WIDTH),
                pl.BlockSpec((BLOCK, KVA_WIDTH), lambda i: (jnp.minimum((i + 1) * R, last), 0)),
                _const_spec(meta_kva.shape), _const_spec(bias.shape)]
    return pl.pallas_call(
        functools.partial(_stage_d_kernel, tiles_per_seq=seq_len // tile),
        grid=(n // tile,),
        in_specs=in_specs + [_const_spec(c.shape) for c in consts],
        out_specs=row(D_MODEL),
        out_shape=jax.ShapeDtypeStruct((n, D_MODEL), F32),
        scratch_shapes=[pltpu.VMEM((tile, D_FF), BF16), pltpu.VMEM((tile, M_WIDTH), BF16),
                        pltpu.VMEM((tile, A_WIDTH), BF16)],
        compiler_params=pltpu.CompilerParams(dimension_semantics=("parallel",),
                                             vmem_limit_bytes=VMEM_LIMIT_BYTES),
        name="stage_d",
    )(x1, hf, hb, qa, kva, kva, kva, meta_kva, bias, *consts)


def _prep_weights(norm_ffn1, w_ffn1_gu, w_ffn1_down, norm_mix, w_in, b_gates, m_out_gain,
                  q_norm_gain, k_norm_gain, w_up_m, w_up_a, w_out, norm_ffn2, w_ffn2_gu,
                  w_ffn2_down):
    row = lambda v: v.reshape(1, -1).astype(F32)
    o_qm, o_om = 0, 3 * M_WIDTH
    o_gm = o_om + M_WIDTH
    o_qa = o_gm + 4 * M_HEADS
    o_ka = o_qa + A_WIDTH
    o_va = o_ka + A_KV_HEADS * A_DH
    o_ga = o_va + A_KV_HEADS * A_DH
    o_gb = o_ga + D_MODEL
    cols = lambda a, b: w_in[:, a:b]
    n_g = 4 * M_HEADS
    wkvg = jnp.pad(jnp.concatenate([cols(o_ka, o_ga), cols(o_gm, o_qa)], axis=1),
                   ((0, 0), (0, MXU_DIM - n_g)))
    gsum = np.kron(np.eye(LANES // A_DH, dtype=np.float32), np.ones((A_DH, A_DH), np.float32))
    return {
        'nf1': row(norm_ffn1), 'wgu1': w_ffn1_gu.astype(BF16), 'wd1': w_ffn1_down.astype(BF16),
        'nm': row(norm_mix),
        'wqkv': cols(o_qm, o_om).astype(BF16),
        'wo': cols(o_om, o_gm).astype(BF16),
        'bg': jnp.pad(row(b_gates), ((0, 0), (0, LANES - n_g))),
        'wqa': cols(o_qa, o_ka).astype(BF16),
        'wkvg': wkvg.astype(BF16),
        'qg': jnp.tile(row(q_norm_gain), (1, A_HEADS)),
        'kg': jnp.tile(row(k_norm_gain), (1, LANES // A_DH)),
        'gsum': jnp.asarray(np.concatenate([gsum, gsum], axis=0), BF16),
        'wga': cols(o_ga, o_gb).astype(BF16), 'wgb': cols(o_gb, o_gb + D_MODEL).astype(BF16),
        'mg': row(m_out_gain),
        'wupm': w_up_m.astype(BF16), 'wupa': w_up_a.astype(BF16), 'wout': w_out.astype(BF16),
        'nf2': row(norm_ffn2), 'wgu2': w_ffn2_gu.astype(BF16), 'wd2': w_ffn2_down.astype(BF16),
    }


def _encode(x, w, meta, bias_fn, tile, chunk):
    B, S, _ = x.shape
    meta_qkv, meta_gates, meta_kva = meta
    x1, qkv, gates, qa, kva = _stage_a(x.reshape(B * S, D_MODEL), w, tile)
    hf, hb = _mlstm(qkv.reshape(B, S, -1), gates.reshape(B, S, -1), meta_qkv, meta_gates, chunk)
    y = _stage_d(x1, hf.reshape(B * S, -1), hb.reshape(B * S, -1), qa, kva, meta_kva,
                 bias_fn(S // BLOCK), w, tile, S)
    return y.reshape(B, S, D_MODEL)


def _layer(x_groups, meta_tokens, rel_bias, norm_ffn1, w_ffn1_gu, w_ffn1_down, norm_mix, w_in,
           b_gates, m_out_gain, q_norm_gain, k_norm_gain, sink_logits, w_up_m, w_up_a, w_out,
           norm_ffn2, w_ffn2_gu, w_ffn2_down, tile=TOKEN_TILE, chunk=M_CHUNK):
    assert norm_ffn1.shape[0] == 1, "single layer"
    w = _prep_weights(norm_ffn1[0], w_ffn1_gu[0], w_ffn1_down[0], norm_mix[0], w_in[0], b_gates[0],
                      m_out_gain[0], q_norm_gain[0], k_norm_gain[0], w_up_m[0], w_up_a[0],
                      w_out[0], norm_ffn2[0], w_ffn2_gu[0], w_ffn2_down[0])
    _, m_qkv, m_gates, _, m_kva = _stage_a(meta_tokens.astype(F32), w, N_META)
    pad_rows = lambda a: jnp.pad(a, ((0, META_PAD - N_META), (0, 0)))
    meta = (pad_rows(m_qkv), pad_rows(m_gates), m_kva)
    bias_fn = functools.lru_cache(None)(
        lambda nb: _attention_bias(rel_bias, sink_logits[0], nb))
    return tuple(_encode(x, w, meta, bias_fn, tile, chunk) for x in x_groups)


def kernel(x_prompt, x_sample, meta_tokens, rel_bias, norm_ffn1, w_ffn1_gu, w_ffn1_down, norm_mix,
           w_in, b_gates, m_out_gain, q_norm_gain, k_norm_gain, sink_logits, w_up_m, w_up_a, w_out,
           norm_ffn2, w_ffn2_gu, w_ffn2_down):
    return _layer((x_prompt, x_sample), meta_tokens, rel_bias, norm_ffn1, w_ffn1_gu, w_ffn1_down,
                  norm_mix, w_in, b_gates, m_out_gain, q_norm_gain, k_norm_gain, sink_logits,
                  w_up_m, w_up_a, w_out, norm_ffn2, w_ffn2_gu, w_ffn2_down)
```

```python
import functools
import math

import numpy as np
import jax
import jax.numpy as jnp
from jax import lax
from jax.experimental import pallas as pl
from jax.experimental.pallas import tpu as pltpu

F32 = jnp.float32
BF16 = jnp.bfloat16

D_MODEL = 1024
D_FF = 2816
N_META = 16
M_HEADS = 4
M_DK = 128
M_DV = 128
M_WIDTH = M_HEADS * M_DV
A_HEADS = 8
A_KV_HEADS = 2
A_GROUP = A_HEADS // A_KV_HEADS
A_DH = 64
A_WIDTH = A_HEADS * A_DH
WINDOW = 128
BLOCK = 128
N_BUCKETS = 32
MAX_DIST = 128
EPS = 1e-6
NEG = -1e30
LOG2E = 1.4426950408889634

LANES = 128
MXU_DIM = 256
VMEM_LIMIT_BYTES = 58 * 1024 * 1024

FF_CHUNK = MXU_DIM
N_FF_CHUNKS = D_FF // FF_CHUNK
TOKEN_TILE = 512
M_CHUNK = 256
META_PAD = 128
PREP_CHUNKS = 16
M_CHUNKS_PER_STEP = 8
A_BLOCKS_PER_STEP = 8
KVA_WIDTH = 2 * A_KV_HEADS * LANES

_NN = (((1,), (0,)), ((), ()))
_NT = (((1,), (1,)), ((), ()))


def _dot(a, b, dims=_NN):
    return lax.dot_general(a, b, dims, preferred_element_type=F32)


def _split_dot_lhs(x_f32, a_bf16):
    hi = x_f32.astype(BF16)
    lo = (x_f32 - hi.astype(F32)).astype(BF16)
    return _dot(hi, a_bf16) + _dot(lo, a_bf16)


def _rms(x, g):
    ms = jnp.mean(x * x, axis=-1, keepdims=True)
    return x * lax.rsqrt(ms + EPS) * g


def _sigmoid(x):
    return 1.0 / (1.0 + jnp.exp(-x))


def _log_sigmoid(x):
    return jnp.minimum(x, 0.0) - jnp.log1p(jnp.exp(-jnp.abs(x)))


def _const_spec(shape):
    nd = len(shape)
    return pl.BlockSpec(shape, lambda *_: (0,) * nd, pipeline_mode=pl.Buffered(1))


def _ffn_residual(x, nf_ref, wgu_ref, wd_ref, a_scr):
    xn = _rms(x, nf_ref[...]).astype(BF16)
    for j in range(N_FF_CHUNKS):
        sl = slice(j * FF_CHUNK, (j + 1) * FF_CHUNK)
        g = _dot(xn, wgu_ref[:, sl])
        u = _dot(xn, wgu_ref[:, D_FF + j * FF_CHUNK:D_FF + (j + 1) * FF_CHUNK])
        a_scr[:, sl] = (g * _sigmoid(g) * u).astype(BF16)
    y = _dot(a_scr[...], wd_ref[...])
    return x + 0.5 * y


def _group_sumsq(x, gsum_ref):
    sq = x * x
    hi = sq.astype(BF16)
    lo = (sq - hi.astype(F32)).astype(BF16)
    return _dot(jnp.concatenate([hi, lo], axis=1), gsum_ref[...])


def _lane_halves_tiled(x):
    swapped = pltpu.roll(x, LANES // 2, axis=1)
    low = lax.broadcasted_iota(jnp.int32, x.shape, 1) < LANES // 2
    return jnp.where(low, x, swapped), jnp.where(low, swapped, x)


def _stage_a_kernel(x_ref, nf_ref, wgu_ref, wd_ref, nm_ref, wqkv_ref, bg_ref, wqa_ref,
                    wkvg_ref, qg_ref, kg_ref, gsum_ref,
                    x1_ref, qkv_ref, gate_ref, qa_ref, kva_ref, a_scr):
    x1 = _ffn_residual(x_ref[...], nf_ref, wgu_ref, wd_ref, a_scr)
    x1_ref[...] = x1
    hn = _rms(x1, nm_ref[...]).astype(BF16)
    qkv_ref[:, :M_WIDTH] = (_dot(hn, wqkv_ref[:, :M_WIDTH]) * (M_DK ** -0.5)).astype(BF16)
    qkv_ref[:, M_WIDTH:] = _dot(hn, wqkv_ref[:, M_WIDTH:]).astype(BF16)
    kvg = _dot(hn, wkvg_ref[...])
    gate_ref[...] = (kvg[:, 2 * LANES:3 * LANES] + bg_ref[...])[:, :4 * M_HEADS]
    k = kvg[:, :LANES]
    kn = k * lax.rsqrt(_group_sumsq(k, gsum_ref) * (1.0 / A_DH) + EPS) * kg_ref[...]
    for t, tiled in enumerate(_lane_halves_tiled(kn) + _lane_halves_tiled(kvg[:, LANES:2 * LANES])):
        kva_ref[:, t * LANES:(t + 1) * LANES] = tiled.astype(BF16)
    q_all = _dot(hn, wqa_ref[...])
    for t in range(A_WIDTH // LANES):
        sl = slice(t * LANES, (t + 1) * LANES)
        q = q_all[:, sl]
        ss = _group_sumsq(q, gsum_ref)
        qn = q * lax.rsqrt(ss * (1.0 / A_DH) + EPS) * qg_ref[:, sl] * (A_DH ** -0.5 * LOG2E)
        qa_ref[:, sl] = qn.astype(BF16)


def _stage_a(x2d, w, tile):
    n = x2d.shape[0]
    assert n % tile == 0
    row = lambda width: pl.BlockSpec((tile, width), lambda i: (i, 0))
    out_shape = (
        jax.ShapeDtypeStruct((n, D_MODEL), F32),
        jax.ShapeDtypeStruct((n, 3 * M_WIDTH), BF16),
        jax.ShapeDtypeStruct((n, 4 * M_HEADS), F32),
        jax.ShapeDtypeStruct((n, A_WIDTH), BF16),
        jax.ShapeDtypeStruct((n, KVA_WIDTH), BF16),
    )
    consts = (w['nf1'], w['wgu1'], w['wd1'], w['nm'], w['wqkv'], w['bg'], w['wqa'],
              w['wkvg'], w['qg'], w['kg'], w['gsum'])
    return pl.pallas_call(
        _stage_a_kernel,
        grid=(n // tile,),
        in_specs=[row(D_MODEL)] + [_const_spec(c.shape) for c in consts],
        out_specs=(row(D_MODEL), row(3 * M_WIDTH), row(4 * M_HEADS), row(A_WIDTH), row(KVA_WIDTH)),
        out_shape=out_shape,
        scratch_shapes=[pltpu.VMEM((tile, D_FF), BF16)],
        compiler_params=pltpu.CompilerParams(dimension_semantics=("parallel",),
                                             vmem_limit_bytes=VMEM_LIMIT_BYTES),
        name="stage_a",
    )(x2d, *consts)


N_UNITS = 2 * M_HEADS


def _running_max(x, L):
    row = lax.broadcasted_iota(jnp.int32, x.shape, 0)
    lane = lax.broadcasted_iota(jnp.int32, x.shape, 1)
    fwd = row < M_HEADS
    k = 1
    while k < L:
        prev = jnp.where(lane >= k, pltpu.roll(x, k, axis=1), NEG)
        nxt = jnp.where(lane < L - k, pltpu.roll(x, L - k, axis=1), NEG)
        x = jnp.maximum(x, jnp.where(fwd, prev, nxt))
        k *= 2
    return x


def _state_update(kt_f32, w_row, vaug, s_old_row, caug):
    wkt = (kt_f32 * w_row).astype(BF16)
    s_old = jnp.concatenate([s_old_row, s_old_row], axis=1)
    return s_old * caug + _dot(wkt, vaug)


def _gate_prep_kernel(g_ref, triu_ref, rows_ref, cols_ref, sc_ref):
    H = M_HEADS
    L = triu_ref.shape[0]
    eye = (lax.broadcasted_iota(jnp.int32, (N_UNITS, LANES), 0)
           == lax.broadcasted_iota(jnp.int32, (N_UNITS, LANES), 1))
    lane_vec = lambda x: jnp.broadcast_to(
        jnp.sum(jnp.where(eye, x, 0.0), axis=0, keepdims=True), (N_UNITS, LANES))
    for j in range(g_ref.shape[2] // L):
        g = g_ref[0, :, j * L:(j + 1) * L]
        lff = _log_sigmoid(g[H:2 * H])
        lfb = _log_sigmoid(g[3 * H:4 * H])
        cs = _split_dot_lhs(jnp.concatenate([lff, lfb], axis=0), triu_ref[...])
        bf = cs[0:H]
        bb = cs[H:2 * H, L - 1:L] - cs[H:2 * H] + lfb
        b = jnp.concatenate([bf, bb], axis=0)
        r = jnp.concatenate([g[0:H], g[2 * H:3 * H]], axis=0) - b
        b_last = jnp.concatenate([jnp.broadcast_to(bf[:, L - 1:L], (H, LANES)),
                                  jnp.broadcast_to(bb[:, 0:1], (H, LANES))], axis=0)
        a = jnp.concatenate([b_last] * (L // LANES), axis=1) + r
        m_loc = jnp.broadcast_to(jnp.max(a, axis=1, keepdims=True), (N_UNITS, LANES))
        rows_ref[0, :, j * L:(j + 1) * L] = jnp.concatenate([r * LOG2E, a], axis=0)
        packed = jnp.concatenate([_running_max(r, L) * LOG2E, b * LOG2E,
                                  jnp.zeros((LANES - 2 * N_UNITS, L), F32)], axis=0)
        cols_ref[0, j * L:(j + 1) * L, :] = jnp.transpose(packed)
        sc_ref[0, j] = jnp.concatenate([b_last, m_loc, lane_vec(b_last), lane_vec(m_loc)], axis=0)


def _gate_prep(gates, chunk, chunks_per_step):
    B, S, G = gates.shape
    L = chunk
    NC = S // L
    J = math.gcd(chunks_per_step, NC)
    gates_t = jnp.transpose(gates, (0, 2, 1))
    triu = jnp.asarray(np.triu(np.ones((L, L), np.float32)), BF16)
    return pl.pallas_call(
        _gate_prep_kernel,
        grid=(B, NC // J),
        in_specs=[pl.BlockSpec((1, G, J * L), lambda b, i: (b, 0, i)),
                  pl.BlockSpec((L, L), lambda b, i: (0, 0))],
        out_specs=(pl.BlockSpec((1, 2 * N_UNITS, J * L), lambda b, i: (b, 0, i)),
                   pl.BlockSpec((1, J * L, LANES), lambda b, i: (b, i, 0)),
                   pl.BlockSpec((1, J, 4 * N_UNITS, LANES), lambda b, i: (b, i, 0, 0))),
        out_shape=(jax.ShapeDtypeStruct((B, 2 * N_UNITS, S), F32),
                   jax.ShapeDtypeStruct((B, S, LANES), F32),
                   jax.ShapeDtypeStruct((B, NC, 4 * N_UNITS, LANES), F32)),
        compiler_params=pltpu.CompilerParams(dimension_semantics=("parallel", "parallel"),
                                             vmem_limit_bytes=VMEM_LIMIT_BYTES),
        name="gate_prep",
    )(gates_t, triu)


def _mlstm_kernel(qf_ref, kf_ref, vf_ref, qb_ref, kb_ref, vb_ref, rf_ref, rb_ref, cf_ref, cb_ref,
                  sf_ref, sb_ref, mk_ref, mv_ref, mg_ref, trim_ref,
                  hf_ref, hb_ref, c_ref, m_ref, *, L):
    H = M_HEADS
    U = N_UNITS
    n_sub = qf_ref.shape[1] // L
    c = pl.program_id(1)

    @pl.when(c == 0)
    def _init():
        c_ref[...] = jnp.zeros_like(c_ref)
        g = mg_ref[...]
        valid = lax.broadcasted_iota(jnp.int32, g.shape, 1) < N_META
        lf = jnp.where(valid, _log_sigmoid(g), 0.0)
        li = jnp.where(valid, g, NEG)
        cs = _split_dot_lhs(lf, trim_ref[...])
        tot = cs[H:2 * H, META_PAD - 1:META_PAD]
        a = tot - cs[H:2 * H] + li[0:H]
        m_new = jnp.maximum(tot, jnp.max(a, axis=1, keepdims=True))
        w = jnp.exp(a - m_new)
        ones_m = jnp.ones((META_PAD, M_DV), BF16)
        for h in range(H):
            sl = slice(h * M_DK, (h + 1) * M_DK)
            vaug = jnp.concatenate([mv_ref[:, sl], ones_m], axis=1)
            kt = jnp.transpose(mk_ref[:, sl].astype(F32))
            c_ref[h] = _state_update(kt, w[h:h + 1], vaug, jnp.zeros((1, LANES), F32),
                                     jnp.zeros(c_ref.shape[1:], F32))
        m_rows = jnp.concatenate([jnp.broadcast_to(m_new, (H, LANES)),
                                  jnp.zeros((H, LANES), F32)], axis=0)
        eye = (lax.broadcasted_iota(jnp.int32, (U, LANES), 0)
               == lax.broadcasted_iota(jnp.int32, (U, LANES), 1))
        m_lanes = jnp.sum(jnp.where(eye, m_rows, 0.0), axis=0, keepdims=True)
        m_ref[...] = jnp.concatenate([m_rows, jnp.broadcast_to(m_lanes, (U, LANES))], axis=0)

    fwd_row = lax.broadcasted_iota(jnp.int32, (U, LANES), 0) < H
    fwd_lane = lax.broadcasted_iota(jnp.int32, (U, LANES), 1) % U < H
    tile_l = lambda x: jnp.concatenate([x] * (L // LANES), axis=1)
    fwd_row_l = tile_l(fwd_row)
    lane_i = lax.broadcasted_iota(jnp.int32, (L, LANES), 1)
    row_i = lax.broadcasted_iota(jnp.int32, (L, L), 0)
    col_i = lax.broadcasted_iota(jnp.int32, (L, L), 1)
    ones = jnp.ones((L, M_DV), BF16)
    m0r = m_ref[0:U]
    m0l = m_ref[U:2 * U]
    for i in range(n_sub):
        jf, jb = i, n_sub - 1 - i
        tf, tb = slice(jf * L, (jf + 1) * L), slice(jb * L, (jb + 1) * L)
        pick_r = lambda n: jnp.where(fwd_row, sf_ref[0, jf, n * U:(n + 1) * U],
                                     sb_ref[0, jb, n * U:(n + 1) * U])
        pick_l = lambda n: jnp.where(fwd_lane, sf_ref[0, jf, n * U:(n + 1) * U],
                                     sb_ref[0, jb, n * U:(n + 1) * U])
        m_new_r = jnp.maximum(pick_r(0) + m0r, pick_r(1))
        m_new_l = jnp.maximum(pick_l(2) + m0l, pick_l(3))
        s_old = jnp.exp(pick_r(0) + m0r - m_new_r)
        r2 = jnp.where(fwd_row_l, rf_ref[0, 0:U, tf], rb_ref[0, 0:U, tb])
        w = jnp.exp(jnp.where(fwd_row_l, rf_ref[0, U:2 * U, tf], rb_ref[0, U:2 * U, tb])
                    - tile_l(m_new_r))
        m2 = m0r * LOG2E
        colsx = jnp.where(lane_i % U < H, cf_ref[0, tf], cb_ref[0, tb])
        g2 = jnp.maximum(colsx, m0l[0:1] * LOG2E)
        e_cols = jnp.exp2(-(pltpu.roll(colsx, LANES - U, axis=1) + g2))
        den = jnp.zeros((L, LANES), F32)
        for u in range(U):
            sl = slice((u % H) * M_DK, (u % H + 1) * M_DK)
            if u < H:
                q, k, v, mask = qf_ref[0, tf, sl], kf_ref[0, tf, sl], vf_ref[0, tf, sl], row_i >= col_i
            else:
                q, k, v, mask = qb_ref[0, tb, sl], kb_ref[0, tb, sl], vb_ref[0, tb, sl], row_i <= col_i
            out, ts = (hf_ref, tf) if u < H else (hb_ref, tb)
            caug = c_ref[u]
            vaug = jnp.concatenate([v, ones], axis=1)
            g_col = g2[:, u:u + 1]
            x_intra = jnp.where(mask, _dot(q, k, _NT) * jnp.exp2(r2[u:u + 1] - g_col), 0.0)
            x_inter = q.astype(F32) * jnp.exp2(m2[u:u + 1] - g_col)
            x = jnp.concatenate([x_intra, x_inter], axis=1).astype(BF16)
            tot = _dot(x, jnp.concatenate([vaug, caug.astype(BF16)], axis=0))
            out[0, ts, sl] = tot[:, :M_DV]
            den = jnp.where(lane_i == u, tot[:, M_DV:], den)
            c_ref[u] = _state_update(jnp.transpose(k.astype(F32)), w[u:u + 1], vaug,
                                     s_old[u:u + 1], caug)
        inv = 1.0 / jnp.maximum(jnp.abs(den), e_cols)
        for u in range(U):
            sl = slice((u % H) * M_DV, (u % H + 1) * M_DV)
            out, ts = (hf_ref, tf) if u < H else (hb_ref, tb)
            out[0, ts, sl] = out[0, ts, sl] * inv[:, u:u + 1]
        m0r, m0l = m_new_r, m_new_l
    m_ref[...] = jnp.concatenate([m0r, m0l], axis=0)


def _mlstm(qkv, gates, meta_qkv, meta_gates, chunk):
    B, S, _ = qkv.shape
    L = chunk
    assert S % L == 0 and L % LANES == 0
    NC = S // L
    rows, cols, sc = _gate_prep(gates, L, PREP_CHUNKS)
    meta_gates_t = jnp.transpose(meta_gates)
    trim = jnp.asarray(np.triu(np.ones((META_PAD, META_PAD), np.float32)), BF16)
    G = 4 * M_HEADS
    J = math.gcd(M_CHUNKS_PER_STEP, NC)
    T = J * L
    NS = NC // J
    fwd = lambda j: pl.BlockSpec((1, T, M_WIDTH), lambda b, c: (b, c, j))
    bwd = lambda j: pl.BlockSpec((1, T, M_WIDTH), lambda b, c: (b, NS - 1 - c, j))
    in_specs = [
        fwd(0), fwd(1), fwd(2), bwd(0), bwd(1), bwd(2),
        pl.BlockSpec((1, 2 * N_UNITS, T), lambda b, c: (b, 0, c)),
        pl.BlockSpec((1, 2 * N_UNITS, T), lambda b, c: (b, 0, NS - 1 - c)),
        pl.BlockSpec((1, T, LANES), lambda b, c: (b, c, 0)),
        pl.BlockSpec((1, T, LANES), lambda b, c: (b, NS - 1 - c, 0)),
        pl.BlockSpec((1, J, 4 * N_UNITS, LANES), lambda b, c: (b, c, 0, 0)),
        pl.BlockSpec((1, J, 4 * N_UNITS, LANES), lambda b, c: (b, NS - 1 - c, 0, 0)),
        pl.BlockSpec((META_PAD, M_WIDTH), lambda b, c: (0, 1)),
        pl.BlockSpec((META_PAD, M_WIDTH), lambda b, c: (0, 2)),
        pl.BlockSpec((G, META_PAD), lambda b, c: (0, 0)),
        pl.BlockSpec((META_PAD, META_PAD), lambda b, c: (0, 0)),
    ]
    out_specs = (pl.BlockSpec((1, T, M_WIDTH), lambda b, c: (b, c, 0)),
                 pl.BlockSpec((1, T, M_WIDTH), lambda b, c: (b, NS - 1 - c, 0)))
    return pl.pallas_call(
        functools.partial(_mlstm_kernel, L=L),
        grid=(B, NS),
        in_specs=in_specs,
        out_specs=out_specs,
        out_shape=(jax.ShapeDtypeStruct((B, S, M_WIDTH), F32),
                   jax.ShapeDtypeStruct((B, S, M_WIDTH), F32)),
        scratch_shapes=[pltpu.VMEM((N_UNITS, M_DK, 2 * M_DV), F32),
                        pltpu.VMEM((2 * N_UNITS, LANES), F32)],
        compiler_params=pltpu.CompilerParams(dimension_semantics=("parallel", "arbitrary"),
                                             vmem_limit_bytes=VMEM_LIMIT_BYTES),
        name="mlstm",
    )(qkv, qkv, qkv, qkv, qkv, qkv, rows, rows, cols, cols, sc, sc,
      meta_qkv, meta_qkv, meta_gates_t, trim)


def _attn_kernel(q_ref, kp_ref, kc_ref, kn_ref, km_ref, bias_ref, o_ref, *, n_blocks):
    GW = A_GROUP * A_DH
    HB = BLOCK // 2
    n_q = q_ref.shape[1] // BLOCK
    lane = lax.broadcasted_iota(jnp.int32, (HB, GW), 1)
    head_masks = [((lane >= g * A_DH) & (lane < (g + 1) * A_DH)).astype(F32)
                  for g in range(A_GROUP)]
    wide = lambda x: jnp.concatenate([x, x], axis=1)
    zeros = jnp.zeros((HB - N_META, LANES), BF16)
    for i in range(n_q):
        blk = pl.program_id(1) * n_q + i
        var = jnp.where(blk == 0, 0, jnp.where(blk == n_blocks - 1, 2, 1))
        cur = kc_ref[0, i * BLOCK:(i + 1) * BLOCK]
        prev = kp_ref[0] if i == 0 else kc_ref[0, (i - 1) * BLOCK:i * BLOCK]
        nxt = kn_ref[0] if i == n_q - 1 else kc_ref[0, (i + 1) * BLOCK:(i + 2) * BLOCK]
        for j in range(A_KV_HEADS):
            for kind in range(2):
                sl = slice((kind * A_KV_HEADS + j) * LANES, (kind * A_KV_HEADS + j + 1) * LANES)
                extra = jnp.concatenate([km_ref[:, sl], zeros], axis=0)
                first = wide(jnp.concatenate([prev[:, sl], cur[:, sl], nxt[:HB, sl], extra], axis=0))
                second = wide(jnp.concatenate([extra, prev[HB:, sl], cur[:, sl], nxt[:, sl]], axis=0))
                if kind == 0:
                    keys = (first, second)
                else:
                    vals = (first, second)
            for h in range(2):
                rows = slice(i * BLOCK + h * HB, i * BLOCK + (h + 1) * HB)
                qg = q_ref[0, rows, j * GW:(j + 1) * GW].astype(F32)
                qs = jnp.concatenate([(qg * m).astype(BF16) for m in head_masks], axis=0)
                s = _dot(qs, keys[h], _NT) + bias_ref[var, j, h]
                e = jnp.exp2(s - jnp.max(s, axis=1, keepdims=True))
                den = jnp.sum(e, axis=1, keepdims=True)
                o = _dot(e.astype(BF16), vals[h]) / den
                og = o[(A_GROUP - 1) * HB:]
                for g in range(A_GROUP - 2, -1, -1):
                    og = jnp.where(lane < (g + 1) * A_DH, o[g * HB:(g + 1) * HB], og)
                o_ref[0, rows, j * GW:(j + 1) * GW] = og.astype(BF16)


def _attention(qa, kva, meta_kva, bias):
    B, S, _ = qa.shape
    NB = S // BLOCK
    assert S % BLOCK == 0 and NB >= 2
    W = KVA_WIDTH
    Q = math.gcd(A_BLOCKS_PER_STEP, NB)
    in_specs = [
        pl.BlockSpec((1, Q * BLOCK, A_WIDTH), lambda b, n: (b, n, 0)),
        pl.BlockSpec((1, BLOCK, W), lambda b, n: (b, jnp.maximum(n * Q - 1, 0), 0)),
        pl.BlockSpec((1, Q * BLOCK, W), lambda b, n: (b, n, 0)),
        pl.BlockSpec((1, BLOCK, W), lambda b, n: (b, jnp.minimum((n + 1) * Q, NB - 1), 0)),
        pl.BlockSpec((N_META, W), lambda b, n: (0, 0)),
        _const_spec(bias.shape),
    ]
    return pl.pallas_call(
        functools.partial(_attn_kernel, n_blocks=NB),
        grid=(B, NB // Q),
        in_specs=in_specs,
        out_specs=pl.BlockSpec((1, Q * BLOCK, A_WIDTH), lambda b, n: (b, n, 0)),
        out_shape=jax.ShapeDtypeStruct((B, S, A_WIDTH), BF16),
        compiler_params=pltpu.CompilerParams(dimension_semantics=("parallel", "parallel"),
                                             vmem_limit_bytes=VMEM_LIMIT_BYTES),
        name="window_attn",
    )(qa, kva, kva, kva, meta_kva, bias)


def _t5_bucket_np(rel):
    nb = N_BUCKETS // 2
    max_exact = nb // 2
    ret = np.where(rel > 0, nb, 0)
    n = np.abs(rel)
    nf = np.maximum(n, 1).astype(np.float32)
    large = max_exact + (np.log(nf / max_exact) / math.log(MAX_DIST / max_exact)
                         * (nb - max_exact)).astype(np.int32)
    large = np.minimum(large, nb - 1)
    return ret + np.where(n < max_exact, n, large)


def _attention_bias(rel_bias, sink_logits, n_blocks):
    HB = BLOCK // 2
    t = np.arange(BLOCK)
    s_off = np.arange(3 * BLOCK) - BLOCK
    rel_band = s_off[None, :] - t[:, None]
    band_ok = np.abs(rel_band) <= WINDOW
    band_bucket = _t5_bucket_np(rel_band)
    q_pos = N_META + np.arange(n_blocks * BLOCK).reshape(n_blocks, BLOCK)
    meta_bucket = _t5_bucket_np(np.arange(N_META)[None, None, :] - q_pos[..., None])
    assert (meta_bucket[1:] == meta_bucket[1:2]).all()
    table = rel_bias.astype(F32)
    lookup = lambda bucket: jnp.einsum(
        '...k,kh->...h', jnp.asarray(np.eye(N_BUCKETS, dtype=np.float32)[bucket]), table,
        precision=lax.Precision.HIGHEST)
    neg = jnp.full((BLOCK, 3 * BLOCK, A_HEADS), NEG, F32)
    variants = []
    for var in range(3):
        ok = band_ok.copy()
        if var == 0:
            ok[:, :BLOCK] = False
        if var == 2:
            ok[:, 2 * BLOCK:] = False
        band = jnp.where(jnp.asarray(ok)[:, :, None], lookup(band_bucket), neg)
        meta = lookup(meta_bucket[0 if var == 0 else 1])
        sink = jnp.broadcast_to(sink_logits.astype(F32)[None, None, :], (BLOCK, 1, A_HEADS))
        padc = jnp.full((BLOCK, HB - N_META - 1, A_HEADS), NEG, F32)
        extra = jnp.concatenate([meta, sink, padc], axis=1)
        halves = [jnp.concatenate([band[:HB, :3 * BLOCK - HB], extra[:HB]], axis=1),
                  jnp.concatenate([extra[HB:], band[HB:, HB:]], axis=1)]
        full = jnp.transpose(jnp.stack(halves), (3, 0, 1, 2))
        full = full.reshape(A_KV_HEADS, A_GROUP, 2, HB, 3 * BLOCK)
        full = jnp.transpose(full, (0, 2, 1, 3, 4)).reshape(A_KV_HEADS, 2, A_GROUP * HB, 3 * BLOCK)
        variants.append(full)
    return jnp.stack(variants) * LOG2E


def _stage_d_kernel(x1_ref, hf_ref, hb_ref, at_ref, nm_ref, wo_ref, mg_ref, wga_ref, wgb_ref,
                    wupm_ref, wupa_ref, wout_ref, nf_ref, wgu_ref, wd_ref, out_ref, a_scr, hm_scr):
    x1 = x1_ref[...]
    hn = _rms(x1, nm_ref[...]).astype(BF16)
    o = _dot(hn, wo_ref[...])
    for h in range(M_HEADS):
        sl = slice(h * M_DV, (h + 1) * M_DV)
        hs = hf_ref[:, sl] + hb_ref[:, sl]
        hm_scr[:, sl] = (_rms(hs, mg_ref[:, sl]) * _sigmoid(o[:, sl])).astype(BF16)
    mixed = _sigmoid(_dot(hn, wga_ref[...])) * _dot(hm_scr[...], wupm_ref[...])
    mixed = mixed + _sigmoid(_dot(hn, wgb_ref[...])) * _dot(at_ref[...], wupa_ref[...])
    x2 = x1 + _dot(mixed.astype(BF16), wout_ref[...])
    out_ref[...] = _ffn_residual(x2, nf_ref, wgu_ref, wd_ref, a_scr)


def _stage_d(x1, hf, hb, at, w, tile):
    n = x1.shape[0]
    row = lambda width: pl.BlockSpec((tile, width), lambda i: (i, 0))
    consts = (w['nm'], w['wo'], w['mg'], w['wga'], w['wgb'], w['wupm'], w['wupa'], w['wout'],
              w['nf2'], w['wgu2'], w['wd2'])
    return pl.pallas_call(
        _stage_d_kernel,
        grid=(n // tile,),
        in_specs=[row(D_MODEL), row(M_WIDTH), row(M_WIDTH), row(A_WIDTH)]
                 + [_const_spec(c.shape) for c in consts],
        out_specs=row(D_MODEL),
        out_shape=jax.ShapeDtypeStruct((n, D_MODEL), F32),
        scratch_shapes=[pltpu.VMEM((tile, D_FF), BF16), pltpu.VMEM((tile, M_WIDTH), BF16)],
        compiler_params=pltpu.CompilerParams(dimension_semantics=("parallel",),
                                             vmem_limit_bytes=VMEM_LIMIT_BYTES),
        name="stage_d",
    )(x1, hf, hb, at, *consts)


def _prep_weights(norm_ffn1, w_ffn1_gu, w_ffn1_down, norm_mix, w_in, b_gates, m_out_gain,
                  q_norm_gain, k_norm_gain, w_up_m, w_up_a, w_out, norm_ffn2, w_ffn2_gu,
                  w_ffn2_down):
    row = lambda v: v.reshape(1, -1).astype(F32)
    o_qm, o_om = 0, 3 * M_WIDTH
    o_gm = o_om + M_WIDTH
    o_qa = o_gm + 4 * M_HEADS
    o_ka = o_qa + A_WIDTH
    o_va = o_ka + A_KV_HEADS * A_DH
    o_ga = o_va + A_KV_HEADS * A_DH
    o_gb = o_ga + D_MODEL
    cols = lambda a, b: w_in[:, a:b]
    n_g = 4 * M_HEADS
    wkvg = jnp.pad(jnp.concatenate([cols(o_ka, o_ga), cols(o_gm, o_qa)], axis=1),
                   ((0, 0), (0, MXU_DIM - n_g)))
    gsum = np.kron(np.eye(LANES // A_DH, dtype=np.float32), np.ones((A_DH, A_DH), np.float32))
    return {
        'nf1': row(norm_ffn1), 'wgu1': w_ffn1_gu.astype(BF16), 'wd1': w_ffn1_down.astype(BF16),
        'nm': row(norm_mix),
        'wqkv': cols(o_qm, o_om).astype(BF16),
        'wo': cols(o_om, o_gm).astype(BF16),
        'bg': jnp.pad(row(b_gates), ((0, 0), (0, LANES - n_g))),
        'wqa': cols(o_qa, o_ka).astype(BF16),
        'wkvg': wkvg.astype(BF16),
        'qg': jnp.tile(row(q_norm_gain), (1, A_HEADS)),
        'kg': jnp.tile(row(k_norm_gain), (1, LANES // A_DH)),
        'gsum': jnp.asarray(np.concatenate([gsum, gsum], axis=0), BF16),
        'wga': cols(o_ga, o_gb).astype(BF16), 'wgb': cols(o_gb, o_gb + D_MODEL).astype(BF16),
        'mg': row(m_out_gain),
        'wupm': w_up_m.astype(BF16), 'wupa': w_up_a.astype(BF16), 'wout': w_out.astype(BF16),
        'nf2': row(norm_ffn2), 'wgu2': w_ffn2_gu.astype(BF16), 'wd2': w_ffn2_down.astype(BF16),
    }


def _encode(x, w, meta, bias_fn, tile, chunk):
    B, S, _ = x.shape
    meta_qkv, meta_gates, meta_kva = meta
    x1, qkv, gates, qa, kva = _stage_a(x.reshape(B * S, D_MODEL), w, tile)
    hf, hb = _mlstm(qkv.reshape(B, S, -1), gates.reshape(B, S, -1), meta_qkv, meta_gates, chunk)
    at = _attention(qa.reshape(B, S, -1), kva.reshape(B, S, -1), meta_kva, bias_fn(S // BLOCK))
    y = _stage_d(x1, hf.reshape(B * S, -1), hb.reshape(B * S, -1), at.reshape(B * S, -1), w, tile)
    return y.reshape(B, S, D_MODEL)


def _layer(x_groups, meta_tokens, rel_bias, norm_ffn1, w_ffn1_gu, w_ffn1_down, norm_mix, w_in,
           b_gates, m_out_gain, q_norm_gain, k_norm_gain, sink_logits, w_up_m, w_up_a, w_out,
           norm_ffn2, w_ffn2_gu, w_ffn2_down, tile=TOKEN_TILE, chunk=M_CHUNK):
    assert norm_ffn1.shape[0] == 1, "single layer"
    w = _prep_weights(norm_ffn1[0], w_ffn1_gu[0], w_ffn1_down[0], norm_mix[0], w_in[0], b_gates[0],
                      m_out_gain[0], q_norm_gain[0], k_norm_gain[0], w_up_m[0], w_up_a[0],
                      w_out[0], norm_ffn2[0], w_ffn2_gu[0], w_ffn2_down[0])
    _, m_qkv, m_gates, _, m_kva = _stage_a(meta_tokens.astype(F32), w, N_META)
    pad_rows = lambda a: jnp.pad(a, ((0, META_PAD - N_META), (0, 0)))
    meta = (pad_rows(m_qkv), pad_rows(m_gates), m_kva)
    bias_fn = functools.lru_cache(None)(
        lambda nb: _attention_bias(rel_bias, sink_logits[0], nb))
    return tuple(_encode(x, w, meta, bias_fn, tile, chunk) for x in x_groups)


def kernel(x_prompt, x_sample, meta_tokens, rel_bias, norm_ffn1, w_ffn1_gu, w_ffn1_down, norm_mix,
           w_in, b_gates, m_out_gain, q_norm_gain, k_norm_gain, sink_logits, w_up_m, w_up_a, w_out,
           norm_ffn2, w_ffn2_gu, w_ffn2_down):
    return _layer((x_prompt, x_sample), meta_tokens, rel_bias, norm_ffn1, w_ffn1_gu, w_ffn1_down,
                  norm_mix, w_in, b_gates, m_out_gain, q_norm_gain, k_norm_gain, sink_logits,
                  w_up_m, w_up_a, w_out, norm_ffn2, w_ffn2_gu, w_ffn2_down)
```

```python
import functools
import math

import numpy as np
import jax
import jax.numpy as jnp
from jax import lax
from jax.experimental import pallas as pl
from jax.experimental.pallas import tpu as pltpu

F32 = jnp.float32
BF16 = jnp.bfloat16

D_MODEL = 1024
D_FF = 2816
N_META = 16
M_HEADS = 4
M_DK = 128
M_DV = 128
M_WIDTH = M_HEADS * M_DV
A_HEADS = 8
A_KV_HEADS = 2
A_GROUP = A_HEADS // A_KV_HEADS
A_DH = 64
A_WIDTH = A_HEADS * A_DH
WINDOW = 128
BLOCK = 128
N_BUCKETS = 32
MAX_DIST = 128
EPS = 1e-6
NEG = -1e30
LOG2E = 1.4426950408889634

LANES = 128
MXU_DIM = 256
VMEM_LIMIT_BYTES = 58 * 1024 * 1024

FF_CHUNK = MXU_DIM
N_FF_CHUNKS = D_FF // FF_CHUNK
TOKEN_TILE = 512
M_CHUNK = 256
META_PAD = 128
PREP_CHUNKS = 8
M_CHUNKS_PER_STEP = 8
A_BLOCKS_PER_STEP = 8
KVA_WIDTH = 2 * A_KV_HEADS * LANES

_NN = (((1,), (0,)), ((), ()))
_NT = (((1,), (1,)), ((), ()))


def _dot(a, b, dims=_NN):
    return lax.dot_general(a, b, dims, preferred_element_type=F32)


def _split_dot_lhs(x_f32, a_bf16):
    hi = x_f32.astype(BF16)
    lo = (x_f32 - hi.astype(F32)).astype(BF16)
    return _dot(hi, a_bf16) + _dot(lo, a_bf16)


def _rms(x, g):
    ms = jnp.mean(x * x, axis=-1, keepdims=True)
    return x * lax.rsqrt(ms + EPS) * g


def _sigmoid(x):
    return 1.0 / (1.0 + jnp.exp(-x))


def _log_sigmoid(x):
    return jnp.minimum(x, 0.0) - jnp.log1p(jnp.exp(-jnp.abs(x)))


def _const_spec(shape):
    nd = len(shape)
    return pl.BlockSpec(shape, lambda *_: (0,) * nd, pipeline_mode=pl.Buffered(1))


def _ffn_residual(x, nf_ref, wgu_ref, wd_ref, a_scr):
    xn = _rms(x, nf_ref[...]).astype(BF16)
    for j in range(N_FF_CHUNKS):
        sl = slice(j * FF_CHUNK, (j + 1) * FF_CHUNK)
        g = _dot(xn, wgu_ref[:, sl])
        u = _dot(xn, wgu_ref[:, D_FF + j * FF_CHUNK:D_FF + (j + 1) * FF_CHUNK])
        a_scr[:, sl] = (g * _sigmoid(g) * u).astype(BF16)
    y = _dot(a_scr[...], wd_ref[...])
    return x + 0.5 * y


def _group_sumsq(x, gsum_ref):
    sq = x * x
    hi = sq.astype(BF16)
    lo = (sq - hi.astype(F32)).astype(BF16)
    return _dot(jnp.concatenate([hi, lo], axis=1), gsum_ref[...])


def _lane_halves_tiled(x):
    swapped = pltpu.roll(x, LANES // 2, axis=1)
    low = lax.broadcasted_iota(jnp.int32, x.shape, 1) < LANES // 2
    return jnp.where(low, x, swapped), jnp.where(low, swapped, x)


def _stage_a_kernel(x_ref, nf_ref, wgu_ref, wd_ref, nm_ref, wqkv_ref, bg_ref, wqa_ref,
                    wkvg_ref, qg_ref, kg_ref, gsum_ref,
                    x1_ref, qkv_ref, gate_ref, qa_ref, kva_ref, a_scr):
    x1 = _ffn_residual(x_ref[...], nf_ref, wgu_ref, wd_ref, a_scr)
    x1_ref[...] = x1
    hn = _rms(x1, nm_ref[...]).astype(BF16)
    qkv_ref[:, :M_WIDTH] = (_dot(hn, wqkv_ref[:, :M_WIDTH]) * (M_DK ** -0.5)).astype(BF16)
    qkv_ref[:, M_WIDTH:] = _dot(hn, wqkv_ref[:, M_WIDTH:]).astype(BF16)
    kvg = _dot(hn, wkvg_ref[...])
    gate_ref[...] = jnp.transpose(kvg[:, 2 * LANES:3 * LANES] + bg_ref[...])[:4 * M_HEADS]
    k = kvg[:, :LANES]
    kn = k * lax.rsqrt(_group_sumsq(k, gsum_ref) * (1.0 / A_DH) + EPS) * kg_ref[...]
    for t, tiled in enumerate(_lane_halves_tiled(kn) + _lane_halves_tiled(kvg[:, LANES:2 * LANES])):
        kva_ref[:, t * LANES:(t + 1) * LANES] = tiled.astype(BF16)
    q_all = _dot(hn, wqa_ref[...])
    for t in range(A_WIDTH // LANES):
        sl = slice(t * LANES, (t + 1) * LANES)
        q = q_all[:, sl]
        ss = _group_sumsq(q, gsum_ref)
        qn = q * lax.rsqrt(ss * (1.0 / A_DH) + EPS) * qg_ref[:, sl] * (A_DH ** -0.5 * LOG2E)
        qa_ref[:, sl] = qn.astype(BF16)


def _stage_a(x2d, w, tile):
    n = x2d.shape[0]
    assert n % tile == 0
    row = lambda width: pl.BlockSpec((tile, width), lambda i: (i, 0))
    out_shape = (
        jax.ShapeDtypeStruct((n, D_MODEL), F32),
        jax.ShapeDtypeStruct((n, 3 * M_WIDTH), BF16),
        jax.ShapeDtypeStruct((4 * M_HEADS, n), F32),
        jax.ShapeDtypeStruct((n, A_WIDTH), BF16),
        jax.ShapeDtypeStruct((n, KVA_WIDTH), BF16),
    )
    consts = (w['nf1'], w['wgu1'], w['wd1'], w['nm'], w['wqkv'], w['bg'], w['wqa'],
              w['wkvg'], w['qg'], w['kg'], w['gsum'])
    return pl.pallas_call(
        _stage_a_kernel,
        grid=(n // tile,),
        in_specs=[row(D_MODEL)] + [_const_spec(c.shape) for c in consts],
        out_specs=(row(D_MODEL), row(3 * M_WIDTH), pl.BlockSpec((4 * M_HEADS, tile), lambda i: (0, i)),
                   row(A_WIDTH), row(KVA_WIDTH)),
        out_shape=out_shape,
        scratch_shapes=[pltpu.VMEM((tile, D_FF), BF16)],
        compiler_params=pltpu.CompilerParams(dimension_semantics=("parallel",),
                                             vmem_limit_bytes=VMEM_LIMIT_BYTES),
        name="stage_a",
    )(x2d, *consts)


N_UNITS = 2 * M_HEADS


def _running_max(x, L):
    row = lax.broadcasted_iota(jnp.int32, x.shape, 0)
    lane = lax.broadcasted_iota(jnp.int32, x.shape, 1)
    fwd = row < M_HEADS
    k = 1
    while k < L:
        prev = jnp.where(lane >= k, pltpu.roll(x, k, axis=1), NEG)
        nxt = jnp.where(lane < L - k, pltpu.roll(x, L - k, axis=1), NEG)
        x = jnp.maximum(x, jnp.where(fwd, prev, nxt))
        k *= 2
    return x


def _state_update(kt_f32, w_row, vaug, s_old_row, caug):
    wkt = (kt_f32 * w_row).astype(BF16)
    s_old = jnp.concatenate([s_old_row, s_old_row], axis=1)
    return s_old * caug + _dot(wkt, vaug)


def _gate_prep_kernel(g_ref, triu_ref, rows_ref, cols_ref, sc_ref):
    H = M_HEADS
    L = triu_ref.shape[0]
    eye = (lax.broadcasted_iota(jnp.int32, (N_UNITS, LANES), 0)
           == lax.broadcasted_iota(jnp.int32, (N_UNITS, LANES), 1))
    lane_vec = lambda x: jnp.broadcast_to(
        jnp.sum(jnp.where(eye, x, 0.0), axis=0, keepdims=True), (N_UNITS, LANES))
    for j in range(g_ref.shape[1] // L):
        g = g_ref[:, j * L:(j + 1) * L]
        lff = _log_sigmoid(g[H:2 * H])
        lfb = _log_sigmoid(g[3 * H:4 * H])
        cs = _split_dot_lhs(jnp.concatenate([lff, lfb], axis=0), triu_ref[...])
        bf = cs[0:H]
        bb = cs[H:2 * H, L - 1:L] - cs[H:2 * H] + lfb
        b = jnp.concatenate([bf, bb], axis=0)
        r = jnp.concatenate([g[0:H], g[2 * H:3 * H]], axis=0) - b
        b_last = jnp.concatenate([jnp.broadcast_to(bf[:, L - 1:L], (H, LANES)),
                                  jnp.broadcast_to(bb[:, 0:1], (H, LANES))], axis=0)
        a = jnp.concatenate([b_last] * (L // LANES), axis=1) + r
        m_loc = jnp.broadcast_to(jnp.max(a, axis=1, keepdims=True), (N_UNITS, LANES))
        rows_ref[0, :, j * L:(j + 1) * L] = jnp.concatenate([r * LOG2E, a], axis=0)
        packed = jnp.concatenate([_running_max(r, L) * LOG2E, b * LOG2E,
                                  jnp.zeros((LANES - 2 * N_UNITS, L), F32)], axis=0)
        cols_ref[0, j * L:(j + 1) * L, :] = jnp.transpose(packed)
        sc_ref[0, j] = jnp.concatenate([b_last, m_loc, lane_vec(b_last), lane_vec(m_loc)], axis=0)


def _gate_prep(gates_t, B, S, chunk, chunks_per_step):
    G = gates_t.shape[0]
    L = chunk
    NC = S // L
    J = math.gcd(chunks_per_step, NC)
    triu = jnp.asarray(np.triu(np.ones((L, L), np.float32)), BF16)
    return pl.pallas_call(
        _gate_prep_kernel,
        grid=(B, NC // J),
        in_specs=[pl.BlockSpec((G, J * L), lambda b, i: (0, b * (NC // J) + i)),
                  pl.BlockSpec((L, L), lambda b, i: (0, 0))],
        out_specs=(pl.BlockSpec((1, 2 * N_UNITS, J * L), lambda b, i: (b, 0, i)),
                   pl.BlockSpec((1, J * L, LANES), lambda b, i: (b, i, 0)),
                   pl.BlockSpec((1, J, 4 * N_UNITS, LANES), lambda b, i: (b, i, 0, 0))),
        out_shape=(jax.ShapeDtypeStruct((B, 2 * N_UNITS, S), F32),
                   jax.ShapeDtypeStruct((B, S, LANES), F32),
                   jax.ShapeDtypeStruct((B, NC, 4 * N_UNITS, LANES), F32)),
        compiler_params=pltpu.CompilerParams(dimension_semantics=("parallel", "parallel"),
                                             vmem_limit_bytes=VMEM_LIMIT_BYTES),
        name="gate_prep",
    )(gates_t, triu)


def _mlstm_kernel(qf_ref, kf_ref, vf_ref, qb_ref, kb_ref, vb_ref, rf_ref, rb_ref, cf_ref, cb_ref,
                  sf_ref, sb_ref, mk_ref, mv_ref, mg_ref, trim_ref,
                  hf_ref, hb_ref, c_ref, m_ref, *, L):
    H = M_HEADS
    U = N_UNITS
    n_sub = qf_ref.shape[1] // L
    c = pl.program_id(1)

    @pl.when(c == 0)
    def _init():
        c_ref[...] = jnp.zeros_like(c_ref)
        g = mg_ref[...]
        valid = lax.broadcasted_iota(jnp.int32, g.shape, 1) < N_META
        lf = jnp.where(valid, _log_sigmoid(g), 0.0)
        li = jnp.where(valid, g, NEG)
        cs = _split_dot_lhs(lf, trim_ref[...])
        tot = cs[H:2 * H, META_PAD - 1:META_PAD]
        a = tot - cs[H:2 * H] + li[0:H]
        m_new = jnp.maximum(tot, jnp.max(a, axis=1, keepdims=True))
        w = jnp.exp(a - m_new)
        ones_m = jnp.ones((META_PAD, M_DV), BF16)
        for h in range(H):
            sl = slice(h * M_DK, (h + 1) * M_DK)
            vaug = jnp.concatenate([mv_ref[:, sl], ones_m], axis=1)
            kt = jnp.transpose(mk_ref[:, sl].astype(F32))
            c_ref[h] = _state_update(kt, w[h:h + 1], vaug, jnp.zeros((1, LANES), F32),
                                     jnp.zeros(c_ref.shape[1:], F32))
        m_rows = jnp.concatenate([jnp.broadcast_to(m_new, (H, LANES)),
                                  jnp.zeros((H, LANES), F32)], axis=0)
        eye = (lax.broadcasted_iota(jnp.int32, (U, LANES), 0)
               == lax.broadcasted_iota(jnp.int32, (U, LANES), 1))
        m_lanes = jnp.sum(jnp.where(eye, m_rows, 0.0), axis=0, keepdims=True)
        m_ref[...] = jnp.concatenate([m_rows, jnp.broadcast_to(m_lanes, (U, LANES))], axis=0)

    fwd_row = lax.broadcasted_iota(jnp.int32, (U, LANES), 0) < H
    fwd_lane = lax.broadcasted_iota(jnp.int32, (U, LANES), 1) % U < H
    tile_l = lambda x: jnp.concatenate([x] * (L // LANES), axis=1)
    fwd_row_l = tile_l(fwd_row)
    lane_i = lax.broadcasted_iota(jnp.int32, (L, LANES), 1)
    row_i = lax.broadcasted_iota(jnp.int32, (L, L), 0)
    col_i = lax.broadcasted_iota(jnp.int32, (L, L), 1)
    ones = jnp.ones((L, M_DV), BF16)
    m0r = m_ref[0:U]
    m0l = m_ref[U:2 * U]
    for i in range(n_sub):
        jf, jb = i, n_sub - 1 - i
        tf, tb = slice(jf * L, (jf + 1) * L), slice(jb * L, (jb + 1) * L)
        pick_r = lambda n: jnp.where(fwd_row, sf_ref[0, jf, n * U:(n + 1) * U],
                                     sb_ref[0, jb, n * U:(n + 1) * U])
        pick_l = lambda n: jnp.where(fwd_lane, sf_ref[0, jf, n * U:(n + 1) * U],
                                     sb_ref[0, jb, n * U:(n + 1) * U])
        m_new_r = jnp.maximum(pick_r(0) + m0r, pick_r(1))
        m_new_l = jnp.maximum(pick_l(2) + m0l, pick_l(3))
        s_old = jnp.exp(pick_r(0) + m0r - m_new_r)
        r2 = jnp.where(fwd_row_l, rf_ref[0, 0:U, tf], rb_ref[0, 0:U, tb])
        w = jnp.exp(jnp.where(fwd_row_l, rf_ref[0, U:2 * U, tf], rb_ref[0, U:2 * U, tb])
                    - tile_l(m_new_r))
        m2 = m0r * LOG2E
        colsx = jnp.where(lane_i % U < H, cf_ref[0, tf], cb_ref[0, tb])
        g2 = jnp.maximum(colsx, m0l[0:1] * LOG2E)
        e_cols = jnp.exp2(-(pltpu.roll(colsx, LANES - U, axis=1) + g2))
        den = jnp.zeros((L, LANES), F32)
        for u in range(U):
            sl = slice((u % H) * M_DK, (u % H + 1) * M_DK)
            if u < H:
                q, k, v, mask = qf_ref[0, tf, sl], kf_ref[0, tf, sl], vf_ref[0, tf, sl], row_i >= col_i
            else:
                q, k, v, mask = qb_ref[0, tb, sl], kb_ref[0, tb, sl], vb_ref[0, tb, sl], row_i <= col_i
            out, ts = (hf_ref, tf) if u < H else (hb_ref, tb)
            caug = c_ref[u]
            vaug = jnp.concatenate([v, ones], axis=1)
            g_col = g2[:, u:u + 1]
            x_intra = jnp.where(mask, _dot(q, k, _NT) * jnp.exp2(r2[u:u + 1] - g_col), 0.0)
            x_inter = q.astype(F32) * jnp.exp2(m2[u:u + 1] - g_col)
            x = jnp.concatenate([x_intra, x_inter], axis=1).astype(BF16)
            tot = _dot(x, jnp.concatenate([vaug, caug.astype(BF16)], axis=0))
            out[0, ts, sl] = tot[:, :M_DV]
            den = jnp.where(lane_i == u, tot[:, M_DV:], den)
            c_ref[u] = _state_update(jnp.transpose(k.astype(F32)), w[u:u + 1], vaug,
                                     s_old[u:u + 1], caug)
        inv = 1.0 / jnp.maximum(jnp.abs(den), e_cols)
        for u in range(U):
            sl = slice((u % H) * M_DV, (u % H + 1) * M_DV)
            out, ts = (hf_ref, tf) if u < H else (hb_ref, tb)
            out[0, ts, sl] = out[0, ts, sl] * inv[:, u:u + 1]
        m0r, m0l = m_new_r, m_new_l
    m_ref[...] = jnp.concatenate([m0r, m0l], axis=0)


def _mlstm(qkv, gates_t, meta_qkv, meta_gates_t, chunk):
    B, S, _ = qkv.shape
    L = chunk
    assert S % L == 0 and L % LANES == 0
    NC = S // L
    rows, cols, sc = _gate_prep(gates_t, B, S, L, PREP_CHUNKS)
    trim = jnp.asarray(np.triu(np.ones((META_PAD, META_PAD), np.float32)), BF16)
    G = 4 * M_HEADS
    J = math.gcd(M_CHUNKS_PER_STEP, NC)
    T = J * L
    NS = NC // J
    fwd = lambda j: pl.BlockSpec((1, T, M_WIDTH), lambda b, c: (b, c, j))
    bwd = lambda j: pl.BlockSpec((1, T, M_WIDTH), lambda b, c: (b, NS - 1 - c, j))
    in_specs = [
        fwd(0), fwd(1), fwd(2), bwd(0), bwd(1), bwd(2),
        pl.BlockSpec((1, 2 * N_UNITS, T), lambda b, c: (b, 0, c)),
        pl.BlockSpec((1, 2 * N_UNITS, T), lambda b, c: (b, 0, NS - 1 - c)),
        pl.BlockSpec((1, T, LANES), lambda b, c: (b, c, 0)),
        pl.BlockSpec((1, T, LANES), lambda b, c: (b, NS - 1 - c, 0)),
        pl.BlockSpec((1, J, 4 * N_UNITS, LANES), lambda b, c: (b, c, 0, 0)),
        pl.BlockSpec((1, J, 4 * N_UNITS, LANES), lambda b, c: (b, NS - 1 - c, 0, 0)),
        pl.BlockSpec((META_PAD, M_WIDTH), lambda b, c: (0, 1)),
        pl.BlockSpec((META_PAD, M_WIDTH), lambda b, c: (0, 2)),
        pl.BlockSpec((G, META_PAD), lambda b, c: (0, 0)),
        pl.BlockSpec((META_PAD, META_PAD), lambda b, c: (0, 0)),
    ]
    out_specs = (pl.BlockSpec((1, T, M_WIDTH), lambda b, c: (b, c, 0)),
                 pl.BlockSpec((1, T, M_WIDTH), lambda b, c: (b, NS - 1 - c, 0)))
    return pl.pallas_call(
        functools.partial(_mlstm_kernel, L=L),
        grid=(B, NS),
        in_specs=in_specs,
        out_specs=out_specs,
        out_shape=(jax.ShapeDtypeStruct((B, S, M_WIDTH), F32),
                   jax.ShapeDtypeStruct((B, S, M_WIDTH), F32)),
        scratch_shapes=[pltpu.VMEM((N_UNITS, M_DK, 2 * M_DV), F32),
                        pltpu.VMEM((2 * N_UNITS, LANES), F32)],
        compiler_params=pltpu.CompilerParams(dimension_semantics=("parallel", "arbitrary"),
                                             vmem_limit_bytes=VMEM_LIMIT_BYTES),
        name="mlstm",
    )(qkv, qkv, qkv, qkv, qkv, qkv, rows, rows, cols, cols, sc, sc,
      meta_qkv, meta_qkv, meta_gates_t, trim)


def _attn_kernel(q_ref, kp_ref, kc_ref, kn_ref, km_ref, bias_ref, o_ref, *, n_blocks):
    GW = A_GROUP * A_DH
    HB = BLOCK // 2
    n_q = q_ref.shape[1] // BLOCK
    lane = lax.broadcasted_iota(jnp.int32, (HB, GW), 1)
    head_masks = [((lane >= g * A_DH) & (lane < (g + 1) * A_DH)).astype(F32)
                  for g in range(A_GROUP)]
    wide = lambda x: jnp.concatenate([x, x], axis=1)
    zeros = jnp.zeros((HB - N_META, LANES), BF16)
    for i in range(n_q):
        blk = pl.program_id(1) * n_q + i
        var = jnp.where(blk == 0, 0, jnp.where(blk == n_blocks - 1, 2, 1))
        cur = kc_ref[0, i * BLOCK:(i + 1) * BLOCK]
        prev = kp_ref[0] if i == 0 else kc_ref[0, (i - 1) * BLOCK:i * BLOCK]
        nxt = kn_ref[0] if i == n_q - 1 else kc_ref[0, (i + 1) * BLOCK:(i + 2) * BLOCK]
        for j in range(A_KV_HEADS):
            for kind in range(2):
                sl = slice((kind * A_KV_HEADS + j) * LANES, (kind * A_KV_HEADS + j + 1) * LANES)
                extra = jnp.concatenate([km_ref[:, sl], zeros], axis=0)
                first = wide(jnp.concatenate([prev[:, sl], cur[:, sl], nxt[:HB, sl], extra], axis=0))
                second = wide(jnp.concatenate([extra, prev[HB:, sl], cur[:, sl], nxt[:, sl]], axis=0))
                if kind == 0:
                    keys = (first, second)
                else:
                    vals = (first, second)
            for h in range(2):
                rows = slice(i * BLOCK + h * HB, i * BLOCK + (h + 1) * HB)
                qg = q_ref[0, rows, j * GW:(j + 1) * GW].astype(F32)
                qs = jnp.concatenate([(qg * m).astype(BF16) for m in head_masks], axis=0)
                s = _dot(qs, keys[h], _NT) + bias_ref[var, j, h]
                e = jnp.exp2(s - jnp.max(s, axis=1, keepdims=True))
                den = jnp.sum(e, axis=1, keepdims=True)
                o = _dot(e.astype(BF16), vals[h]) / den
                og = o[(A_GROUP - 1) * HB:]
                for g in range(A_GROUP - 2, -1, -1):
                    og = jnp.where(lane < (g + 1) * A_DH, o[g * HB:(g + 1) * HB], og)
                o_ref[0, rows, j * GW:(j + 1) * GW] = og.astype(BF16)


def _attention(qa, kva, meta_kva, bias):
    B, S, _ = qa.shape
    NB = S // BLOCK
    assert S % BLOCK == 0 and NB >= 2
    W = KVA_WIDTH
    Q = math.gcd(A_BLOCKS_PER_STEP, NB)
    in_specs = [
        pl.BlockSpec((1, Q * BLOCK, A_WIDTH), lambda b, n: (b, n, 0)),
        pl.BlockSpec((1, BLOCK, W), lambda b, n: (b, jnp.maximum(n * Q - 1, 0), 0)),
        pl.BlockSpec((1, Q * BLOCK, W), lambda b, n: (b, n, 0)),
        pl.BlockSpec((1, BLOCK, W), lambda b, n: (b, jnp.minimum((n + 1) * Q, NB - 1), 0)),
        pl.BlockSpec((N_META, W), lambda b, n: (0, 0)),
        _const_spec(bias.shape),
    ]
    return pl.pallas_call(
        functools.partial(_attn_kernel, n_blocks=NB),
        grid=(B, NB // Q),
        in_specs=in_specs,
        out_specs=pl.BlockSpec((1, Q * BLOCK, A_WIDTH), lambda b, n: (b, n, 0)),
        out_shape=jax.ShapeDtypeStruct((B, S, A_WIDTH), BF16),
        compiler_params=pltpu.CompilerParams(dimension_semantics=("parallel", "parallel"),
                                             vmem_limit_bytes=VMEM_LIMIT_BYTES),
        name="window_attn",
    )(qa, kva, kva, kva, meta_kva, bias)


def _t5_bucket_np(rel):
    nb = N_BUCKETS // 2
    max_exact = nb // 2
    ret = np.where(rel > 0, nb, 0)
    n = np.abs(rel)
    nf = np.maximum(n, 1).astype(np.float32)
    large = max_exact + (np.log(nf / max_exact) / math.log(MAX_DIST / max_exact)
                         * (nb - max_exact)).astype(np.int32)
    large = np.minimum(large, nb - 1)
    return ret + np.where(n < max_exact, n, large)


def _attention_bias(rel_bias, sink_logits, n_blocks):
    HB = BLOCK // 2
    t = np.arange(BLOCK)
    s_off = np.arange(3 * BLOCK) - BLOCK
    rel_band = s_off[None, :] - t[:, None]
    band_ok = np.abs(rel_band) <= WINDOW
    band_bucket = _t5_bucket_np(rel_band)
    q_pos = N_META + np.arange(n_blocks * BLOCK).reshape(n_blocks, BLOCK)
    meta_bucket = _t5_bucket_np(np.arange(N_META)[None, None, :] - q_pos[..., None])
    assert (meta_bucket[1:] == meta_bucket[1:2]).all()
    table = rel_bias.astype(F32)
    lookup = lambda bucket: jnp.einsum(
        '...k,kh->...h', jnp.asarray(np.eye(N_BUCKETS, dtype=np.float32)[bucket]), table,
        precision=lax.Precision.HIGHEST)
    neg = jnp.full((BLOCK, 3 * BLOCK, A_HEADS), NEG, F32)
    variants = []
    for var in range(3):
        ok = band_ok.copy()
        if var == 0:
            ok[:, :BLOCK] = False
        if var == 2:
            ok[:, 2 * BLOCK:] = False
        band = jnp.where(jnp.asarray(ok)[:, :, None], lookup(band_bucket), neg)
        meta = lookup(meta_bucket[0 if var == 0 else 1])
        sink = jnp.broadcast_to(sink_logits.astype(F32)[None, None, :], (BLOCK, 1, A_HEADS))
        padc = jnp.full((BLOCK, HB - N_META - 1, A_HEADS), NEG, F32)
        extra = jnp.concatenate([meta, sink, padc], axis=1)
        halves = [jnp.concatenate([band[:HB, :3 * BLOCK - HB], extra[:HB]], axis=1),
                  jnp.concatenate([extra[HB:], band[HB:, HB:]], axis=1)]
        full = jnp.transpose(jnp.stack(halves), (3, 0, 1, 2))
        full = full.reshape(A_KV_HEADS, A_GROUP, 2, HB, 3 * BLOCK)
        full = jnp.transpose(full, (0, 2, 1, 3, 4)).reshape(A_KV_HEADS, 2, A_GROUP * HB, 3 * BLOCK)
        variants.append(full)
    return jnp.stack(variants) * LOG2E


def _stage_d_kernel(x1_ref, hf_ref, hb_ref, at_ref, nm_ref, wo_ref, mg_ref, wga_ref, wgb_ref,
                    wupm_ref, wupa_ref, wout_ref, nf_ref, wgu_ref, wd_ref, out_ref, a_scr, hm_scr):
    x1 = x1_ref[...]
    hn = _rms(x1, nm_ref[...]).astype(BF16)
    o = _dot(hn, wo_ref[...])
    for h in range(M_HEADS):
        sl = slice(h * M_DV, (h + 1) * M_DV)
        hs = hf_ref[:, sl] + hb_ref[:, sl]
        hm_scr[:, sl] = (_rms(hs, mg_ref[:, sl]) * _sigmoid(o[:, sl])).astype(BF16)
    mixed = _sigmoid(_dot(hn, wga_ref[...])) * _dot(hm_scr[...], wupm_ref[...])
    mixed = mixed + _sigmoid(_dot(hn, wgb_ref[...])) * _dot(at_ref[...], wupa_ref[...])
    x2 = x1 + _dot(mixed.astype(BF16), wout_ref[...])
    out_ref[...] = _ffn_residual(x2, nf_ref, wgu_ref, wd_ref, a_scr)


def _stage_d(x1, hf, hb, at, w, tile):
    n = x1.shape[0]
    row = lambda width: pl.BlockSpec((tile, width), lambda i: (i, 0))
    consts = (w['nm'], w['wo'], w['mg'], w['wga'], w['wgb'], w['wupm'], w['wupa'], w['wout'],
              w['nf2'], w['wgu2'], w['wd2'])
    return pl.pallas_call(
        _stage_d_kernel,
        grid=(n // tile,),
        in_specs=[row(D_MODEL), row(M_WIDTH), row(M_WIDTH), row(A_WIDTH)]
                 + [_const_spec(c.shape) for c in consts],
        out_specs=row(D_MODEL),
        out_shape=jax.ShapeDtypeStruct((n, D_MODEL), F32),
        scratch_shapes=[pltpu.VMEM((tile, D_FF), BF16), pltpu.VMEM((tile, M_WIDTH), BF16)],
        compiler_params=pltpu.CompilerParams(dimension_semantics=("parallel",),
                                             vmem_limit_bytes=VMEM_LIMIT_BYTES),
        name="stage_d",
    )(x1, hf, hb, at, *consts)


def _prep_weights(norm_ffn1, w_ffn1_gu, w_ffn1_down, norm_mix, w_in, b_gates, m_out_gain,
                  q_norm_gain, k_norm_gain, w_up_m, w_up_a, w_out, norm_ffn2, w_ffn2_gu,
                  w_ffn2_down):
    row = lambda v: v.reshape(1, -1).astype(F32)
    o_qm, o_om = 0, 3 * M_WIDTH
    o_gm = o_om + M_WIDTH
    o_qa = o_gm + 4 * M_HEADS
    o_ka = o_qa + A_WIDTH
    o_va = o_ka + A_KV_HEADS * A_DH
    o_ga = o_va + A_KV_HEADS * A_DH
    o_gb = o_ga + D_MODEL
    cols = lambda a, b: w_in[:, a:b]
    n_g = 4 * M_HEADS
    wkvg = jnp.pad(jnp.concatenate([cols(o_ka, o_ga), cols(o_gm, o_qa)], axis=1),
                   ((0, 0), (0, MXU_DIM - n_g)))
    gsum = np.kron(np.eye(LANES // A_DH, dtype=np.float32), np.ones((A_DH, A_DH), np.float32))
    return {
        'nf1': row(norm_ffn1), 'wgu1': w_ffn1_gu.astype(BF16), 'wd1': w_ffn1_down.astype(BF16),
        'nm': row(norm_mix),
        'wqkv': cols(o_qm, o_om).astype(BF16),
        'wo': cols(o_om, o_gm).astype(BF16),
        'bg': jnp.pad(row(b_gates), ((0, 0), (0, LANES - n_g))),
        'wqa': cols(o_qa, o_ka).astype(BF16),
        'wkvg': wkvg.astype(BF16),
        'qg': jnp.tile(row(q_norm_gain), (1, A_HEADS)),
        'kg': jnp.tile(row(k_norm_gain), (1, LANES // A_DH)),
        'gsum': jnp.asarray(np.concatenate([gsum, gsum], axis=0), BF16),
        'wga': cols(o_ga, o_gb).astype(BF16), 'wgb': cols(o_gb, o_gb + D_MODEL).astype(BF16),
        'mg': row(m_out_gain),
        'wupm': w_up_m.astype(BF16), 'wupa': w_up_a.astype(BF16), 'wout': w_out.astype(BF16),
        'nf2': row(norm_ffn2), 'wgu2': w_ffn2_gu.astype(BF16), 'wd2': w_ffn2_down.astype(BF16),
    }


def _encode(x, w, meta, bias_fn, tile, chunk):
    B, S, _ = x.shape
    meta_qkv, meta_gates, meta_kva = meta
    x1, qkv, gates, qa, kva = _stage_a(x.reshape(B * S, D_MODEL), w, tile)
    hf, hb = _mlstm(qkv.reshape(B, S, -1), gates, meta_qkv, meta_gates, chunk)
    at = _attention(qa.reshape(B, S, -1), kva.reshape(B, S, -1), meta_kva, bias_fn(S // BLOCK))
    y = _stage_d(x1, hf.reshape(B * S, -1), hb.reshape(B * S, -1), at.reshape(B * S, -1), w, tile)
    return y.reshape(B, S, D_MODEL)


def _layer(x_groups, meta_tokens, rel_bias, norm_ffn1, w_ffn1_gu, w_ffn1_down, norm_mix, w_in,
           b_gates, m_out_gain, q_norm_gain, k_norm_gain, sink_logits, w_up_m, w_up_a, w_out,
           norm_ffn2, w_ffn2_gu, w_ffn2_down, tile=TOKEN_TILE, chunk=M_CHUNK):
    assert norm_ffn1.shape[0] == 1, "single layer"
    w = _prep_weights(norm_ffn1[0], w_ffn1_gu[0], w_ffn1_down[0], norm_mix[0], w_in[0], b_gates[0],
                      m_out_gain[0], q_norm_gain[0], k_norm_gain[0], w_up_m[0], w_up_a[0],
                      w_out[0], norm_ffn2[0], w_ffn2_gu[0], w_ffn2_down[0])
    meta_x = jnp.pad(meta_tokens.astype(F32), ((0, META_PAD - N_META), (0, 0)))
    _, m_qkv, m_gates, _, m_kva = _stage_a(meta_x, w, META_PAD)
    meta = (m_qkv, m_gates, m_kva[:N_META])
    bias_fn = functools.lru_cache(None)(
        lambda nb: _attention_bias(rel_bias, sink_logits[0], nb))
    return tuple(_encode(x, w, meta, bias_fn, tile, chunk) for x in x_groups)


def kernel(x_prompt, x_sample, meta_tokens, rel_bias, norm_ffn1, w_ffn1_gu, w_ffn1_down, norm_mix,
           w_in, b_gates, m_out_gain, q_norm_gain, k_norm_gain, sink_logits, w_up_m, w_up_a, w_out,
           norm_ffn2, w_ffn2_gu, w_ffn2_down):
    return _layer((x_prompt, x_sample), meta_tokens, rel_bias, norm_ffn1, w_ffn1_gu, w_ffn1_down,
                  norm_mix, w_in, b_gates, m_out_gain, q_norm_gain, k_norm_gain, sink_logits,
                  w_up_m, w_up_a, w_out, norm_ffn2, w_ffn2_gu, w_ffn2_down)
```

```python
import functools
import math

import numpy as np
import jax
import jax.numpy as jnp
from jax import lax
from jax.experimental import pallas as pl
from jax.experimental.pallas import tpu as pltpu

F32 = jnp.float32
BF16 = jnp.bfloat16

D_MODEL = 1024
D_FF = 2816
N_META = 16
M_HEADS = 4
M_DK = 128
M_DV = 128
M_WIDTH = M_HEADS * M_DV
A_HEADS = 8
A_KV_HEADS = 2
A_GROUP = A_HEADS // A_KV_HEADS
A_DH = 64
A_WIDTH = A_HEADS * A_DH
WINDOW = 128
BLOCK = 128
N_BUCKETS = 32
MAX_DIST = 128
EPS = 1e-6
NEG = -1e30
LOG2E = 1.4426950408889634

LANES = 128
MXU_DIM = 256
VMEM_LIMIT_BYTES = 58 * 1024 * 1024

FF_CHUNK = MXU_DIM
N_FF_CHUNKS = D_FF // FF_CHUNK
TOKEN_TILE = 512
M_CHUNK = 256
META_PAD = 128
PREP_CHUNKS = 8
M_CHUNKS_PER_STEP = 8
A_BLOCKS_PER_STEP = 8
KVA_WIDTH = 2 * A_KV_HEADS * LANES

_NN = (((1,), (0,)), ((), ()))
_NT = (((1,), (1,)), ((), ()))


def _dot(a, b, dims=_NN):
    return lax.dot_general(a, b, dims, preferred_element_type=F32)


def _split_dot_lhs(x_f32, a_bf16):
    hi = x_f32.astype(BF16)
    lo = (x_f32 - hi.astype(F32)).astype(BF16)
    return _dot(hi, a_bf16) + _dot(lo, a_bf16)


def _rms(x, g):
    ms = jnp.mean(x * x, axis=-1, keepdims=True)
    return x * lax.rsqrt(ms + EPS) * g


def _sigmoid(x):
    return 1.0 / (1.0 + jnp.exp(-x))


def _log_sigmoid(x):
    return jnp.minimum(x, 0.0) - jnp.log1p(jnp.exp(-jnp.abs(x)))


def _row_streams(n_rows):
    n = 2 if n_rows % (2 * MXU_DIM) == 0 else 1
    return [slice(r * (n_rows // n), (r + 1) * (n_rows // n)) for r in range(n)]


def _const_spec(shape):
    nd = len(shape)
    return pl.BlockSpec(shape, lambda *_: (0,) * nd, pipeline_mode=pl.Buffered(1))


def _ffn_residual(x, nf_ref, wgu_ref, wd_ref, a_scr):
    xn = _rms(x, nf_ref[...]).astype(BF16)
    for j in range(N_FF_CHUNKS):
        sl = slice(j * FF_CHUNK, (j + 1) * FF_CHUNK)
        g = _dot(xn, wgu_ref[:, sl])
        u = _dot(xn, wgu_ref[:, D_FF + j * FF_CHUNK:D_FF + (j + 1) * FF_CHUNK])
        a_scr[:, sl] = (g * _sigmoid(g) * u).astype(BF16)
    y = _dot(a_scr[...], wd_ref[...])
    return x + 0.5 * y


def _group_sumsq(x, gsum_ref):
    sq = x * x
    hi = sq.astype(BF16)
    lo = (sq - hi.astype(F32)).astype(BF16)
    return _dot(jnp.concatenate([hi, lo], axis=1), gsum_ref[...])


def _lane_halves_tiled(x):
    swapped = pltpu.roll(x, LANES // 2, axis=1)
    low = lax.broadcasted_iota(jnp.int32, x.shape, 1) < LANES // 2
    return jnp.where(low, x, swapped), jnp.where(low, swapped, x)


def _stage_a_kernel(x_ref, nf_ref, wgu_ref, wd_ref, nm_ref, wqkv_ref, bg_ref, wqa_ref,
                    wkvg_ref, qg_ref, kg_ref, gsum_ref,
                    x1_ref, qkv_ref, gate_ref, qa_ref, kva_ref, a_scr):
    streams = _row_streams(x_ref.shape[0])

    def project(rows, x1):
        x1_ref[rows] = x1
        hn = _rms(x1, nm_ref[...]).astype(BF16)
        qkv_ref[rows, :M_WIDTH] = (_dot(hn, wqkv_ref[:, :M_WIDTH]) * (M_DK ** -0.5)).astype(BF16)
        qkv_ref[rows, M_WIDTH:] = _dot(hn, wqkv_ref[:, M_WIDTH:]).astype(BF16)
        kvg = _dot(hn, wkvg_ref[...])
        gate_ref[:, rows] = jnp.transpose(kvg[:, 2 * LANES:3 * LANES] + bg_ref[...])[:4 * M_HEADS]
        k = kvg[:, :LANES]
        kn = k * lax.rsqrt(_group_sumsq(k, gsum_ref) * (1.0 / A_DH) + EPS) * kg_ref[...]
        tiles = _lane_halves_tiled(kn) + _lane_halves_tiled(kvg[:, LANES:2 * LANES])
        for t, tiled in enumerate(tiles):
            kva_ref[rows, t * LANES:(t + 1) * LANES] = tiled.astype(BF16)
        q_all = _dot(hn, wqa_ref[...])
        for t in range(A_WIDTH // LANES):
            sl = slice(t * LANES, (t + 1) * LANES)
            q = q_all[:, sl]
            ss = _group_sumsq(q, gsum_ref)
            qn = q * lax.rsqrt(ss * (1.0 / A_DH) + EPS) * qg_ref[:, sl] * (A_DH ** -0.5 * LOG2E)
            qa_ref[rows, sl] = qn.astype(BF16)

    x1 = [_ffn_residual(x_ref[rows], nf_ref, wgu_ref, wd_ref, a_scr.at[rows]) for rows in streams]
    for rows, x in zip(streams, x1):
        project(rows, x)


def _stage_a(x2d, w, tile):
    n = x2d.shape[0]
    assert n % tile == 0
    row = lambda width: pl.BlockSpec((tile, width), lambda i: (i, 0))
    out_shape = (
        jax.ShapeDtypeStruct((n, D_MODEL), F32),
        jax.ShapeDtypeStruct((n, 3 * M_WIDTH), BF16),
        jax.ShapeDtypeStruct((4 * M_HEADS, n), F32),
        jax.ShapeDtypeStruct((n, A_WIDTH), BF16),
        jax.ShapeDtypeStruct((n, KVA_WIDTH), BF16),
    )
    consts = (w['nf1'], w['wgu1'], w['wd1'], w['nm'], w['wqkv'], w['bg'], w['wqa'],
              w['wkvg'], w['qg'], w['kg'], w['gsum'])
    return pl.pallas_call(
        _stage_a_kernel,
        grid=(n // tile,),
        in_specs=[row(D_MODEL)] + [_const_spec(c.shape) for c in consts],
        out_specs=(row(D_MODEL), row(3 * M_WIDTH), pl.BlockSpec((4 * M_HEADS, tile), lambda i: (0, i)),
                   row(A_WIDTH), row(KVA_WIDTH)),
        out_shape=out_shape,
        scratch_shapes=[pltpu.VMEM((tile, D_FF), BF16)],
        compiler_params=pltpu.CompilerParams(dimension_semantics=("parallel",),
                                             vmem_limit_bytes=VMEM_LIMIT_BYTES),
        name="stage_a",
    )(x2d, *consts)


N_UNITS = 2 * M_HEADS


def _running_max(x, L):
    row = lax.broadcasted_iota(jnp.int32, x.shape, 0)
    lane = lax.broadcasted_iota(jnp.int32, x.shape, 1)
    fwd = row < M_HEADS
    k = 1
    while k < L:
        prev = jnp.where(lane >= k, pltpu.roll(x, k, axis=1), NEG)
        nxt = jnp.where(lane < L - k, pltpu.roll(x, L - k, axis=1), NEG)
        x = jnp.maximum(x, jnp.where(fwd, prev, nxt))
        k *= 2
    return x


def _state_update(kt_f32, w_row, vaug, s_old_row, caug):
    wkt = (kt_f32 * w_row).astype(BF16)
    s_old = jnp.concatenate([s_old_row, s_old_row], axis=1)
    return s_old * caug + _dot(wkt, vaug)


def _gate_prep_kernel(g_ref, triu_ref, rows_ref, cols_ref, sc_ref):
    H = M_HEADS
    L = triu_ref.shape[0]
    eye = (lax.broadcasted_iota(jnp.int32, (N_UNITS, LANES), 0)
           == lax.broadcasted_iota(jnp.int32, (N_UNITS, LANES), 1))
    lane_vec = lambda x: jnp.broadcast_to(
        jnp.sum(jnp.where(eye, x, 0.0), axis=0, keepdims=True), (N_UNITS, LANES))
    for j in range(g_ref.shape[1] // L):
        g = g_ref[:, j * L:(j + 1) * L]
        lff = _log_sigmoid(g[H:2 * H])
        lfb = _log_sigmoid(g[3 * H:4 * H])
        cs = _split_dot_lhs(jnp.concatenate([lff, lfb], axis=0), triu_ref[...])
        bf = cs[0:H]
        bb = cs[H:2 * H, L - 1:L] - cs[H:2 * H] + lfb
        b = jnp.concatenate([bf, bb], axis=0)
        r = jnp.concatenate([g[0:H], g[2 * H:3 * H]], axis=0) - b
        b_last = jnp.concatenate([jnp.broadcast_to(bf[:, L - 1:L], (H, LANES)),
                                  jnp.broadcast_to(bb[:, 0:1], (H, LANES))], axis=0)
        a = jnp.concatenate([b_last] * (L // LANES), axis=1) + r
        m_loc = jnp.broadcast_to(jnp.max(a, axis=1, keepdims=True), (N_UNITS, LANES))
        rows_ref[0, :, j * L:(j + 1) * L] = jnp.concatenate([r * LOG2E, a], axis=0)
        packed = jnp.concatenate([_running_max(r, L) * LOG2E, b * LOG2E,
                                  jnp.zeros((LANES - 2 * N_UNITS, L), F32)], axis=0)
        cols_ref[0, j * L:(j + 1) * L, :] = jnp.transpose(packed)
        sc_ref[0, j] = jnp.concatenate([b_last, m_loc, lane_vec(b_last), lane_vec(m_loc)], axis=0)


def _gate_prep(gates_t, B, S, chunk, chunks_per_step):
    G = gates_t.shape[0]
    L = chunk
    NC = S // L
    J = math.gcd(chunks_per_step, NC)
    triu = jnp.asarray(np.triu(np.ones((L, L), np.float32)), BF16)
    return pl.pallas_call(
        _gate_prep_kernel,
        grid=(B, NC // J),
        in_specs=[pl.BlockSpec((G, J * L), lambda b, i: (0, b * (NC // J) + i)),
                  pl.BlockSpec((L, L), lambda b, i: (0, 0))],
        out_specs=(pl.BlockSpec((1, 2 * N_UNITS, J * L), lambda b, i: (b, 0, i)),
                   pl.BlockSpec((1, J * L, LANES), lambda b, i: (b, i, 0)),
                   pl.BlockSpec((1, J, 4 * N_UNITS, LANES), lambda b, i: (b, i, 0, 0))),
        out_shape=(jax.ShapeDtypeStruct((B, 2 * N_UNITS, S), F32),
                   jax.ShapeDtypeStruct((B, S, LANES), F32),
                   jax.ShapeDtypeStruct((B, NC, 4 * N_UNITS, LANES), F32)),
        compiler_params=pltpu.CompilerParams(dimension_semantics=("parallel", "parallel"),
                                             vmem_limit_bytes=VMEM_LIMIT_BYTES),
        name="gate_prep",
    )(gates_t, triu)


def _mlstm_kernel(qf_ref, kf_ref, vf_ref, qb_ref, kb_ref, vb_ref, rf_ref, rb_ref, cf_ref, cb_ref,
                  sf_ref, sb_ref, mk_ref, mv_ref, mg_ref, trim_ref,
                  hf_ref, hb_ref, c_ref, m_ref, *, L):
    H = M_HEADS
    U = N_UNITS
    n_sub = qf_ref.shape[1] // L
    c = pl.program_id(1)

    @pl.when(c == 0)
    def _init():
        c_ref[...] = jnp.zeros_like(c_ref)
        g = mg_ref[...]
        valid = lax.broadcasted_iota(jnp.int32, g.shape, 1) < N_META
        lf = jnp.where(valid, _log_sigmoid(g), 0.0)
        li = jnp.where(valid, g, NEG)
        cs = _split_dot_lhs(lf, trim_ref[...])
        tot = cs[H:2 * H, META_PAD - 1:META_PAD]
        a = tot - cs[H:2 * H] + li[0:H]
        m_new = jnp.maximum(tot, jnp.max(a, axis=1, keepdims=True))
        w = jnp.exp(a - m_new)
        ones_m = jnp.ones((META_PAD, M_DV), BF16)
        for h in range(H):
            sl = slice(h * M_DK, (h + 1) * M_DK)
            vaug = jnp.concatenate([mv_ref[:, sl], ones_m], axis=1)
            kt = jnp.transpose(mk_ref[:, sl].astype(F32))
            c_ref[h] = _state_update(kt, w[h:h + 1], vaug, jnp.zeros((1, LANES), F32),
                                     jnp.zeros(c_ref.shape[1:], F32))
        m_rows = jnp.concatenate([jnp.broadcast_to(m_new, (H, LANES)),
                                  jnp.zeros((H, LANES), F32)], axis=0)
        eye = (lax.broadcasted_iota(jnp.int32, (U, LANES), 0)
               == lax.broadcasted_iota(jnp.int32, (U, LANES), 1))
        m_lanes = jnp.sum(jnp.where(eye, m_rows, 0.0), axis=0, keepdims=True)
        m_ref[...] = jnp.concatenate([m_rows, jnp.broadcast_to(m_lanes, (U, LANES))], axis=0)

    fwd_row = lax.broadcasted_iota(jnp.int32, (U, LANES), 0) < H
    fwd_lane = lax.broadcasted_iota(jnp.int32, (U, LANES), 1) % U < H
    tile_l = lambda x: jnp.concatenate([x] * (L // LANES), axis=1)
    fwd_row_l = tile_l(fwd_row)
    lane_i = lax.broadcasted_iota(jnp.int32, (L, LANES), 1)
    row_i = lax.broadcasted_iota(jnp.int32, (L, L), 0)
    col_i = lax.broadcasted_iota(jnp.int32, (L, L), 1)
    ones = jnp.ones((L, M_DV), BF16)
    m0r = m_ref[0:U]
    m0l = m_ref[U:2 * U]
    for i in range(n_sub):
        jf, jb = i, n_sub - 1 - i
        tf, tb = slice(jf * L, (jf + 1) * L), slice(jb * L, (jb + 1) * L)
        pick_r = lambda n: jnp.where(fwd_row, sf_ref[0, jf, n * U:(n + 1) * U],
                                     sb_ref[0, jb, n * U:(n + 1) * U])
        pick_l = lambda n: jnp.where(fwd_lane, sf_ref[0, jf, n * U:(n + 1) * U],
                                     sb_ref[0, jb, n * U:(n + 1) * U])
        m_new_r = jnp.maximum(pick_r(0) + m0r, pick_r(1))
        m_new_l = jnp.maximum(pick_l(2) + m0l, pick_l(3))
        s_old = jnp.exp(pick_r(0) + m0r - m_new_r)
        r2 = jnp.where(fwd_row_l, rf_ref[0, 0:U, tf], rb_ref[0, 0:U, tb])
        w = jnp.exp(jnp.where(fwd_row_l, rf_ref[0, U:2 * U, tf], rb_ref[0, U:2 * U, tb])
                    - tile_l(m_new_r))
        m2 = m0r * LOG2E
        colsx = jnp.where(lane_i % U < H, cf_ref[0, tf], cb_ref[0, tb])
        g2 = jnp.maximum(colsx, m0l[0:1] * LOG2E)
        e_cols = jnp.exp2(-(pltpu.roll(colsx, LANES - U, axis=1) + g2))
        den = jnp.zeros((L, LANES), F32)
        for u in range(U):
            sl = slice((u % H) * M_DK, (u % H + 1) * M_DK)
            if u < H:
                q, k, v, mask = qf_ref[0, tf, sl], kf_ref[0, tf, sl], vf_ref[0, tf, sl], row_i >= col_i
            else:
                q, k, v, mask = qb_ref[0, tb, sl], kb_ref[0, tb, sl], vb_ref[0, tb, sl], row_i <= col_i
            out, ts = (hf_ref, tf) if u < H else (hb_ref, tb)
            caug = c_ref[u]
            vaug = jnp.concatenate([v, ones], axis=1)
            g_col = g2[:, u:u + 1]
            x_intra = jnp.where(mask, _dot(q, k, _NT) * jnp.exp2(r2[u:u + 1] - g_col), 0.0)
            x_inter = q.astype(F32) * jnp.exp2(m2[u:u + 1] - g_col)
            x = jnp.concatenate([x_intra, x_inter], axis=1).astype(BF16)
            tot = _dot(x, jnp.concatenate([vaug, caug.astype(BF16)], axis=0))
            out[0, ts, sl] = tot[:, :M_DV]
            den = jnp.where(lane_i == u, tot[:, M_DV:], den)
            c_ref[u] = _state_update(jnp.transpose(k.astype(F32)), w[u:u + 1], vaug,
                                     s_old[u:u + 1], caug)
        inv = 1.0 / jnp.maximum(jnp.abs(den), e_cols)
        for u in range(U):
            sl = slice((u % H) * M_DV, (u % H + 1) * M_DV)
            out, ts = (hf_ref, tf) if u < H else (hb_ref, tb)
            out[0, ts, sl] = out[0, ts, sl] * inv[:, u:u + 1]
        m0r, m0l = m_new_r, m_new_l
    m_ref[...] = jnp.concatenate([m0r, m0l], axis=0)


def _mlstm(qkv, gates_t, meta_qkv, meta_gates_t, chunk):
    B, S, _ = qkv.shape
    L = chunk
    assert S % L == 0 and L % LANES == 0
    NC = S // L
    rows, cols, sc = _gate_prep(gates_t, B, S, L, PREP_CHUNKS)
    trim = jnp.asarray(np.triu(np.ones((META_PAD, META_PAD), np.float32)), BF16)
    G = 4 * M_HEADS
    J = math.gcd(M_CHUNKS_PER_STEP, NC)
    T = J * L
    NS = NC // J
    fwd = lambda j: pl.BlockSpec((1, T, M_WIDTH), lambda b, c: (b, c, j))
    bwd = lambda j: pl.BlockSpec((1, T, M_WIDTH), lambda b, c: (b, NS - 1 - c, j))
    in_specs = [
        fwd(0), fwd(1), fwd(2), bwd(0), bwd(1), bwd(2),
        pl.BlockSpec((1, 2 * N_UNITS, T), lambda b, c: (b, 0, c)),
        pl.BlockSpec((1, 2 * N_UNITS, T), lambda b, c: (b, 0, NS - 1 - c)),
        pl.BlockSpec((1, T, LANES), lambda b, c: (b, c, 0)),
        pl.BlockSpec((1, T, LANES), lambda b, c: (b, NS - 1 - c, 0)),
        pl.BlockSpec((1, J, 4 * N_UNITS, LANES), lambda b, c: (b, c, 0, 0)),
        pl.BlockSpec((1, J, 4 * N_UNITS, LANES), lambda b, c: (b, NS - 1 - c, 0, 0)),
        pl.BlockSpec((META_PAD, M_WIDTH), lambda b, c: (0, 1)),
        pl.BlockSpec((META_PAD, M_WIDTH), lambda b, c: (0, 2)),
        pl.BlockSpec((G, META_PAD), lambda b, c: (0, 0)),
        pl.BlockSpec((META_PAD, META_PAD), lambda b, c: (0, 0)),
    ]
    out_specs = (pl.BlockSpec((1, T, M_WIDTH), lambda b, c: (b, c, 0)),
                 pl.BlockSpec((1, T, M_WIDTH), lambda b, c: (b, NS - 1 - c, 0)))
    return pl.pallas_call(
        functools.partial(_mlstm_kernel, L=L),
        grid=(B, NS),
        in_specs=in_specs,
        out_specs=out_specs,
        out_shape=(jax.ShapeDtypeStruct((B, S, M_WIDTH), F32),
                   jax.ShapeDtypeStruct((B, S, M_WIDTH), F32)),
        scratch_shapes=[pltpu.VMEM((N_UNITS, M_DK, 2 * M_DV), F32),
                        pltpu.VMEM((2 * N_UNITS, LANES), F32)],
        compiler_params=pltpu.CompilerParams(dimension_semantics=("parallel", "arbitrary"),
                                             vmem_limit_bytes=VMEM_LIMIT_BYTES),
        name="mlstm",
    )(qkv, qkv, qkv, qkv, qkv, qkv, rows, rows, cols, cols, sc, sc,
      meta_qkv, meta_qkv, meta_gates_t, trim)


def _attn_kernel(q_ref, kp_ref, kc_ref, kn_ref, km_ref, bias_ref, o_ref, *, n_blocks):
    GW = A_GROUP * A_DH
    HB = BLOCK // 2
    n_q = q_ref.shape[1] // BLOCK
    lane = lax.broadcasted_iota(jnp.int32, (HB, GW), 1)
    head_masks = [((lane >= g * A_DH) & (lane < (g + 1) * A_DH)).astype(F32)
                  for g in range(A_GROUP)]
    wide = lambda x: jnp.concatenate([x, x], axis=1)
    zeros = jnp.zeros((HB - N_META, LANES), BF16)
    for i in range(n_q):
        blk = pl.program_id(1) * n_q + i
        var = jnp.where(blk == 0, 0, jnp.where(blk == n_blocks - 1, 2, 1))
        cur = kc_ref[0, i * BLOCK:(i + 1) * BLOCK]
        prev = kp_ref[0] if i == 0 else kc_ref[0, (i - 1) * BLOCK:i * BLOCK]
        nxt = kn_ref[0] if i == n_q - 1 else kc_ref[0, (i + 1) * BLOCK:(i + 2) * BLOCK]
        for j in range(A_KV_HEADS):
            for kind in range(2):
                sl = slice((kind * A_KV_HEADS + j) * LANES, (kind * A_KV_HEADS + j + 1) * LANES)
                extra = jnp.concatenate([km_ref[:, sl], zeros], axis=0)
                first = wide(jnp.concatenate([prev[:, sl], cur[:, sl], nxt[:HB, sl], extra], axis=0))
                second = wide(jnp.concatenate([extra, prev[HB:, sl], cur[:, sl], nxt[:, sl]], axis=0))
                if kind == 0:
                    keys = (first, second)
                else:
                    vals = (first, second)
            for h in range(2):
                rows = slice(i * BLOCK + h * HB, i * BLOCK + (h + 1) * HB)
                qg = q_ref[0, rows, j * GW:(j + 1) * GW].astype(F32)
                qs = jnp.concatenate([(qg * m).astype(BF16) for m in head_masks], axis=0)
                s = _dot(qs, keys[h], _NT) + bias_ref[var, j, h]
                e = jnp.exp2(s - jnp.max(s, axis=1, keepdims=True))
                den = jnp.sum(e, axis=1, keepdims=True)
                o = _dot(e.astype(BF16), vals[h]) / den
                og = o[(A_GROUP - 1) * HB:]
                for g in range(A_GROUP - 2, -1, -1):
                    og = jnp.where(lane < (g + 1) * A_DH, o[g * HB:(g + 1) * HB], og)
                o_ref[0, rows, j * GW:(j + 1) * GW] = og.astype(BF16)


def _attention(qa, kva, meta_kva, bias):
    B, S, _ = qa.shape
    NB = S // BLOCK
    assert S % BLOCK == 0 and NB >= 2
    W = KVA_WIDTH
    Q = math.gcd(A_BLOCKS_PER_STEP, NB)
    in_specs = [
        pl.BlockSpec((1, Q * BLOCK, A_WIDTH), lambda b, n: (b, n, 0)),
        pl.BlockSpec((1, BLOCK, W), lambda b, n: (b, jnp.maximum(n * Q - 1, 0), 0)),
        pl.BlockSpec((1, Q * BLOCK, W), lambda b, n: (b, n, 0)),
        pl.BlockSpec((1, BLOCK, W), lambda b, n: (b, jnp.minimum((n + 1) * Q, NB - 1), 0)),
        pl.BlockSpec((N_META, W), lambda b, n: (0, 0)),
        _const_spec(bias.shape),
    ]
    return pl.pallas_call(
        functools.partial(_attn_kernel, n_blocks=NB),
        grid=(B, NB // Q),
        in_specs=in_specs,
        out_specs=pl.BlockSpec((1, Q * BLOCK, A_WIDTH), lambda b, n: (b, n, 0)),
        out_shape=jax.ShapeDtypeStruct((B, S, A_WIDTH), BF16),
        compiler_params=pltpu.CompilerParams(dimension_semantics=("parallel", "parallel"),
                                             vmem_limit_bytes=VMEM_LIMIT_BYTES),
        name="window_attn",
    )(qa, kva, kva, kva, meta_kva, bias)


def _t5_bucket_np(rel):
    nb = N_BUCKETS // 2
    max_exact = nb // 2
    ret = np.where(rel > 0, nb, 0)
    n = np.abs(rel)
    nf = np.maximum(n, 1).astype(np.float32)
    large = max_exact + (np.log(nf / max_exact) / math.log(MAX_DIST / max_exact)
                         * (nb - max_exact)).astype(np.int32)
    large = np.minimum(large, nb - 1)
    return ret + np.where(n < max_exact, n, large)


def _attention_bias(rel_bias, sink_logits, n_blocks):
    HB = BLOCK // 2
    t = np.arange(BLOCK)
    s_off = np.arange(3 * BLOCK) - BLOCK
    rel_band = s_off[None, :] - t[:, None]
    band_ok = np.abs(rel_band) <= WINDOW
    band_bucket = _t5_bucket_np(rel_band)
    q_pos = N_META + np.arange(n_blocks * BLOCK).reshape(n_blocks, BLOCK)
    meta_bucket = _t5_bucket_np(np.arange(N_META)[None, None, :] - q_pos[..., None])
    assert (meta_bucket[1:] == meta_bucket[1:2]).all()
    table = rel_bias.astype(F32)
    lookup = lambda bucket: jnp.einsum(
        '...k,kh->...h', jnp.asarray(np.eye(N_BUCKETS, dtype=np.float32)[bucket]), table,
        precision=lax.Precision.HIGHEST)
    neg = jnp.full((BLOCK, 3 * BLOCK, A_HEADS), NEG, F32)
    variants = []
    for var in range(3):
        ok = band_ok.copy()
        if var == 0:
            ok[:, :BLOCK] = False
        if var == 2:
            ok[:, 2 * BLOCK:] = False
        band = jnp.where(jnp.asarray(ok)[:, :, None], lookup(band_bucket), neg)
        meta = lookup(meta_bucket[0 if var == 0 else 1])
        sink = jnp.broadcast_to(sink_logits.astype(F32)[None, None, :], (BLOCK, 1, A_HEADS))
        padc = jnp.full((BLOCK, HB - N_META - 1, A_HEADS), NEG, F32)
        extra = jnp.concatenate([meta, sink, padc], axis=1)
        halves = [jnp.concatenate([band[:HB, :3 * BLOCK - HB], extra[:HB]], axis=1),
                  jnp.concatenate([extra[HB:], band[HB:, HB:]], axis=1)]
        full = jnp.transpose(jnp.stack(halves), (3, 0, 1, 2))
        full = full.reshape(A_KV_HEADS, A_GROUP, 2, HB, 3 * BLOCK)
        full = jnp.transpose(full, (0, 2, 1, 3, 4)).reshape(A_KV_HEADS, 2, A_GROUP * HB, 3 * BLOCK)
        variants.append(full)
    return jnp.stack(variants) * LOG2E


def _stage_d_kernel(x1_ref, hf_ref, hb_ref, at_ref, nm_ref, wo_ref, mg_ref, wga_ref, wgb_ref,
                    wupm_ref, wupa_ref, wout_ref, nf_ref, wgu_ref, wd_ref, out_ref, a_scr, hm_scr):
    streams = _row_streams(x1_ref.shape[0])

    def gates(rows):
        x1 = x1_ref[rows]
        hn = _rms(x1, nm_ref[...]).astype(BF16)
        o = _dot(hn, wo_ref[...])
        for h in range(M_HEADS):
            sl = slice(h * M_DV, (h + 1) * M_DV)
            hs = hf_ref[rows, sl] + hb_ref[rows, sl]
            hm_scr[rows, sl] = (_rms(hs, mg_ref[:, sl]) * _sigmoid(o[:, sl])).astype(BF16)
        return x1, hn

    def mix(rows, x1, hn):
        mixed = _sigmoid(_dot(hn, wga_ref[...])) * _dot(hm_scr[rows], wupm_ref[...])
        mixed = mixed + _sigmoid(_dot(hn, wgb_ref[...])) * _dot(at_ref[rows], wupa_ref[...])
        return x1 + _dot(mixed.astype(BF16), wout_ref[...])

    pre = [gates(rows) for rows in streams]
    x2 = [mix(rows, *p) for rows, p in zip(streams, pre)]
    for rows, x in zip(streams, x2):
        out_ref[rows] = _ffn_residual(x, nf_ref, wgu_ref, wd_ref, a_scr.at[rows])


def _stage_d(x1, hf, hb, at, w, tile):
    n = x1.shape[0]
    row = lambda width: pl.BlockSpec((tile, width), lambda i: (i, 0))
    consts = (w['nm'], w['wo'], w['mg'], w['wga'], w['wgb'], w['wupm'], w['wupa'], w['wout'],
              w['nf2'], w['wgu2'], w['wd2'])
    return pl.pallas_call(
        _stage_d_kernel,
        grid=(n // tile,),
        in_specs=[row(D_MODEL), row(M_WIDTH), row(M_WIDTH), row(A_WIDTH)]
                 + [_const_spec(c.shape) for c in consts],
        out_specs=row(D_MODEL),
        out_shape=jax.ShapeDtypeStruct((n, D_MODEL), F32),
        scratch_shapes=[pltpu.VMEM((tile, D_FF), BF16), pltpu.VMEM((tile, M_WIDTH), BF16)],
        compiler_params=pltpu.CompilerParams(dimension_semantics=("parallel",),
                                             vmem_limit_bytes=VMEM_LIMIT_BYTES),
        name="stage_d",
    )(x1, hf, hb, at, *consts)


def _prep_weights(norm_ffn1, w_ffn1_gu, w_ffn1_down, norm_mix, w_in, b_gates, m_out_gain,
                  q_norm_gain, k_norm_gain, w_up_m, w_up_a, w_out, norm_ffn2, w_ffn2_gu,
                  w_ffn2_down):
    row = lambda v: v.reshape(1, -1).astype(F32)
    o_qm, o_om = 0, 3 * M_WIDTH
    o_gm = o_om + M_WIDTH
    o_qa = o_gm + 4 * M_HEADS
    o_ka = o_qa + A_WIDTH
    o_va = o_ka + A_KV_HEADS * A_DH
    o_ga = o_va + A_KV_HEADS * A_DH
    o_gb = o_ga + D_MODEL
    cols = lambda a, b: w_in[:, a:b]
    n_g = 4 * M_HEADS
    wkvg = jnp.pad(jnp.concatenate([cols(o_ka, o_ga), cols(o_gm, o_qa)], axis=1),
                   ((0, 0), (0, MXU_DIM - n_g)))
    gsum = np.kron(np.eye(LANES // A_DH, dtype=np.float32), np.ones((A_DH, A_DH), np.float32))
    return {
        'nf1': row(norm_ffn1), 'wgu1': w_ffn1_gu.astype(BF16), 'wd1': w_ffn1_down.astype(BF16),
        'nm': row(norm_mix),
        'wqkv': cols(o_qm, o_om).astype(BF16),
        'wo': cols(o_om, o_gm).astype(BF16),
        'bg': jnp.pad(row(b_gates), ((0, 0), (0, LANES - n_g))),
        'wqa': cols(o_qa, o_ka).astype(BF16),
        'wkvg': wkvg.astype(BF16),
        'qg': jnp.tile(row(q_norm_gain), (1, A_HEADS)),
        'kg': jnp.tile(row(k_norm_gain), (1, LANES // A_DH)),
        'gsum': jnp.asarray(np.concatenate([gsum, gsum], axis=0), BF16),
        'wga': cols(o_ga, o_gb).astype(BF16), 'wgb': cols(o_gb, o_gb + D_MODEL).astype(BF16),
        'mg': row(m_out_gain),
        'wupm': w_up_m.astype(BF16), 'wupa': w_up_a.astype(BF16), 'wout': w_out.astype(BF16),
        'nf2': row(norm_ffn2), 'wgu2': w_ffn2_gu.astype(BF16), 'wd2': w_ffn2_down.astype(BF16),
    }


def _encode(x, w, meta, bias_fn, tile, chunk):
    B, S, _ = x.shape
    meta_qkv, meta_gates, meta_kva = meta
    x1, qkv, gates, qa, kva = _stage_a(x.reshape(B * S, D_MODEL), w, tile)
    hf, hb = _mlstm(qkv.reshape(B, S, -1), gates, meta_qkv, meta_gates, chunk)
    at = _attention(qa.reshape(B, S, -1), kva.reshape(B, S, -1), meta_kva, bias_fn(S // BLOCK))
    y = _stage_d(x1, hf.reshape(B * S, -1), hb.reshape(B * S, -1), at.reshape(B * S, -1), w, tile)
    return y.reshape(B, S, D_MODEL)


def _layer(x_groups, meta_tokens, rel_bias, norm_ffn1, w_ffn1_gu, w_ffn1_down, norm_mix, w_in,
           b_gates, m_out_gain, q_norm_gain, k_norm_gain, sink_logits, w_up_m, w_up_a, w_out,
           norm_ffn2, w_ffn2_gu, w_ffn2_down, tile=TOKEN_TILE, chunk=M_CHUNK):
    assert norm_ffn1.shape[0] == 1, "single layer"
    w = _prep_weights(norm_ffn1[0], w_ffn1_gu[0], w_ffn1_down[0], norm_mix[0], w_in[0], b_gates[0],
                      m_out_gain[0], q_norm_gain[0], k_norm_gain[0], w_up_m[0], w_up_a[0],
                      w_out[0], norm_ffn2[0], w_ffn2_gu[0], w_ffn2_down[0])
    meta_x = jnp.pad(meta_tokens.astype(F32), ((0, META_PAD - N_META), (0, 0)))
    _, m_qkv, m_gates, _, m_kva = _stage_a(meta_x, w, META_PAD)
    meta = (m_qkv, m_gates, m_kva[:N_META])
    bias_fn = functools.lru_cache(None)(
        lambda nb: _attention_bias(rel_bias, sink_logits[0], nb))
    return tuple(_encode(x, w, meta, bias_fn, tile, chunk) for x in x_groups)


def kernel(x_prompt, x_sample, meta_tokens, rel_bias, norm_ffn1, w_ffn1_gu, w_ffn1_down, norm_mix,
           w_in, b_gates, m_out_gain, q_norm_gain, k_norm_gain, sink_logits, w_up_m, w_up_a, w_out,
           norm_ffn2, w_ffn2_gu, w_ffn2_down):
    return _layer((x_prompt, x_sample), meta_tokens, rel_bias, norm_ffn1, w_ffn1_gu, w_ffn1_down,
                  norm_mix, w_in, b_gates, m_out_gain, q_norm_gain, k_norm_gain, sink_logits,
                  w_up_m, w_up_a, w_out, norm_ffn2, w_ffn2_gu, w_ffn2_down)
```

```python
import functools
import math

import numpy as np
import jax
import jax.numpy as jnp
from jax import lax
from jax.experimental import pallas as pl
from jax.experimental.pallas import tpu as pltpu

F32 = jnp.float32
BF16 = jnp.bfloat16

D_MODEL = 1024
D_FF = 2816
N_META = 16
M_HEADS = 4
M_DK = 128
M_DV = 128
M_WIDTH = M_HEADS * M_DV
A_HEADS = 8
A_KV_HEADS = 2
A_GROUP = A_HEADS // A_KV_HEADS
A_DH = 64
A_WIDTH = A_HEADS * A_DH
WINDOW = 128
BLOCK = 128
N_BUCKETS = 32
MAX_DIST = 128
EPS = 1e-6
NEG = -1e30
LOG2E = 1.4426950408889634

LANES = 128
MXU_DIM = 256
VMEM_LIMIT_BYTES = 58 * 1024 * 1024

FF_CHUNK = MXU_DIM
N_FF_CHUNKS = D_FF // FF_CHUNK
TOKEN_TILE = 512
M_CHUNK = 256
META_PAD = 128
M_CHUNKS_PER_STEP = 8
A_BLOCKS_PER_STEP = 8
KVA_WIDTH = 2 * A_KV_HEADS * LANES

_NN = (((1,), (0,)), ((), ()))
_NT = (((1,), (1,)), ((), ()))


def _dot(a, b, dims=_NN):
    return lax.dot_general(a, b, dims, preferred_element_type=F32)


def _split_dot_lhs(x_f32, a_bf16):
    hi = x_f32.astype(BF16)
    lo = (x_f32 - hi.astype(F32)).astype(BF16)
    return _dot(hi, a_bf16) + _dot(lo, a_bf16)


def _rms(x, g):
    ms = jnp.mean(x * x, axis=-1, keepdims=True)
    return x * lax.rsqrt(ms + EPS) * g


def _sigmoid(x):
    return 1.0 / (1.0 + jnp.exp(-x))


def _log_sigmoid(x):
    return jnp.minimum(x, 0.0) - jnp.log1p(jnp.exp(-jnp.abs(x)))


def _row_streams(n_rows):
    n = 2 if n_rows % (2 * MXU_DIM) == 0 else 1
    return [slice(r * (n_rows // n), (r + 1) * (n_rows // n)) for r in range(n)]


def _const_spec(shape):
    nd = len(shape)
    return pl.BlockSpec(shape, lambda *_: (0,) * nd, pipeline_mode=pl.Buffered(1))


def _ffn_residual(x, nf_ref, wgu_ref, wd_ref, a_scr):
    xn = _rms(x, nf_ref[...]).astype(BF16)
    for j in range(N_FF_CHUNKS):
        sl = slice(j * FF_CHUNK, (j + 1) * FF_CHUNK)
        g = _dot(xn, wgu_ref[:, sl])
        u = _dot(xn, wgu_ref[:, D_FF + j * FF_CHUNK:D_FF + (j + 1) * FF_CHUNK])
        a_scr[:, sl] = (g * _sigmoid(g) * u).astype(BF16)
    y = _dot(a_scr[...], wd_ref[...])
    return x + 0.5 * y


def _group_sumsq(x, gsum_ref):
    sq = x * x
    hi = sq.astype(BF16)
    lo = (sq - hi.astype(F32)).astype(BF16)
    return _dot(jnp.concatenate([hi, lo], axis=1), gsum_ref[...])


def _lane_halves_tiled(x):
    swapped = pltpu.roll(x, LANES // 2, axis=1)
    low = lax.broadcasted_iota(jnp.int32, x.shape, 1) < LANES // 2
    return jnp.where(low, x, swapped), jnp.where(low, swapped, x)


def _stage_a_kernel(x_ref, nf_ref, wgu_ref, wd_ref, nm_ref, wqkv_ref, bg_ref, wqa_ref,
                    wkvg_ref, qg_ref, kg_ref, gsum_ref, triu_ref,
                    x1_ref, qkv_ref, qa_ref, kva_ref, *gate_outs_and_scratch, chunk):
    *gate_outs, a_scr = gate_outs_and_scratch
    streams = _row_streams(x_ref.shape[0])

    def gate_path(rows, x1):
        x1_ref[rows] = x1
        hn = _rms(x1, nm_ref[...]).astype(BF16)
        kvg = _dot(hn, wkvg_ref[...])
        gates_t = jnp.transpose(kvg[:, 2 * LANES:3 * LANES] + bg_ref[...])[:4 * M_HEADS]
        if chunk is None:
            gate_outs[0][:, rows] = gates_t
        else:
            prows_ref, pcols_ref, psc_ref = gate_outs
            for c in range((rows.stop - rows.start) // chunk):
                t0 = rows.start + c * chunk
                p_rows, p_cols, p_sc = _gate_prep_chunk(gates_t[:, c * chunk:(c + 1) * chunk], triu_ref)
                prows_ref[:, t0:t0 + chunk] = p_rows
                pcols_ref[t0:t0 + chunk, :] = p_cols
                psc_ref[t0 // chunk] = p_sc
        k = kvg[:, :LANES]
        kn = k * lax.rsqrt(_group_sumsq(k, gsum_ref) * (1.0 / A_DH) + EPS) * kg_ref[...]
        tiles = _lane_halves_tiled(kn) + _lane_halves_tiled(kvg[:, LANES:2 * LANES])
        for t, tiled in enumerate(tiles):
            kva_ref[rows, t * LANES:(t + 1) * LANES] = tiled.astype(BF16)
        return hn

    def project(rows, hn):
        qkv_ref[rows, :M_WIDTH] = (_dot(hn, wqkv_ref[:, :M_WIDTH]) * (M_DK ** -0.5)).astype(BF16)
        qkv_ref[rows, M_WIDTH:] = _dot(hn, wqkv_ref[:, M_WIDTH:]).astype(BF16)
        q_all = _dot(hn, wqa_ref[...])
        for t in range(A_WIDTH // LANES):
            sl = slice(t * LANES, (t + 1) * LANES)
            q = q_all[:, sl]
            ss = _group_sumsq(q, gsum_ref)
            qn = q * lax.rsqrt(ss * (1.0 / A_DH) + EPS) * qg_ref[:, sl] * (A_DH ** -0.5 * LOG2E)
            qa_ref[rows, sl] = qn.astype(BF16)

    x1 = [_ffn_residual(x_ref[rows], nf_ref, wgu_ref, wd_ref, a_scr.at[rows]) for rows in streams]
    hn = [gate_path(rows, x) for rows, x in zip(streams, x1)]
    for rows, h in zip(streams, hn):
        project(rows, h)


def _stage_a(x2d, w, tile, chunk=None):
    n = x2d.shape[0]
    assert n % tile == 0 and (chunk is None or tile % chunk == 0)
    row = lambda width: pl.BlockSpec((tile, width), lambda i: (i, 0))
    col = lambda height: pl.BlockSpec((height, tile), lambda i: (0, i))
    out_shape = [
        jax.ShapeDtypeStruct((n, D_MODEL), F32),
        jax.ShapeDtypeStruct((n, 3 * M_WIDTH), BF16),
        jax.ShapeDtypeStruct((n, A_WIDTH), BF16),
        jax.ShapeDtypeStruct((n, KVA_WIDTH), BF16),
    ]
    out_specs = [row(D_MODEL), row(3 * M_WIDTH), row(A_WIDTH), row(KVA_WIDTH)]
    if chunk is None:
        out_shape += [jax.ShapeDtypeStruct((4 * M_HEADS, n), F32)]
        out_specs += [col(4 * M_HEADS)]
    else:
        out_shape += [jax.ShapeDtypeStruct((2 * N_UNITS, n), F32),
                      jax.ShapeDtypeStruct((n, LANES), F32),
                      jax.ShapeDtypeStruct((n // chunk, 4 * N_UNITS, LANES), F32)]
        out_specs += [col(2 * N_UNITS), row(LANES),
                      pl.BlockSpec((tile // chunk, 4 * N_UNITS, LANES), lambda i: (i, 0, 0))]
    L = M_CHUNK if chunk is None else chunk
    triu = jnp.asarray(np.triu(np.ones((L, L), np.float32)), BF16)
    consts = (w['nf1'], w['wgu1'], w['wd1'], w['nm'], w['wqkv'], w['bg'], w['wqa'],
              w['wkvg'], w['qg'], w['kg'], w['gsum'], triu)
    return pl.pallas_call(
        functools.partial(_stage_a_kernel, chunk=chunk),
        grid=(n // tile,),
        in_specs=[row(D_MODEL)] + [_const_spec(c.shape) for c in consts],
        out_specs=out_specs,
        out_shape=out_shape,
        scratch_shapes=[pltpu.VMEM((tile, D_FF), BF16)],
        compiler_params=pltpu.CompilerParams(dimension_semantics=("parallel",),
                                             vmem_limit_bytes=VMEM_LIMIT_BYTES),
        name="stage_a",
    )(x2d, *consts)


N_UNITS = 2 * M_HEADS


def _running_max(x, L):
    row = lax.broadcasted_iota(jnp.int32, x.shape, 0)
    lane = lax.broadcasted_iota(jnp.int32, x.shape, 1)
    fwd = row < M_HEADS
    k = 1
    while k < L:
        prev = jnp.where(lane >= k, pltpu.roll(x, k, axis=1), NEG)
        nxt = jnp.where(lane < L - k, pltpu.roll(x, L - k, axis=1), NEG)
        x = jnp.maximum(x, jnp.where(fwd, prev, nxt))
        k *= 2
    return x


def _state_update(kt_f32, w_row, vaug, s_old_row, caug):
    wkt = (kt_f32 * w_row).astype(BF16)
    s_old = jnp.concatenate([s_old_row, s_old_row], axis=1)
    return s_old * caug + _dot(wkt, vaug)


def _gate_prep_chunk(g, triu_ref):
    H = M_HEADS
    L = g.shape[1]
    eye = (lax.broadcasted_iota(jnp.int32, (N_UNITS, LANES), 0)
           == lax.broadcasted_iota(jnp.int32, (N_UNITS, LANES), 1))
    lane_vec = lambda x: jnp.broadcast_to(
        jnp.sum(jnp.where(eye, x, 0.0), axis=0, keepdims=True), (N_UNITS, LANES))
    lff = _log_sigmoid(g[H:2 * H])
    lfb = _log_sigmoid(g[3 * H:4 * H])
    cs = _split_dot_lhs(jnp.concatenate([lff, lfb], axis=0), triu_ref[...])
    bf = cs[0:H]
    bb = cs[H:2 * H, L - 1:L] - cs[H:2 * H] + lfb
    b = jnp.concatenate([bf, bb], axis=0)
    r = jnp.concatenate([g[0:H], g[2 * H:3 * H]], axis=0) - b
    b_last = jnp.concatenate([jnp.broadcast_to(bf[:, L - 1:L], (H, LANES)),
                              jnp.broadcast_to(bb[:, 0:1], (H, LANES))], axis=0)
    a = jnp.concatenate([b_last] * (L // LANES), axis=1) + r
    m_loc = jnp.broadcast_to(jnp.max(a, axis=1, keepdims=True), (N_UNITS, LANES))
    packed = jnp.concatenate([_running_max(r, L) * LOG2E, b * LOG2E,
                              jnp.zeros((LANES - 2 * N_UNITS, L), F32)], axis=0)
    return (jnp.concatenate([r * LOG2E, a], axis=0), jnp.transpose(packed),
            jnp.concatenate([b_last, m_loc, lane_vec(b_last), lane_vec(m_loc)], axis=0))


def _mlstm_kernel(qf_ref, kf_ref, vf_ref, qb_ref, kb_ref, vb_ref, rf_ref, rb_ref, cf_ref, cb_ref,
                  sf_ref, sb_ref, mk_ref, mv_ref, mg_ref, trim_ref,
                  hf_ref, hb_ref, c_ref, m_ref, *, L):
    H = M_HEADS
    U = N_UNITS
    n_sub = qf_ref.shape[1] // L
    c = pl.program_id(1)

    @pl.when(c == 0)
    def _init():
        c_ref[...] = jnp.zeros_like(c_ref)
        g = mg_ref[...]
        valid = lax.broadcasted_iota(jnp.int32, g.shape, 1) < N_META
        lf = jnp.where(valid, _log_sigmoid(g), 0.0)
        li = jnp.where(valid, g, NEG)
        cs = _split_dot_lhs(lf, trim_ref[...])
        tot = cs[H:2 * H, META_PAD - 1:META_PAD]
        a = tot - cs[H:2 * H] + li[0:H]
        m_new = jnp.maximum(tot, jnp.max(a, axis=1, keepdims=True))
        w = jnp.exp(a - m_new)
        ones_m = jnp.ones((META_PAD, M_DV), BF16)
        for h in range(H):
            sl = slice(h * M_DK, (h + 1) * M_DK)
            vaug = jnp.concatenate([mv_ref[:, sl], ones_m], axis=1)
            kt = jnp.transpose(mk_ref[:, sl].astype(F32))
            c_ref[h] = _state_update(kt, w[h:h + 1], vaug, jnp.zeros((1, LANES), F32),
                                     jnp.zeros(c_ref.shape[1:], F32))
        m_rows = jnp.concatenate([jnp.broadcast_to(m_new, (H, LANES)),
                                  jnp.zeros((H, LANES), F32)], axis=0)
        eye = (lax.broadcasted_iota(jnp.int32, (U, LANES), 0)
               == lax.broadcasted_iota(jnp.int32, (U, LANES), 1))
        m_lanes = jnp.sum(jnp.where(eye, m_rows, 0.0), axis=0, keepdims=True)
        m_ref[...] = jnp.concatenate([m_rows, jnp.broadcast_to(m_lanes, (U, LANES))], axis=0)

    fwd_row = lax.broadcasted_iota(jnp.int32, (U, LANES), 0) < H
    fwd_lane = lax.broadcasted_iota(jnp.int32, (U, LANES), 1) % U < H
    tile_l = lambda x: jnp.concatenate([x] * (L // LANES), axis=1)
    fwd_row_l = tile_l(fwd_row)
    lane_i = lax.broadcasted_iota(jnp.int32, (L, LANES), 1)
    row_i = lax.broadcasted_iota(jnp.int32, (L, L), 0)
    col_i = lax.broadcasted_iota(jnp.int32, (L, L), 1)
    ones = jnp.ones((L, M_DV), BF16)
    m0r = m_ref[0:U]
    m0l = m_ref[U:2 * U]
    for i in range(n_sub):
        jf, jb = i, n_sub - 1 - i
        tf, tb = slice(jf * L, (jf + 1) * L), slice(jb * L, (jb + 1) * L)
        pick_r = lambda n: jnp.where(fwd_row, sf_ref[jf, n * U:(n + 1) * U],
                                     sb_ref[jb, n * U:(n + 1) * U])
        pick_l = lambda n: jnp.where(fwd_lane, sf_ref[jf, n * U:(n + 1) * U],
                                     sb_ref[jb, n * U:(n + 1) * U])
        m_new_r = jnp.maximum(pick_r(0) + m0r, pick_r(1))
        m_new_l = jnp.maximum(pick_l(2) + m0l, pick_l(3))
        s_old = jnp.exp(pick_r(0) + m0r - m_new_r)
        r2 = jnp.where(fwd_row_l, rf_ref[0:U, tf], rb_ref[0:U, tb])
        w = jnp.exp(jnp.where(fwd_row_l, rf_ref[U:2 * U, tf], rb_ref[U:2 * U, tb])
                    - tile_l(m_new_r))
        m2 = m0r * LOG2E
        colsx = jnp.where(lane_i % U < H, cf_ref[tf], cb_ref[tb])
        g2 = jnp.maximum(colsx, m0l[0:1] * LOG2E)
        e_cols = jnp.exp2(-(pltpu.roll(colsx, LANES - U, axis=1) + g2))
        den = jnp.zeros((L, LANES), F32)
        for u in range(U):
            sl = slice((u % H) * M_DK, (u % H + 1) * M_DK)
            if u < H:
                q, k, v, mask = qf_ref[0, tf, sl], kf_ref[0, tf, sl], vf_ref[0, tf, sl], row_i >= col_i
            else:
                q, k, v, mask = qb_ref[0, tb, sl], kb_ref[0, tb, sl], vb_ref[0, tb, sl], row_i <= col_i
            out, ts = (hf_ref, tf) if u < H else (hb_ref, tb)
            caug = c_ref[u]
            vaug = jnp.concatenate([v, ones], axis=1)
            g_col = g2[:, u:u + 1]
            x_intra = jnp.where(mask, _dot(q, k, _NT) * jnp.exp2(r2[u:u + 1] - g_col), 0.0)
            x_inter = q.astype(F32) * jnp.exp2(m2[u:u + 1] - g_col)
            x = jnp.concatenate([x_intra, x_inter], axis=1).astype(BF16)
            tot = _dot(x, jnp.concatenate([vaug, caug.astype(BF16)], axis=0))
            out[0, ts, sl] = tot[:, :M_DV]
            den = jnp.where(lane_i == u, tot[:, M_DV:], den)
            c_ref[u] = _state_update(jnp.transpose(k.astype(F32)), w[u:u + 1], vaug,
                                     s_old[u:u + 1], caug)
        inv = 1.0 / jnp.maximum(jnp.abs(den), e_cols)
        for u in range(U):
            sl = slice((u % H) * M_DV, (u % H + 1) * M_DV)
            out, ts = (hf_ref, tf) if u < H else (hb_ref, tb)
            out[0, ts, sl] = out[0, ts, sl] * inv[:, u:u + 1]
        m0r, m0l = m_new_r, m_new_l
    m_ref[...] = jnp.concatenate([m0r, m0l], axis=0)


def _mlstm(qkv, prep, meta_qkv, meta_gates_t, chunk):
    B, S, _ = qkv.shape
    L = chunk
    assert S % L == 0 and L % LANES == 0
    NC = S // L
    rows, cols, sc = prep
    trim = jnp.asarray(np.triu(np.ones((META_PAD, META_PAD), np.float32)), BF16)
    G = 4 * M_HEADS
    J = math.gcd(M_CHUNKS_PER_STEP, NC)
    T = J * L
    NS = NC // J
    fwd = lambda j: pl.BlockSpec((1, T, M_WIDTH), lambda b, c: (b, c, j))
    bwd = lambda j: pl.BlockSpec((1, T, M_WIDTH), lambda b, c: (b, NS - 1 - c, j))
    in_specs = [
        fwd(0), fwd(1), fwd(2), bwd(0), bwd(1), bwd(2),
        pl.BlockSpec((2 * N_UNITS, T), lambda b, c: (0, b * NS + c)),
        pl.BlockSpec((2 * N_UNITS, T), lambda b, c: (0, b * NS + NS - 1 - c)),
        pl.BlockSpec((T, LANES), lambda b, c: (b * NS + c, 0)),
        pl.BlockSpec((T, LANES), lambda b, c: (b * NS + NS - 1 - c, 0)),
        pl.BlockSpec((J, 4 * N_UNITS, LANES), lambda b, c: (b * NS + c, 0, 0)),
        pl.BlockSpec((J, 4 * N_UNITS, LANES), lambda b, c: (b * NS + NS - 1 - c, 0, 0)),
        pl.BlockSpec((META_PAD, M_WIDTH), lambda b, c: (0, 1)),
        pl.BlockSpec((META_PAD, M_WIDTH), lambda b, c: (0, 2)),
        pl.BlockSpec((G, META_PAD), lambda b, c: (0, 0)),
        pl.BlockSpec((META_PAD, META_PAD), lambda b, c: (0, 0)),
    ]
    out_specs = (pl.BlockSpec((1, T, M_WIDTH), lambda b, c: (b, c, 0)),
                 pl.BlockSpec((1, T, M_WIDTH), lambda b, c: (b, NS - 1 - c, 0)))
    return pl.pallas_call(
        functools.partial(_mlstm_kernel, L=L),
        grid=(B, NS),
        in_specs=in_specs,
        out_specs=out_specs,
        out_shape=(jax.ShapeDtypeStruct((B, S, M_WIDTH), F32),
                   jax.ShapeDtypeStruct((B, S, M_WIDTH), F32)),
        scratch_shapes=[pltpu.VMEM((N_UNITS, M_DK, 2 * M_DV), F32),
                        pltpu.VMEM((2 * N_UNITS, LANES), F32)],
        compiler_params=pltpu.CompilerParams(dimension_semantics=("parallel", "arbitrary"),
                                             vmem_limit_bytes=VMEM_LIMIT_BYTES),
        name="mlstm",
    )(qkv, qkv, qkv, qkv, qkv, qkv, rows, rows, cols, cols, sc, sc,
      meta_qkv, meta_qkv, meta_gates_t, trim)


def _attn_kernel(q_ref, kp_ref, kc_ref, kn_ref, km_ref, bias_ref, o_ref, *, n_blocks):
    GW = A_GROUP * A_DH
    HB = BLOCK // 2
    n_q = q_ref.shape[1] // BLOCK
    lane = lax.broadcasted_iota(jnp.int32, (HB, GW), 1)
    head_masks = [((lane >= g * A_DH) & (lane < (g + 1) * A_DH)).astype(F32)
                  for g in range(A_GROUP)]
    wide = lambda x: jnp.concatenate([x, x], axis=1)
    zeros = jnp.zeros((HB - N_META, LANES), BF16)
    for i in range(n_q):
        blk = pl.program_id(1) * n_q + i
        var = jnp.where(blk == 0, 0, jnp.where(blk == n_blocks - 1, 2, 1))
        cur = kc_ref[0, i * BLOCK:(i + 1) * BLOCK]
        prev = kp_ref[0] if i == 0 else kc_ref[0, (i - 1) * BLOCK:i * BLOCK]
        nxt = kn_ref[0] if i == n_q - 1 else kc_ref[0, (i + 1) * BLOCK:(i + 2) * BLOCK]
        for j in range(A_KV_HEADS):
            for kind in range(2):
                sl = slice((kind * A_KV_HEADS + j) * LANES, (kind * A_KV_HEADS + j + 1) * LANES)
                extra = jnp.concatenate([km_ref[:, sl], zeros], axis=0)
                first = wide(jnp.concatenate([prev[:, sl], cur[:, sl], nxt[:HB, sl], extra], axis=0))
                second = wide(jnp.concatenate([extra, prev[HB:, sl], cur[:, sl], nxt[:, sl]], axis=0))
                if kind == 0:
                    keys = (first, second)
                else:
                    vals = (first, second)
            for h in range(2):
                rows = slice(i * BLOCK + h * HB, i * BLOCK + (h + 1) * HB)
                qg = q_ref[0, rows, j * GW:(j + 1) * GW].astype(F32)
                qs = jnp.concatenate([(qg * m).astype(BF16) for m in head_masks], axis=0)
                s = _dot(qs, keys[h], _NT) + bias_ref[var, j, h]
                e = jnp.exp2(s - jnp.max(s, axis=1, keepdims=True))
                den = jnp.sum(e, axis=1, keepdims=True)
                o = _dot(e.astype(BF16), vals[h]) / den
                og = o[(A_GROUP - 1) * HB:]
                for g in range(A_GROUP - 2, -1, -1):
                    og = jnp.where(lane < (g + 1) * A_DH, o[g * HB:(g + 1) * HB], og)
                o_ref[0, rows, j * GW:(j + 1) * GW] = og.astype(BF16)


def _attention(qa, kva, meta_kva, bias):
    B, S, _ = qa.shape
    NB = S // BLOCK
    assert S % BLOCK == 0 and NB >= 2
    W = KVA_WIDTH
    Q = math.gcd(A_BLOCKS_PER_STEP, NB)
    in_specs = [
        pl.BlockSpec((1, Q * BLOCK, A_WIDTH), lambda b, n: (b, n, 0)),
        pl.BlockSpec((1, BLOCK, W), lambda b, n: (b, jnp.maximum(n * Q - 1, 0), 0)),
        pl.BlockSpec((1, Q * BLOCK, W), lambda b, n: (b, n, 0)),
        pl.BlockSpec((1, BLOCK, W), lambda b, n: (b, jnp.minimum((n + 1) * Q, NB - 1), 0)),
        pl.BlockSpec((N_META, W), lambda b, n: (0, 0)),
        _const_spec(bias.shape),
    ]
    return pl.pallas_call(
        functools.partial(_attn_kernel, n_blocks=NB),
        grid=(B, NB // Q),
        in_specs=in_specs,
        out_specs=pl.BlockSpec((1, Q * BLOCK, A_WIDTH), lambda b, n: (b, n, 0)),
        out_shape=jax.ShapeDtypeStruct((B, S, A_WIDTH), BF16),
        compiler_params=pltpu.CompilerParams(dimension_semantics=("parallel", "parallel"),
                                             vmem_limit_bytes=VMEM_LIMIT_BYTES),
        name="window_attn",
    )(qa, kva, kva, kva, meta_kva, bias)


def _t5_bucket_np(rel):
    nb = N_BUCKETS // 2
    max_exact = nb // 2
    ret = np.where(rel > 0, nb, 0)
    n = np.abs(rel)
    nf = np.maximum(n, 1).astype(np.float32)
    large = max_exact + (np.log(nf / max_exact) / math.log(MAX_DIST / max_exact)
                         * (nb - max_exact)).astype(np.int32)
    large = np.minimum(large, nb - 1)
    return ret + np.where(n < max_exact, n, large)


def _attention_bias(rel_bias, sink_logits, n_blocks):
    HB = BLOCK // 2
    t = np.arange(BLOCK)
    s_off = np.arange(3 * BLOCK) - BLOCK
    rel_band = s_off[None, :] - t[:, None]
    band_ok = np.abs(rel_band) <= WINDOW
    band_bucket = _t5_bucket_np(rel_band)
    q_pos = N_META + np.arange(n_blocks * BLOCK).reshape(n_blocks, BLOCK)
    meta_bucket = _t5_bucket_np(np.arange(N_META)[None, None, :] - q_pos[..., None])
    assert (meta_bucket[1:] == meta_bucket[1:2]).all()
    table = rel_bias.astype(F32)
    lookup = lambda bucket: jnp.einsum(
        '...k,kh->...h', jnp.asarray(np.eye(N_BUCKETS, dtype=np.float32)[bucket]), table,
        precision=lax.Precision.HIGHEST)
    neg = jnp.full((BLOCK, 3 * BLOCK, A_HEADS), NEG, F32)
    variants = []
    for var in range(3):
        ok = band_ok.copy()
        if var == 0:
            ok[:, :BLOCK] = False
        if var == 2:
            ok[:, 2 * BLOCK:] = False
        band = jnp.where(jnp.asarray(ok)[:, :, None], lookup(band_bucket), neg)
        meta = lookup(meta_bucket[0 if var == 0 else 1])
        sink = jnp.broadcast_to(sink_logits.astype(F32)[None, None, :], (BLOCK, 1, A_HEADS))
        padc = jnp.full((BLOCK, HB - N_META - 1, A_HEADS), NEG, F32)
        extra = jnp.concatenate([meta, sink, padc], axis=1)
        halves = [jnp.concatenate([band[:HB, :3 * BLOCK - HB], extra[:HB]], axis=1),
                  jnp.concatenate([extra[HB:], band[HB:, HB:]], axis=1)]
        full = jnp.transpose(jnp.stack(halves), (3, 0, 1, 2))
        full = full.reshape(A_KV_HEADS, A_GROUP, 2, HB, 3 * BLOCK)
        full = jnp.transpose(full, (0, 2, 1, 3, 4)).reshape(A_KV_HEADS, 2, A_GROUP * HB, 3 * BLOCK)
        variants.append(full)
    return jnp.stack(variants) * LOG2E


def _stage_d_kernel(x1_ref, hf_ref, hb_ref, at_ref, nm_ref, wo_ref, mg_ref, wga_ref, wgb_ref,
                    wupm_ref, wupa_ref, wout_ref, nf_ref, wgu_ref, wd_ref, out_ref, a_scr, hm_scr):
    streams = _row_streams(x1_ref.shape[0])

    def gates(rows):
        y_a = _dot(at_ref[rows], wupa_ref[...])
        x1 = x1_ref[rows]
        hn = _rms(x1, nm_ref[...]).astype(BF16)
        o = _dot(hn, wo_ref[...])
        for h in range(M_HEADS):
            sl = slice(h * M_DV, (h + 1) * M_DV)
            hs = hf_ref[rows, sl] + hb_ref[rows, sl]
            hm_scr[rows, sl] = (_rms(hs, mg_ref[:, sl]) * _sigmoid(o[:, sl])).astype(BF16)
        return x1, hn, y_a

    def mix(rows, x1, hn, y_a):
        mixed = _sigmoid(_dot(hn, wga_ref[...])) * _dot(hm_scr[rows], wupm_ref[...])
        mixed = mixed + _sigmoid(_dot(hn, wgb_ref[...])) * y_a
        return x1 + _dot(mixed.astype(BF16), wout_ref[...])

    pre = [gates(rows) for rows in streams]
    x2 = [mix(rows, *p) for rows, p in zip(streams, pre)]
    for rows, x in zip(streams, x2):
        out_ref[rows] = _ffn_residual(x, nf_ref, wgu_ref, wd_ref, a_scr.at[rows])


def _stage_d(x1, hf, hb, at, w, tile):
    n = x1.shape[0]
    row = lambda width: pl.BlockSpec((tile, width), lambda i: (i, 0))
    consts = (w['nm'], w['wo'], w['mg'], w['wga'], w['wgb'], w['wupm'], w['wupa'], w['wout'],
              w['nf2'], w['wgu2'], w['wd2'])
    return pl.pallas_call(
        _stage_d_kernel,
        grid=(n // tile,),
        in_specs=[row(D_MODEL), row(M_WIDTH), row(M_WIDTH), row(A_WIDTH)]
                 + [_const_spec(c.shape) for c in consts],
        out_specs=row(D_MODEL),
        out_shape=jax.ShapeDtypeStruct((n, D_MODEL), F32),
        scratch_shapes=[pltpu.VMEM((tile, D_FF), BF16), pltpu.VMEM((tile, M_WIDTH), BF16)],
        compiler_params=pltpu.CompilerParams(dimension_semantics=("parallel",),
                                             vmem_limit_bytes=VMEM_LIMIT_BYTES),
        name="stage_d",
    )(x1, hf, hb, at, *consts)


def _prep_weights(norm_ffn1, w_ffn1_gu, w_ffn1_down, norm_mix, w_in, b_gates, m_out_gain,
                  q_norm_gain, k_norm_gain, w_up_m, w_up_a, w_out, norm_ffn2, w_ffn2_gu,
                  w_ffn2_down):
    row = lambda v: v.reshape(1, -1).astype(F32)
    o_qm, o_om = 0, 3 * M_WIDTH
    o_gm = o_om + M_WIDTH
    o_qa = o_gm + 4 * M_HEADS
    o_ka = o_qa + A_WIDTH
    o_va = o_ka + A_KV_HEADS * A_DH
    o_ga = o_va + A_KV_HEADS * A_DH
    o_gb = o_ga + D_MODEL
    cols = lambda a, b: w_in[:, a:b]
    n_g = 4 * M_HEADS
    wkvg = jnp.pad(jnp.concatenate([cols(o_ka, o_ga), cols(o_gm, o_qa)], axis=1),
                   ((0, 0), (0, MXU_DIM - n_g)))
    gsum = np.kron(np.eye(LANES // A_DH, dtype=np.float32), np.ones((A_DH, A_DH), np.float32))
    return {
        'nf1': row(norm_ffn1), 'wgu1': w_ffn1_gu.astype(BF16), 'wd1': w_ffn1_down.astype(BF16),
        'nm': row(norm_mix),
        'wqkv': cols(o_qm, o_om).astype(BF16),
        'wo': cols(o_om, o_gm).astype(BF16),
        'bg': jnp.pad(row(b_gates), ((0, 0), (0, LANES - n_g))),
        'wqa': cols(o_qa, o_ka).astype(BF16),
        'wkvg': wkvg.astype(BF16),
        'qg': jnp.tile(row(q_norm_gain), (1, A_HEADS)),
        'kg': jnp.tile(row(k_norm_gain), (1, LANES // A_DH)),
        'gsum': jnp.asarray(np.concatenate([gsum, gsum], axis=0), BF16),
        'wga': cols(o_ga, o_gb).astype(BF16), 'wgb': cols(o_gb, o_gb + D_MODEL).astype(BF16),
        'mg': row(m_out_gain),
        'wupm': w_up_m.astype(BF16), 'wupa': w_up_a.astype(BF16), 'wout': w_out.astype(BF16),
        'nf2': row(norm_ffn2), 'wgu2': w_ffn2_gu.astype(BF16), 'wd2': w_ffn2_down.astype(BF16),
    }


def _encode(x, w, meta, bias_fn, tile, chunk):
    B, S, _ = x.shape
    meta_qkv, meta_gates, meta_kva = meta
    x1, qkv, qa, kva, *prep = _stage_a(x.reshape(B * S, D_MODEL), w, tile, chunk)
    hf, hb = _mlstm(qkv.reshape(B, S, -1), prep, meta_qkv, meta_gates, chunk)
    at = _attention(qa.reshape(B, S, -1), kva.reshape(B, S, -1), meta_kva, bias_fn(S // BLOCK))
    y = _stage_d(x1, hf.reshape(B * S, -1), hb.reshape(B * S, -1), at.reshape(B * S, -1), w, tile)
    return y.reshape(B, S, D_MODEL)


def _layer(x_groups, meta_tokens, rel_bias, norm_ffn1, w_ffn1_gu, w_ffn1_down, norm_mix, w_in,
           b_gates, m_out_gain, q_norm_gain, k_norm_gain, sink_logits, w_up_m, w_up_a, w_out,
           norm_ffn2, w_ffn2_gu, w_ffn2_down, tile=TOKEN_TILE, chunk=M_CHUNK):
    assert norm_ffn1.shape[0] == 1, "single layer"
    w = _prep_weights(norm_ffn1[0], w_ffn1_gu[0], w_ffn1_down[0], norm_mix[0], w_in[0], b_gates[0],
                      m_out_gain[0], q_norm_gain[0], k_norm_gain[0], w_up_m[0], w_up_a[0],
                      w_out[0], norm_ffn2[0], w_ffn2_gu[0], w_ffn2_down[0])
    meta_x = jnp.pad(meta_tokens.astype(F32), ((0, META_PAD - N_META), (0, 0)))
    _, m_qkv, _, m_kva, m_gates = _stage_a(meta_x, w, META_PAD)
    meta = (m_qkv, m_gates, m_kva[:N_META])
    bias_fn = functools.lru_cache(None)(
        lambda nb: _attention_bias(rel_bias, sink_logits[0], nb))
    return tuple(_encode(x, w, meta, bias_fn, tile, chunk) for x in x_groups)


def kernel(x_prompt, x_sample, meta_tokens, rel_bias, norm_ffn1, w_ffn1_gu, w_ffn1_down, norm_mix,
           w_in, b_gates, m_out_gain, q_norm_gain, k_norm_gain, sink_logits, w_up_m, w_up_a, w_out,
           norm_ffn2, w_ffn2_gu, w_ffn2_down):
    return _layer((x_prompt, x_sample), meta_tokens, rel_bias, norm_ffn1, w_ffn1_gu, w_ffn1_down,
                  norm_mix, w_in, b_gates, m_out_gain, q_norm_gain, k_norm_gain, sink_logits,
                  w_up_m, w_up_a, w_out, norm_ffn2, w_ffn2_gu, w_ffn2_down)
```

```python
import functools
import math

import numpy as np
import jax
import jax.numpy as jnp
from jax import lax
from jax.experimental import pallas as pl
from jax.experimental.pallas import tpu as pltpu

F32 = jnp.float32
BF16 = jnp.bfloat16

D_MODEL = 1024
D_FF = 2816
N_META = 16
M_HEADS = 4
M_DK = 128
M_DV = 128
M_WIDTH = M_HEADS * M_DV
A_HEADS = 8
A_KV_HEADS = 2
A_GROUP = A_HEADS // A_KV_HEADS
A_DH = 64
A_WIDTH = A_HEADS * A_DH
WINDOW = 128
BLOCK = 128
N_BUCKETS = 32
MAX_DIST = 128
EPS = 1e-6
NEG = -1e30
LOG2E = 1.4426950408889634

LANES = 128
MXU_DIM = 256
VMEM_LIMIT_BYTES = 58 * 1024 * 1024

FF_CHUNK = MXU_DIM
N_FF_CHUNKS = D_FF // FF_CHUNK
TOKEN_TILE = 512
M_CHUNK = 256
META_PAD = 128
M_CHUNKS_PER_STEP = 8
A_BLOCKS_PER_STEP = 8
KVA_WIDTH = 2 * A_KV_HEADS * LANES

_NN = (((1,), (0,)), ((), ()))
_NT = (((1,), (1,)), ((), ()))


def _dot(a, b, dims=_NN):
    return lax.dot_general(a, b, dims, preferred_element_type=F32)


def _split_dot_lhs(x_f32, a_bf16):
    hi = x_f32.astype(BF16)
    lo = (x_f32 - hi.astype(F32)).astype(BF16)
    return _dot(hi, a_bf16) + _dot(lo, a_bf16)


def _rms(x, g):
    ms = jnp.mean(x * x, axis=-1, keepdims=True)
    return x * lax.rsqrt(ms + EPS) * g


def _sigmoid(x):
    return 1.0 / (1.0 + jnp.exp(-x))


def _log_sigmoid(x):
    return jnp.minimum(x, 0.0) - jnp.log1p(jnp.exp(-jnp.abs(x)))


def _row_streams(n_rows):
    n = 2 if n_rows % (2 * MXU_DIM) == 0 else 1
    return [slice(r * (n_rows // n), (r + 1) * (n_rows // n)) for r in range(n)]


def _const_spec(shape):
    nd = len(shape)
    return pl.BlockSpec(shape, lambda *_: (0,) * nd, pipeline_mode=pl.Buffered(1))


def _ffn_residual(x, nf_ref, wgu_ref, wd_ref, a_scr):
    xn = _rms(x, nf_ref[...]).astype(BF16)
    for j in range(N_FF_CHUNKS):
        sl = slice(j * FF_CHUNK, (j + 1) * FF_CHUNK)
        g = _dot(xn, wgu_ref[:, sl])
        u = _dot(xn, wgu_ref[:, D_FF + j * FF_CHUNK:D_FF + (j + 1) * FF_CHUNK])
        a_scr[:, sl] = (g * _sigmoid(g) * u).astype(BF16)
    y = _dot(a_scr[...], wd_ref[...])
    return x + 0.5 * y


def _group_sumsq(x, gsum_ref):
    sq = x * x
    hi = sq.astype(BF16)
    lo = (sq - hi.astype(F32)).astype(BF16)
    return _dot(jnp.concatenate([hi, lo], axis=1), gsum_ref[...])


def _lane_halves_tiled(x):
    swapped = pltpu.roll(x, LANES // 2, axis=1)
    low = lax.broadcasted_iota(jnp.int32, x.shape, 1) < LANES // 2
    return jnp.where(low, x, swapped), jnp.where(low, swapped, x)


def _stage_a_kernel(x_ref, nf_ref, wgu_ref, wd_ref, nm_ref, wqkv_ref, bg_ref, wqa_ref,
                    wkvg_ref, qg_ref, kg_ref, gsum_ref, triu_ref,
                    x1_ref, qkv_ref, qa_ref, kva_ref, *gate_outs_and_scratch, chunk):
    *gate_outs, a_scr = gate_outs_and_scratch
    streams = _row_streams(x_ref.shape[0])

    def gate_path(rows, x1):
        x1_ref[rows] = x1
        hn = _rms(x1, nm_ref[...]).astype(BF16)
        kvg = _dot(hn, wkvg_ref[...])
        gates_t = jnp.transpose(kvg[:, 2 * LANES:3 * LANES] + bg_ref[...])[:4 * M_HEADS]
        if chunk is None:
            gate_outs[0][:, rows] = gates_t
        else:
            prows_ref, pcols_ref, psc_ref = gate_outs
            for c in range((rows.stop - rows.start) // chunk):
                t0 = rows.start + c * chunk
                p_rows, p_cols, p_sc = _gate_prep_chunk(gates_t[:, c * chunk:(c + 1) * chunk], triu_ref)
                prows_ref[:, t0:t0 + chunk] = p_rows
                pcols_ref[t0:t0 + chunk, :] = p_cols
                psc_ref[t0 // chunk] = p_sc
        k = kvg[:, :LANES]
        kn = k * lax.rsqrt(_group_sumsq(k, gsum_ref) * (1.0 / A_DH) + EPS) * kg_ref[...]
        tiles = _lane_halves_tiled(kn) + _lane_halves_tiled(kvg[:, LANES:2 * LANES])
        for t, tiled in enumerate(tiles):
            kva_ref[rows, t * LANES:(t + 1) * LANES] = tiled.astype(BF16)
        return hn

    def project(rows, hn):
        qkv_ref[rows, :M_WIDTH] = (_dot(hn, wqkv_ref[:, :M_WIDTH]) * (M_DK ** -0.5)).astype(BF16)
        qkv_ref[rows, M_WIDTH:] = _dot(hn, wqkv_ref[:, M_WIDTH:]).astype(BF16)
        q_all = _dot(hn, wqa_ref[...])
        for t in range(A_WIDTH // LANES):
            sl = slice(t * LANES, (t + 1) * LANES)
            q = q_all[:, sl]
            ss = _group_sumsq(q, gsum_ref)
            qn = q * lax.rsqrt(ss * (1.0 / A_DH) + EPS) * qg_ref[:, sl] * (A_DH ** -0.5 * LOG2E)
            qa_ref[rows, sl] = qn.astype(BF16)

    x1 = [_ffn_residual(x_ref[rows], nf_ref, wgu_ref, wd_ref, a_scr.at[rows]) for rows in streams]
    hn = [gate_path(rows, x) for rows, x in zip(streams, x1)]
    for rows, h in zip(streams, hn):
        project(rows, h)


def _stage_a(x2d, w, tile, chunk=None):
    n = x2d.shape[0]
    assert n % tile == 0 and (chunk is None or tile % chunk == 0)
    row = lambda width: pl.BlockSpec((tile, width), lambda i: (i, 0))
    col = lambda height: pl.BlockSpec((height, tile), lambda i: (0, i))
    out_shape = [
        jax.ShapeDtypeStruct((n, D_MODEL), F32),
        jax.ShapeDtypeStruct((n, 3 * M_WIDTH), BF16),
        jax.ShapeDtypeStruct((n, A_WIDTH), BF16),
        jax.ShapeDtypeStruct((n, KVA_WIDTH), BF16),
    ]
    out_specs = [row(D_MODEL), row(3 * M_WIDTH), row(A_WIDTH), row(KVA_WIDTH)]
    if chunk is None:
        out_shape += [jax.ShapeDtypeStruct((4 * M_HEADS, n), F32)]
        out_specs += [col(4 * M_HEADS)]
    else:
        out_shape += [jax.ShapeDtypeStruct((2 * N_UNITS, n), F32),
                      jax.ShapeDtypeStruct((n, LANES), F32),
                      jax.ShapeDtypeStruct((n // chunk, 4 * N_UNITS, LANES), F32)]
        out_specs += [col(2 * N_UNITS), row(LANES),
                      pl.BlockSpec((tile // chunk, 4 * N_UNITS, LANES), lambda i: (i, 0, 0))]
    L = M_CHUNK if chunk is None else chunk
    triu = jnp.asarray(np.triu(np.ones((L, L), np.float32)), BF16)
    consts = (w['nf1'], w['wgu1'], w['wd1'], w['nm'], w['wqkv'], w['bg'], w['wqa'],
              w['wkvg'], w['qg'], w['kg'], w['gsum'], triu)
    return pl.pallas_call(
        functools.partial(_stage_a_kernel, chunk=chunk),
        grid=(n // tile,),
        in_specs=[row(D_MODEL)] + [_const_spec(c.shape) for c in consts],
        out_specs=out_specs,
        out_shape=out_shape,
        scratch_shapes=[pltpu.VMEM((tile, D_FF), BF16)],
        compiler_params=pltpu.CompilerParams(dimension_semantics=("parallel",),
                                             vmem_limit_bytes=VMEM_LIMIT_BYTES),
        name="stage_a",
    )(x2d, *consts)


N_UNITS = 2 * M_HEADS


def _running_max(x, L):
    row = lax.broadcasted_iota(jnp.int32, x.shape, 0)
    lane = lax.broadcasted_iota(jnp.int32, x.shape, 1)
    fwd = row < M_HEADS
    k = 1
    while k < L:
        prev = jnp.where(lane >= k, pltpu.roll(x, k, axis=1), NEG)
        nxt = jnp.where(lane < L - k, pltpu.roll(x, L - k, axis=1), NEG)
        x = jnp.maximum(x, jnp.where(fwd, prev, nxt))
        k *= 2
    return x


def _state_update(kt_f32, w_row, vaug, s_old_row, caug):
    wkt = (kt_f32 * w_row).astype(BF16)
    s_old = jnp.concatenate([s_old_row, s_old_row], axis=1)
    return s_old * caug + _dot(wkt, vaug)


def _gate_prep_chunk(g, triu_ref):
    H = M_HEADS
    L = g.shape[1]
    eye = (lax.broadcasted_iota(jnp.int32, (N_UNITS, LANES), 0)
           == lax.broadcasted_iota(jnp.int32, (N_UNITS, LANES), 1))
    lane_vec = lambda x: jnp.broadcast_to(
        jnp.sum(jnp.where(eye, x, 0.0), axis=0, keepdims=True), (N_UNITS, LANES))
    lff = _log_sigmoid(g[H:2 * H])
    lfb = _log_sigmoid(g[3 * H:4 * H])
    cs = _split_dot_lhs(jnp.concatenate([lff, lfb], axis=0), triu_ref[...])
    bf = cs[0:H]
    bb = cs[H:2 * H, L - 1:L] - cs[H:2 * H] + lfb
    b = jnp.concatenate([bf, bb], axis=0)
    r = jnp.concatenate([g[0:H], g[2 * H:3 * H]], axis=0) - b
    b_last = jnp.concatenate([jnp.broadcast_to(bf[:, L - 1:L], (H, LANES)),
                              jnp.broadcast_to(bb[:, 0:1], (H, LANES))], axis=0)
    a = jnp.concatenate([b_last] * (L // LANES), axis=1) + r
    m_loc = jnp.broadcast_to(jnp.max(a, axis=1, keepdims=True), (N_UNITS, LANES))
    packed = jnp.concatenate([_running_max(r, L) * LOG2E, b * LOG2E,
                              jnp.zeros((LANES - 2 * N_UNITS, L), F32)], axis=0)
    return (jnp.concatenate([r * LOG2E, a], axis=0), jnp.transpose(packed),
            jnp.concatenate([b_last, m_loc, lane_vec(b_last), lane_vec(m_loc)], axis=0))


def _mlstm_kernel(qf_ref, kf_ref, vf_ref, qb_ref, kb_ref, vb_ref, rf_ref, rb_ref, cf_ref, cb_ref,
                  sf_ref, sb_ref, mk_ref, mv_ref, mg_ref, trim_ref,
                  hf_ref, hb_ref, c_ref, m_ref, *, L):
    H = M_HEADS
    U = N_UNITS
    n_sub = qf_ref.shape[1] // L
    c = pl.program_id(1)

    @pl.when(c == 0)
    def _init():
        c_ref[...] = jnp.zeros_like(c_ref)
        g = mg_ref[...]
        valid = lax.broadcasted_iota(jnp.int32, g.shape, 1) < N_META
        lf = jnp.where(valid, _log_sigmoid(g), 0.0)
        li = jnp.where(valid, g, NEG)
        cs = _split_dot_lhs(lf, trim_ref[...])
        tot = cs[H:2 * H, META_PAD - 1:META_PAD]
        a = tot - cs[H:2 * H] + li[0:H]
        m_new = jnp.maximum(tot, jnp.max(a, axis=1, keepdims=True))
        w = jnp.exp(a - m_new)
        ones_m = jnp.ones((META_PAD, M_DV), BF16)
        for h in range(H):
            sl = slice(h * M_DK, (h + 1) * M_DK)
            vaug = jnp.concatenate([mv_ref[:, sl], ones_m], axis=1)
            kt = jnp.transpose(mk_ref[:, sl].astype(F32))
            c_ref[h] = _state_update(kt, w[h:h + 1], vaug, jnp.zeros((1, LANES), F32),
                                     jnp.zeros(c_ref.shape[1:], F32))
        m_rows = jnp.concatenate([jnp.broadcast_to(m_new, (H, LANES)),
                                  jnp.zeros((H, LANES), F32)], axis=0)
        eye = (lax.broadcasted_iota(jnp.int32, (U, LANES), 0)
               == lax.broadcasted_iota(jnp.int32, (U, LANES), 1))
        m_lanes = jnp.sum(jnp.where(eye, m_rows, 0.0), axis=0, keepdims=True)
        m_ref[...] = jnp.concatenate([m_rows, jnp.broadcast_to(m_lanes, (U, LANES))], axis=0)

    fwd_row = lax.broadcasted_iota(jnp.int32, (U, LANES), 0) < H
    fwd_lane = lax.broadcasted_iota(jnp.int32, (U, LANES), 1) % U < H
    tile_l = lambda x: jnp.concatenate([x] * (L // LANES), axis=1)
    fwd_row_l = tile_l(fwd_row)
    lane_i = lax.broadcasted_iota(jnp.int32, (L, LANES), 1)
    row_i = lax.broadcasted_iota(jnp.int32, (L, L), 0)
    col_i = lax.broadcasted_iota(jnp.int32, (L, L), 1)
    ones = jnp.ones((L, M_DV), BF16)
    m0r = m_ref[0:U]
    m0l = m_ref[U:2 * U]
    def token_rows(i):
        jf, jb = i, n_sub - 1 - i
        return jf, jb, slice(jf * L, (jf + 1) * L), slice(jb * L, (jb + 1) * L)

    def operands(i, u):
        _, _, tf, tb = token_rows(i)
        sl = slice((u % H) * M_DK, (u % H + 1) * M_DK)
        if u < H:
            return qf_ref[0, tf, sl], kf_ref[0, tf, sl], vf_ref[0, tf, sl], row_i >= col_i
        return qb_ref[0, tb, sl], kb_ref[0, tb, sl], vb_ref[0, tb, sl], row_i <= col_i

    def scores(i, u):
        q, k, _, _ = operands(i, u)
        return _dot(q, k, _NT), jnp.transpose(k.astype(F32))

    order = [(i, u) for i in range(n_sub) for u in range(U)]
    pending = scores(*order[0])
    for i in range(n_sub):
        jf, jb, tf, tb = token_rows(i)
        pick_r = lambda n: jnp.where(fwd_row, sf_ref[jf, n * U:(n + 1) * U],
                                     sb_ref[jb, n * U:(n + 1) * U])
        pick_l = lambda n: jnp.where(fwd_lane, sf_ref[jf, n * U:(n + 1) * U],
                                     sb_ref[jb, n * U:(n + 1) * U])
        m_new_r = jnp.maximum(pick_r(0) + m0r, pick_r(1))
        m_new_l = jnp.maximum(pick_l(2) + m0l, pick_l(3))
        s_old = jnp.exp(pick_r(0) + m0r - m_new_r)
        r2 = jnp.where(fwd_row_l, rf_ref[0:U, tf], rb_ref[0:U, tb])
        w = jnp.exp(jnp.where(fwd_row_l, rf_ref[U:2 * U, tf], rb_ref[U:2 * U, tb])
                    - tile_l(m_new_r))
        m2 = m0r * LOG2E
        colsx = jnp.where(lane_i % U < H, cf_ref[tf], cb_ref[tb])
        g2 = jnp.maximum(colsx, m0l[0:1] * LOG2E)
        e_cols = jnp.exp2(-(pltpu.roll(colsx, LANES - U, axis=1) + g2))
        den = jnp.zeros((L, LANES), F32)
        for u in range(U):
            s, kt = pending
            nxt = i * U + u + 1
            pending = scores(*order[nxt]) if nxt < len(order) else None
            q, _, v, mask = operands(i, u)
            sl = slice((u % H) * M_DK, (u % H + 1) * M_DK)
            out, ts = (hf_ref, tf) if u < H else (hb_ref, tb)
            caug = c_ref[u]
            vaug = jnp.concatenate([v, ones], axis=1)
            g_col = g2[:, u:u + 1]
            x_intra = jnp.where(mask, s * jnp.exp2(r2[u:u + 1] - g_col), 0.0)
            x_inter = q.astype(F32) * jnp.exp2(m2[u:u + 1] - g_col)
            x = jnp.concatenate([x_intra, x_inter], axis=1).astype(BF16)
            tot = _dot(x, jnp.concatenate([vaug, caug.astype(BF16)], axis=0))
            out[0, ts, sl] = tot[:, :M_DV]
            den = jnp.where(lane_i == u, tot[:, M_DV:], den)
            c_ref[u] = _state_update(kt, w[u:u + 1], vaug, s_old[u:u + 1], caug)
        inv = 1.0 / jnp.maximum(jnp.abs(den), e_cols)
        for u in range(U):
            sl = slice((u % H) * M_DV, (u % H + 1) * M_DV)
            out, ts = (hf_ref, tf) if u < H else (hb_ref, tb)
            out[0, ts, sl] = out[0, ts, sl] * inv[:, u:u + 1]
        m0r, m0l = m_new_r, m_new_l
    m_ref[...] = jnp.concatenate([m0r, m0l], axis=0)


def _mlstm(qkv, prep, meta_qkv, meta_gates_t, chunk):
    B, S, _ = qkv.shape
    L = chunk
    assert S % L == 0 and L % LANES == 0
    NC = S // L
    rows, cols, sc = prep
    trim = jnp.asarray(np.triu(np.ones((META_PAD, META_PAD), np.float32)), BF16)
    G = 4 * M_HEADS
    J = math.gcd(M_CHUNKS_PER_STEP, NC)
    T = J * L
    NS = NC // J
    fwd = lambda j: pl.BlockSpec((1, T, M_WIDTH), lambda b, c: (b, c, j))
    bwd = lambda j: pl.BlockSpec((1, T, M_WIDTH), lambda b, c: (b, NS - 1 - c, j))
    in_specs = [
        fwd(0), fwd(1), fwd(2), bwd(0), bwd(1), bwd(2),
        pl.BlockSpec((2 * N_UNITS, T), lambda b, c: (0, b * NS + c)),
        pl.BlockSpec((2 * N_UNITS, T), lambda b, c: (0, b * NS + NS - 1 - c)),
        pl.BlockSpec((T, LANES), lambda b, c: (b * NS + c, 0)),
        pl.BlockSpec((T, LANES), lambda b, c: (b * NS + NS - 1 - c, 0)),
        pl.BlockSpec((J, 4 * N_UNITS, LANES), lambda b, c: (b * NS + c, 0, 0)),
        pl.BlockSpec((J, 4 * N_UNITS, LANES), lambda b, c: (b * NS + NS - 1 - c, 0, 0)),
        pl.BlockSpec((META_PAD, M_WIDTH), lambda b, c: (0, 1)),
        pl.BlockSpec((META_PAD, M_WIDTH), lambda b, c: (0, 2)),
        pl.BlockSpec((G, META_PAD), lambda b, c: (0, 0)),
        pl.BlockSpec((META_PAD, META_PAD), lambda b, c: (0, 0)),
    ]
    out_specs = (pl.BlockSpec((1, T, M_WIDTH), lambda b, c: (b, c, 0)),
                 pl.BlockSpec((1, T, M_WIDTH), lambda b, c: (b, NS - 1 - c, 0)))
    return pl.pallas_call(
        functools.partial(_mlstm_kernel, L=L),
        grid=(B, NS),
        in_specs=in_specs,
        out_specs=out_specs,
        out_shape=(jax.ShapeDtypeStruct((B, S, M_WIDTH), F32),
                   jax.ShapeDtypeStruct((B, S, M_WIDTH), F32)),
        scratch_shapes=[pltpu.VMEM((N_UNITS, M_DK, 2 * M_DV), F32),
                        pltpu.VMEM((2 * N_UNITS, LANES), F32)],
        compiler_params=pltpu.CompilerParams(dimension_semantics=("parallel", "arbitrary"),
                                             vmem_limit_bytes=VMEM_LIMIT_BYTES),
        name="mlstm",
    )(qkv, qkv, qkv, qkv, qkv, qkv, rows, rows, cols, cols, sc, sc,
      meta_qkv, meta_qkv, meta_gates_t, trim)


def _attn_kernel(q_ref, kp_ref, kc_ref, kn_ref, km_ref, bias_ref, o_ref, *, n_blocks):
    GW = A_GROUP * A_DH
    HB = BLOCK // 2
    n_q = q_ref.shape[1] // BLOCK
    lane = lax.broadcasted_iota(jnp.int32, (HB, GW), 1)
    head_masks = [((lane >= g * A_DH) & (lane < (g + 1) * A_DH)).astype(F32)
                  for g in range(A_GROUP)]
    wide = lambda x: jnp.concatenate([x, x], axis=1)
    zeros = jnp.zeros((HB - N_META, LANES), BF16)
    for i in range(n_q):
        blk = pl.program_id(1) * n_q + i
        var = jnp.where(blk == 0, 0, jnp.where(blk == n_blocks - 1, 2, 1))
        cur = kc_ref[0, i * BLOCK:(i + 1) * BLOCK]
        prev = kp_ref[0] if i == 0 else kc_ref[0, (i - 1) * BLOCK:i * BLOCK]
        nxt = kn_ref[0] if i == n_q - 1 else kc_ref[0, (i + 1) * BLOCK:(i + 2) * BLOCK]
        for j in range(A_KV_HEADS):
            for kind in range(2):
                sl = slice((kind * A_KV_HEADS + j) * LANES, (kind * A_KV_HEADS + j + 1) * LANES)
                extra = jnp.concatenate([km_ref[:, sl], zeros], axis=0)
                first = wide(jnp.concatenate([prev[:, sl], cur[:, sl], nxt[:HB, sl], extra], axis=0))
                second = wide(jnp.concatenate([extra, prev[HB:, sl], cur[:, sl], nxt[:, sl]], axis=0))
                if kind == 0:
                    keys = (first, second)
                else:
                    vals = (first, second)
            for h in range(2):
                rows = slice(i * BLOCK + h * HB, i * BLOCK + (h + 1) * HB)
                qg = q_ref[0, rows, j * GW:(j + 1) * GW].astype(F32)
                qs = jnp.concatenate([(qg * m).astype(BF16) for m in head_masks], axis=0)
                s = _dot(qs, keys[h], _NT) + bias_ref[var, j, h]
                e = jnp.exp2(s - jnp.max(s, axis=1, keepdims=True))
                den = jnp.sum(e, axis=1, keepdims=True)
                o = _dot(e.astype(BF16), vals[h]) / den
                og = o[(A_GROUP - 1) * HB:]
                for g in range(A_GROUP - 2, -1, -1):
                    og = jnp.where(lane < (g + 1) * A_DH, o[g * HB:(g + 1) * HB], og)
                o_ref[0, rows, j * GW:(j + 1) * GW] = og.astype(BF16)


def _attention(qa, kva, meta_kva, bias):
    B, S, _ = qa.shape
    NB = S // BLOCK
    assert S % BLOCK == 0 and NB >= 2
    W = KVA_WIDTH
    Q = math.gcd(A_BLOCKS_PER_STEP, NB)
    in_specs = [
        pl.BlockSpec((1, Q * BLOCK, A_WIDTH), lambda b, n: (b, n, 0)),
        pl.BlockSpec((1, BLOCK, W), lambda b, n: (b, jnp.maximum(n * Q - 1, 0), 0)),
        pl.BlockSpec((1, Q * BLOCK, W), lambda b, n: (b, n, 0)),
        pl.BlockSpec((1, BLOCK, W), lambda b, n: (b, jnp.minimum((n + 1) * Q, NB - 1), 0)),
        pl.BlockSpec((N_META, W), lambda b, n: (0, 0)),
        _const_spec(bias.shape),
    ]
    return pl.pallas_call(
        functools.partial(_attn_kernel, n_blocks=NB),
        grid=(B, NB // Q),
        in_specs=in_specs,
        out_specs=pl.BlockSpec((1, Q * BLOCK, A_WIDTH), lambda b, n: (b, n, 0)),
        out_shape=jax.ShapeDtypeStruct((B, S, A_WIDTH), BF16),
        compiler_params=pltpu.CompilerParams(dimension_semantics=("parallel", "parallel"),
                                             vmem_limit_bytes=VMEM_LIMIT_BYTES),
        name="window_attn",
    )(qa, kva, kva, kva, meta_kva, bias)


def _t5_bucket_np(rel):
    nb = N_BUCKETS // 2
    max_exact = nb // 2
    ret = np.where(rel > 0, nb, 0)
    n = np.abs(rel)
    nf = np.maximum(n, 1).astype(np.float32)
    large = max_exact + (np.log(nf / max_exact) / math.log(MAX_DIST / max_exact)
                         * (nb - max_exact)).astype(np.int32)
    large = np.minimum(large, nb - 1)
    return ret + np.where(n < max_exact, n, large)


def _attention_bias(rel_bias, sink_logits, n_blocks):
    HB = BLOCK // 2
    t = np.arange(BLOCK)
    s_off = np.arange(3 * BLOCK) - BLOCK
    rel_band = s_off[None, :] - t[:, None]
    band_ok = np.abs(rel_band) <= WINDOW
    band_bucket = _t5_bucket_np(rel_band)
    q_pos = N_META + np.arange(n_blocks * BLOCK).reshape(n_blocks, BLOCK)
    meta_bucket = _t5_bucket_np(np.arange(N_META)[None, None, :] - q_pos[..., None])
    assert (meta_bucket[1:] == meta_bucket[1:2]).all()
    table = rel_bias.astype(F32)
    lookup = lambda bucket: jnp.einsum(
        '...k,kh->...h', jnp.asarray(np.eye(N_BUCKETS, dtype=np.float32)[bucket]), table,
        precision=lax.Precision.HIGHEST)
    neg = jnp.full((BLOCK, 3 * BLOCK, A_HEADS), NEG, F32)
    variants = []
    for var in range(3):
        ok = band_ok.copy()
        if var == 0:
            ok[:, :BLOCK] = False
        if var == 2:
            ok[:, 2 * BLOCK:] = False
        band = jnp.where(jnp.asarray(ok)[:, :, None], lookup(band_bucket), neg)
        meta = lookup(meta_bucket[0 if var == 0 else 1])
        sink = jnp.broadcast_to(sink_logits.astype(F32)[None, None, :], (BLOCK, 1, A_HEADS))
        padc = jnp.full((BLOCK, HB - N_META - 1, A_HEADS), NEG, F32)
        extra = jnp.concatenate([meta, sink, padc], axis=1)
        halves = [jnp.concatenate([band[:HB, :3 * BLOCK - HB], extra[:HB]], axis=1),
                  jnp.concatenate([extra[HB:], band[HB:, HB:]], axis=1)]
        full = jnp.transpose(jnp.stack(halves), (3, 0, 1, 2))
        full = full.reshape(A_KV_HEADS, A_GROUP, 2, HB, 3 * BLOCK)
        full = jnp.transpose(full, (0, 2, 1, 3, 4)).reshape(A_KV_HEADS, 2, A_GROUP * HB, 3 * BLOCK)
        variants.append(full)
    return jnp.stack(variants) * LOG2E


def _stage_d_kernel(x1_ref, hf_ref, hb_ref, at_ref, nm_ref, wo_ref, mg_ref, wga_ref, wgb_ref,
                    wupm_ref, wupa_ref, wout_ref, nf_ref, wgu_ref, wd_ref, out_ref, a_scr, hm_scr):
    streams = _row_streams(x1_ref.shape[0])

    def gates(rows):
        y_a = _dot(at_ref[rows], wupa_ref[...])
        x1 = x1_ref[rows]
        hn = _rms(x1, nm_ref[...]).astype(BF16)
        o = _dot(hn, wo_ref[...])
        for h in range(M_HEADS):
            sl = slice(h * M_DV, (h + 1) * M_DV)
            hs = hf_ref[rows, sl] + hb_ref[rows, sl]
            hm_scr[rows, sl] = (_rms(hs, mg_ref[:, sl]) * _sigmoid(o[:, sl])).astype(BF16)
        return x1, hn, y_a

    def mix(rows, x1, hn, y_a):
        mixed = _sigmoid(_dot(hn, wga_ref[...])) * _dot(hm_scr[rows], wupm_ref[...])
        mixed = mixed + _sigmoid(_dot(hn, wgb_ref[...])) * y_a
        return x1 + _dot(mixed.astype(BF16), wout_ref[...])

    pre = [gates(rows) for rows in streams]
    x2 = [mix(rows, *p) for rows, p in zip(streams, pre)]
    for rows, x in zip(streams, x2):
        out_ref[rows] = _ffn_residual(x, nf_ref, wgu_ref, wd_ref, a_scr.at[rows])


def _stage_d(x1, hf, hb, at, w, tile):
    n = x1.shape[0]
    row = lambda width: pl.BlockSpec((tile, width), lambda i: (i, 0))
    consts = (w['nm'], w['wo'], w['mg'], w['wga'], w['wgb'], w['wupm'], w['wupa'], w['wout'],
              w['nf2'], w['wgu2'], w['wd2'])
    return pl.pallas_call(
        _stage_d_kernel,
        grid=(n // tile,),
        in_specs=[row(D_MODEL), row(M_WIDTH), row(M_WIDTH), row(A_WIDTH)]
                 + [_const_spec(c.shape) for c in consts],
        out_specs=row(D_MODEL),
        out_shape=jax.ShapeDtypeStruct((n, D_MODEL), F32),
        scratch_shapes=[pltpu.VMEM((tile, D_FF), BF16), pltpu.VMEM((tile, M_WIDTH), BF16)],
        compiler_params=pltpu.CompilerParams(dimension_semantics=("parallel",),
                                             vmem_limit_bytes=VMEM_LIMIT_BYTES),
        name="stage_d",
    )(x1, hf, hb, at, *consts)


def _prep_weights(norm_ffn1, w_ffn1_gu, w_ffn1_down, norm_mix, w_in, b_gates, m_out_gain,
                  q_norm_gain, k_norm_gain, w_up_m, w_up_a, w_out, norm_ffn2, w_ffn2_gu,
                  w_ffn2_down):
    row = lambda v: v.reshape(1, -1).astype(F32)
    o_qm, o_om = 0, 3 * M_WIDTH
    o_gm = o_om + M_WIDTH
    o_qa = o_gm + 4 * M_HEADS
    o_ka = o_qa + A_WIDTH
    o_va = o_ka + A_KV_HEADS * A_DH
    o_ga = o_va + A_KV_HEADS * A_DH
    o_gb = o_ga + D_MODEL
    cols = lambda a, b: w_in[:, a:b]
    n_g = 4 * M_HEADS
    wkvg = jnp.pad(jnp.concatenate([cols(o_ka, o_ga), cols(o_gm, o_qa)], axis=1),
                   ((0, 0), (0, MXU_DIM - n_g)))
    gsum = np.kron(np.eye(LANES // A_DH, dtype=np.float32), np.ones((A_DH, A_DH), np.float32))
    return {
        'nf1': row(norm_ffn1), 'wgu1': w_ffn1_gu.astype(BF16), 'wd1': w_ffn1_down.astype(BF16),
        'nm': row(norm_mix),
        'wqkv': cols(o_qm, o_om).astype(BF16),
        'wo': cols(o_om, o_gm).astype(BF16),
        'bg': jnp.pad(row(b_gates), ((0, 0), (0, LANES - n_g))),
        'wqa': cols(o_qa, o_ka).astype(BF16),
        'wkvg': wkvg.astype(BF16),
        'qg': jnp.tile(row(q_norm_gain), (1, A_HEADS)),
        'kg': jnp.tile(row(k_norm_gain), (1, LANES // A_DH)),
        'gsum': jnp.asarray(np.concatenate([gsum, gsum], axis=0), BF16),
        'wga': cols(o_ga, o_gb).astype(BF16), 'wgb': cols(o_gb, o_gb + D_MODEL).astype(BF16),
        'mg': row(m_out_gain),
        'wupm': w_up_m.astype(BF16), 'wupa': w_up_a.astype(BF16), 'wout': w_out.astype(BF16),
        'nf2': row(norm_ffn2), 'wgu2': w_ffn2_gu.astype(BF16), 'wd2': w_ffn2_down.astype(BF16),
    }


def _encode(x, w, meta, bias_fn, tile, chunk):
    B, S, _ = x.shape
    meta_qkv, meta_gates, meta_kva = meta
    x1, qkv, qa, kva, *prep = _stage_a(x.reshape(B * S, D_MODEL), w, tile, chunk)
    hf, hb = _mlstm(qkv.reshape(B, S, -1), prep, meta_qkv, meta_gates, chunk)
    at = _attention(qa.reshape(B, S, -1), kva.reshape(B, S, -1), meta_kva, bias_fn(S // BLOCK))
    y = _stage_d(x1, hf.reshape(B * S, -1), hb.reshape(B * S, -1), at.reshape(B * S, -1), w, tile)
    return y.reshape(B, S, D_MODEL)


def _layer(x_groups, meta_tokens, rel_bias, norm_ffn1, w_ffn1_gu, w_ffn1_down, norm_mix, w_in,
           b_gates, m_out_gain, q_norm_gain, k_norm_gain, sink_logits, w_up_m, w_up_a, w_out,
           norm_ffn2, w_ffn2_gu, w_ffn2_down, tile=TOKEN_TILE, chunk=M_CHUNK):
    assert norm_ffn1.shape[0] == 1, "single layer"
    w = _prep_weights(norm_ffn1[0], w_ffn1_gu[0], w_ffn1_down[0], norm_mix[0], w_in[0], b_gates[0],
                      m_out_gain[0], q_norm_gain[0], k_norm_gain[0], w_up_m[0], w_up_a[0],
                      w_out[0], norm_ffn2[0], w_ffn2_gu[0], w_ffn2_down[0])
    meta_x = jnp.pad(meta_tokens.astype(F32), ((0, META_PAD - N_META), (0, 0)))
    _, m_qkv, _, m_kva, m_gates = _stage_a(meta_x, w, META_PAD)
    meta = (m_qkv, m_gates, m_kva[:N_META])
    bias_fn = functools.lru_cache(None)(
        lambda nb: _attention_bias(rel_bias, sink_logits[0], nb))
    return tuple(_encode(x, w, meta, bias_fn, tile, chunk) for x in x_groups)


def kernel(x_prompt, x_sample, meta_tokens, rel_bias, norm_ffn1, w_ffn1_gu, w_ffn1_down, norm_mix,
           w_in, b_gates, m_out_gain, q_norm_gain, k_norm_gain, sink_logits, w_up_m, w_up_a, w_out,
           norm_ffn2, w_ffn2_gu, w_ffn2_down):
    return _layer((x_prompt, x_sample), meta_tokens, rel_bias, norm_ffn1, w_ffn1_gu, w_ffn1_down,
                  norm_mix, w_in, b_gates, m_out_gain, q_norm_gain, k_norm_gain, sink_logits,
                  w_up_m, w_up_a, w_out, norm_ffn2, w_ffn2_gu, w_ffn2_down)
```

```python
import functools
import math

import numpy as np
import jax
import jax.numpy as jnp
from jax import lax
from jax.experimental import pallas as pl
from jax.experimental.pallas import tpu as pltpu

F32 = jnp.float32
BF16 = jnp.bfloat16

D_MODEL = 1024
D_FF = 2816
N_META = 16
M_HEADS = 4
M_DK = 128
M_DV = 128
M_WIDTH = M_HEADS * M_DV
A_HEADS = 8
A_KV_HEADS = 2
A_GROUP = A_HEADS // A_KV_HEADS
A_DH = 64
A_WIDTH = A_HEADS * A_DH
WINDOW = 128
BLOCK = 128
N_BUCKETS = 32
MAX_DIST = 128
EPS = 1e-6
NEG = -1e30
LOG2E = 1.4426950408889634

LANES = 128
MXU_DIM = 256
VMEM_LIMIT_BYTES = 58 * 1024 * 1024

FF_CHUNK = MXU_DIM
N_FF_CHUNKS = D_FF // FF_CHUNK
TOKEN_TILE = 512
M_CHUNK = 256
META_PAD = 128
M_CHUNKS_PER_STEP = 8
A_BLOCKS_PER_STEP = 8
KVA_WIDTH = 2 * A_KV_HEADS * LANES

_NN = (((1,), (0,)), ((), ()))
_NT = (((1,), (1,)), ((), ()))


def _dot(a, b, dims=_NN):
    return lax.dot_general(a, b, dims, preferred_element_type=F32)


def _split_dot_lhs(x_f32, a_bf16):
    hi = x_f32.astype(BF16)
    lo = (x_f32 - hi.astype(F32)).astype(BF16)
    return _dot(hi, a_bf16) + _dot(lo, a_bf16)


def _rms(x, g):
    ms = jnp.mean(x * x, axis=-1, keepdims=True)
    return x * lax.rsqrt(ms + EPS) * g


def _sigmoid(x):
    return 1.0 / (1.0 + jnp.exp(-x))


def _log_sigmoid(x):
    return jnp.minimum(x, 0.0) - jnp.log1p(jnp.exp(-jnp.abs(x)))


def _row_streams(n_rows):
    n = 2 if n_rows % (2 * MXU_DIM) == 0 else 1
    return [slice(r * (n_rows // n), (r + 1) * (n_rows // n)) for r in range(n)]


def _const_spec(shape):
    nd = len(shape)
    return pl.BlockSpec(shape, lambda *_: (0,) * nd, pipeline_mode=pl.Buffered(1))


def _ffn_residual(x, nf_ref, wgu_ref, wd_ref, a_scr):
    xn = _rms(x, nf_ref[...]).astype(BF16)
    for j in range(N_FF_CHUNKS):
        sl = slice(j * FF_CHUNK, (j + 1) * FF_CHUNK)
        g = _dot(xn, wgu_ref[:, sl])
        u = _dot(xn, wgu_ref[:, D_FF + j * FF_CHUNK:D_FF + (j + 1) * FF_CHUNK])
        a_scr[:, sl] = (g * _sigmoid(g) * u).astype(BF16)
    y = _dot(a_scr[...], wd_ref[...])
    return x + 0.5 * y


def _group_sumsq(x, gsum_ref):
    sq = x * x
    hi = sq.astype(BF16)
    lo = (sq - hi.astype(F32)).astype(BF16)
    return _dot(jnp.concatenate([hi, lo], axis=1), gsum_ref[...])


def _lane_halves_tiled(x):
    swapped = pltpu.roll(x, LANES // 2, axis=1)
    low = lax.broadcasted_iota(jnp.int32, x.shape, 1) < LANES // 2
    return jnp.where(low, x, swapped), jnp.where(low, swapped, x)


def _stage_a_kernel(x_ref, nf_ref, wgu_ref, wd_ref, nm_ref, wqkv_ref, bg_ref, wqa_ref,
                    wkvg_ref, qg_ref, kg_ref, gsum_ref, triu_ref,
                    x1_ref, qkv_ref, qa_ref, kva_ref, *gate_outs_and_scratch, chunk):
    *gate_outs, a_scr = gate_outs_and_scratch
    streams = _row_streams(x_ref.shape[0])

    def gate_path(rows, x1):
        x1_ref[rows] = x1
        hn = _rms(x1, nm_ref[...]).astype(BF16)
        kvg = _dot(hn, wkvg_ref[...])
        gates_t = jnp.transpose(kvg[:, 2 * LANES:3 * LANES] + bg_ref[...])[:4 * M_HEADS]
        k = kvg[:, :LANES]
        kn = k * lax.rsqrt(_group_sumsq(k, gsum_ref) * (1.0 / A_DH) + EPS) * kg_ref[...]
        tiles = _lane_halves_tiled(kn) + _lane_halves_tiled(kvg[:, LANES:2 * LANES])
        for t, tiled in enumerate(tiles):
            kva_ref[rows, t * LANES:(t + 1) * LANES] = tiled.astype(BF16)
        return hn, gates_t

    def gate_scalars(rows, gates_t):
        if chunk is None:
            gate_outs[0][:, rows] = gates_t
            return
        prows_ref, pcols_ref, psc_ref = gate_outs
        for c in range((rows.stop - rows.start) // chunk):
            t0 = rows.start + c * chunk
            p_rows, p_cols, p_sc = _gate_prep_chunk(gates_t[:, c * chunk:(c + 1) * chunk], triu_ref)
            prows_ref[:, t0:t0 + chunk] = p_rows
            pcols_ref[t0:t0 + chunk, :] = p_cols
            psc_ref[t0 // chunk] = p_sc

    def project(rows, hn):
        qkv_ref[rows, :M_WIDTH] = (_dot(hn, wqkv_ref[:, :M_WIDTH]) * (M_DK ** -0.5)).astype(BF16)
        qkv_ref[rows, M_WIDTH:] = _dot(hn, wqkv_ref[:, M_WIDTH:]).astype(BF16)
        q_all = _dot(hn, wqa_ref[...])
        for t in range(A_WIDTH // LANES):
            sl = slice(t * LANES, (t + 1) * LANES)
            q = q_all[:, sl]
            ss = _group_sumsq(q, gsum_ref)
            qn = q * lax.rsqrt(ss * (1.0 / A_DH) + EPS) * qg_ref[:, sl] * (A_DH ** -0.5 * LOG2E)
            qa_ref[rows, sl] = qn.astype(BF16)

    x1 = [_ffn_residual(x_ref[rows], nf_ref, wgu_ref, wd_ref, a_scr.at[rows]) for rows in streams]
    pre = [gate_path(rows, x) for rows, x in zip(streams, x1)]
    for rows, (hn, gates_t) in zip(streams, pre):
        gate_scalars(rows, gates_t)
        project(rows, hn)


def _stage_a(x2d, w, tile, chunk=None):
    n = x2d.shape[0]
    assert n % tile == 0 and (chunk is None or tile % chunk == 0)
    row = lambda width: pl.BlockSpec((tile, width), lambda i: (i, 0))
    col = lambda height: pl.BlockSpec((height, tile), lambda i: (0, i))
    out_shape = [
        jax.ShapeDtypeStruct((n, D_MODEL), F32),
        jax.ShapeDtypeStruct((n, 3 * M_WIDTH), BF16),
        jax.ShapeDtypeStruct((n, A_WIDTH), BF16),
        jax.ShapeDtypeStruct((n, KVA_WIDTH), BF16),
    ]
    out_specs = [row(D_MODEL), row(3 * M_WIDTH), row(A_WIDTH), row(KVA_WIDTH)]
    if chunk is None:
        out_shape += [jax.ShapeDtypeStruct((4 * M_HEADS, n), F32)]
        out_specs += [col(4 * M_HEADS)]
    else:
        out_shape += [jax.ShapeDtypeStruct((2 * N_UNITS, n), F32),
                      jax.ShapeDtypeStruct((n, LANES), F32),
                      jax.ShapeDtypeStruct((n // chunk, 4 * N_UNITS, LANES), F32)]
        out_specs += [col(2 * N_UNITS), row(LANES),
                      pl.BlockSpec((tile // chunk, 4 * N_UNITS, LANES), lambda i: (i, 0, 0))]
    L = M_CHUNK if chunk is None else chunk
    triu = jnp.asarray(np.triu(np.ones((L, L), np.float32)), BF16)
    consts = (w['nf1'], w['wgu1'], w['wd1'], w['nm'], w['wqkv'], w['bg'], w['wqa'],
              w['wkvg'], w['qg'], w['kg'], w['gsum'], triu)
    return pl.pallas_call(
        functools.partial(_stage_a_kernel, chunk=chunk),
        grid=(n // tile,),
        in_specs=[row(D_MODEL)] + [_const_spec(c.shape) for c in consts],
        out_specs=out_specs,
        out_shape=out_shape,
        scratch_shapes=[pltpu.VMEM((tile, D_FF), BF16)],
        compiler_params=pltpu.CompilerParams(dimension_semantics=("parallel",),
                                             vmem_limit_bytes=VMEM_LIMIT_BYTES),
        name="stage_a",
    )(x2d, *consts)


N_UNITS = 2 * M_HEADS


def _running_max(x, L):
    row = lax.broadcasted_iota(jnp.int32, x.shape, 0)
    lane = lax.broadcasted_iota(jnp.int32, x.shape, 1)
    fwd = row < M_HEADS
    k = 1
    while k < L:
        prev = jnp.where(lane >= k, pltpu.roll(x, k, axis=1), NEG)
        nxt = jnp.where(lane < L - k, pltpu.roll(x, L - k, axis=1), NEG)
        x = jnp.maximum(x, jnp.where(fwd, prev, nxt))
        k *= 2
    return x


def _state_update(kt_f32, w_row, vaug, s_old_row, caug):
    wkt = (kt_f32 * w_row).astype(BF16)
    s_old = jnp.concatenate([s_old_row, s_old_row], axis=1)
    return s_old * caug + _dot(wkt, vaug)


def _gate_prep_chunk(g, triu_ref):
    H = M_HEADS
    L = g.shape[1]
    eye = (lax.broadcasted_iota(jnp.int32, (N_UNITS, LANES), 0)
           == lax.broadcasted_iota(jnp.int32, (N_UNITS, LANES), 1))
    lane_vec = lambda x: jnp.broadcast_to(
        jnp.sum(jnp.where(eye, x, 0.0), axis=0, keepdims=True), (N_UNITS, LANES))
    lff = _log_sigmoid(g[H:2 * H])
    lfb = _log_sigmoid(g[3 * H:4 * H])
    cs = _split_dot_lhs(jnp.concatenate([lff, lfb], axis=0), triu_ref[...])
    bf = cs[0:H]
    bb = cs[H:2 * H, L - 1:L] - cs[H:2 * H] + lfb
    b = jnp.concatenate([bf, bb], axis=0)
    r = jnp.concatenate([g[0:H], g[2 * H:3 * H]], axis=0) - b
    b_last = jnp.concatenate([jnp.broadcast_to(bf[:, L - 1:L], (H, LANES)),
                              jnp.broadcast_to(bb[:, 0:1], (H, LANES))], axis=0)
    a = jnp.concatenate([b_last] * (L // LANES), axis=1) + r
    m_loc = jnp.broadcast_to(jnp.max(a, axis=1, keepdims=True), (N_UNITS, LANES))
    packed = jnp.concatenate([_running_max(r, L) * LOG2E, b * LOG2E,
                              jnp.zeros((LANES - 2 * N_UNITS, L), F32)], axis=0)
    return (jnp.concatenate([r * LOG2E, a], axis=0), jnp.transpose(packed),
            jnp.concatenate([b_last, m_loc, lane_vec(b_last), lane_vec(m_loc)], axis=0))


def _mlstm_kernel(qf_ref, kf_ref, vf_ref, qb_ref, kb_ref, vb_ref, rf_ref, rb_ref, cf_ref, cb_ref,
                  sf_ref, sb_ref, mk_ref, mv_ref, mg_ref, trim_ref,
                  hf_ref, hb_ref, c_ref, m_ref, *, L):
    H = M_HEADS
    U = N_UNITS
    n_sub = qf_ref.shape[1] // L
    c = pl.program_id(1)

    @pl.when(c == 0)
    def _init():
        c_ref[...] = jnp.zeros_like(c_ref)
        g = mg_ref[...]
        valid = lax.broadcasted_iota(jnp.int32, g.shape, 1) < N_META
        lf = jnp.where(valid, _log_sigmoid(g), 0.0)
        li = jnp.where(valid, g, NEG)
        cs = _split_dot_lhs(lf, trim_ref[...])
        tot = cs[H:2 * H, META_PAD - 1:META_PAD]
        a = tot - cs[H:2 * H] + li[0:H]
        m_new = jnp.maximum(tot, jnp.max(a, axis=1, keepdims=True))
        w = jnp.exp(a - m_new)
        ones_m = jnp.ones((META_PAD, M_DV), BF16)
        for h in range(H):
            sl = slice(h * M_DK, (h + 1) * M_DK)
            vaug = jnp.concatenate([mv_ref[:, sl], ones_m], axis=1)
            kt = jnp.transpose(mk_ref[:, sl].astype(F32))
            c_ref[h] = _state_update(kt, w[h:h + 1], vaug, jnp.zeros((1, LANES), F32),
                                     jnp.zeros(c_ref.shape[1:], F32))
        m_rows = jnp.concatenate([jnp.broadcast_to(m_new, (H, LANES)),
                                  jnp.zeros((H, LANES), F32)], axis=0)
        eye = (lax.broadcasted_iota(jnp.int32, (U, LANES), 0)
               == lax.broadcasted_iota(jnp.int32, (U, LANES), 1))
        m_lanes = jnp.sum(jnp.where(eye, m_rows, 0.0), axis=0, keepdims=True)
        m_ref[...] = jnp.concatenate([m_rows, jnp.broadcast_to(m_lanes, (U, LANES))], axis=0)

    fwd_row = lax.broadcasted_iota(jnp.int32, (U, LANES), 0) < H
    fwd_lane = lax.broadcasted_iota(jnp.int32, (U, LANES), 1) % U < H
    tile_l = lambda x: jnp.concatenate([x] * (L // LANES), axis=1)
    fwd_row_l = tile_l(fwd_row)
    lane_i = lax.broadcasted_iota(jnp.int32, (L, LANES), 1)
    row_i = lax.broadcasted_iota(jnp.int32, (L, L), 0)
    col_i = lax.broadcasted_iota(jnp.int32, (L, L), 1)
    ones = jnp.ones((L, M_DV), BF16)
    m0r = m_ref[0:U]
    m0l = m_ref[U:2 * U]
    def token_rows(i):
        jf, jb = i, n_sub - 1 - i
        return jf, jb, slice(jf * L, (jf + 1) * L), slice(jb * L, (jb + 1) * L)

    def operands(i, u):
        _, _, tf, tb = token_rows(i)
        sl = slice((u % H) * M_DK, (u % H + 1) * M_DK)
        if u < H:
            return qf_ref[0, tf, sl], kf_ref[0, tf, sl], vf_ref[0, tf, sl], row_i >= col_i
        return qb_ref[0, tb, sl], kb_ref[0, tb, sl], vb_ref[0, tb, sl], row_i <= col_i

    def scores(i, u):
        q, k, _, _ = operands(i, u)
        return _dot(q, k, _NT), jnp.transpose(k.astype(F32))

    order = [(i, u) for i in range(n_sub) for u in range(U)]
    pending = scores(*order[0])
    for i in range(n_sub):
        jf, jb, tf, tb = token_rows(i)
        pick_r = lambda n: jnp.where(fwd_row, sf_ref[jf, n * U:(n + 1) * U],
                                     sb_ref[jb, n * U:(n + 1) * U])
        pick_l = lambda n: jnp.where(fwd_lane, sf_ref[jf, n * U:(n + 1) * U],
                                     sb_ref[jb, n * U:(n + 1) * U])
        m_new_r = jnp.maximum(pick_r(0) + m0r, pick_r(1))
        m_new_l = jnp.maximum(pick_l(2) + m0l, pick_l(3))
        s_old = jnp.exp(pick_r(0) + m0r - m_new_r)
        r2 = jnp.where(fwd_row_l, rf_ref[0:U, tf], rb_ref[0:U, tb])
        w = jnp.exp(jnp.where(fwd_row_l, rf_ref[U:2 * U, tf], rb_ref[U:2 * U, tb])
                    - tile_l(m_new_r))
        m2 = m0r * LOG2E
        colsx = jnp.where(lane_i % U < H, cf_ref[tf], cb_ref[tb])
        g2 = jnp.maximum(colsx, m0l[0:1] * LOG2E)
        e_cols = jnp.exp2(-(pltpu.roll(colsx, LANES - U, axis=1) + g2))
        den = jnp.zeros((L, LANES), F32)
        for u in range(U):
            s, kt = pending
            nxt = i * U + u + 1
            pending = scores(*order[nxt]) if nxt < len(order) else None
            q, _, v, mask = operands(i, u)
            sl = slice((u % H) * M_DK, (u % H + 1) * M_DK)
            out, ts = (hf_ref, tf) if u < H else (hb_ref, tb)
            caug = c_ref[u]
            vaug = jnp.concatenate([v, ones], axis=1)
            g_col = g2[:, u:u + 1]
            x_intra = jnp.where(mask, s * jnp.exp2(r2[u:u + 1] - g_col), 0.0)
            x_inter = q.astype(F32) * jnp.exp2(m2[u:u + 1] - g_col)
            x = jnp.concatenate([x_intra, x_inter], axis=1).astype(BF16)
            tot = _dot(x, jnp.concatenate([vaug, caug.astype(BF16)], axis=0))
            out[0, ts, sl] = tot[:, :M_DV]
            den = jnp.where(lane_i == u, tot[:, M_DV:], den)
            c_ref[u] = _state_update(kt, w[u:u + 1], vaug, s_old[u:u + 1], caug)
        inv = 1.0 / jnp.maximum(jnp.abs(den), e_cols)
        for u in range(U):
            sl = slice((u % H) * M_DV, (u % H + 1) * M_DV)
            out, ts = (hf_ref, tf) if u < H else (hb_ref, tb)
            out[0, ts, sl] = out[0, ts, sl] * inv[:, u:u + 1]
        m0r, m0l = m_new_r, m_new_l
    m_ref[...] = jnp.concatenate([m0r, m0l], axis=0)


def _mlstm(qkv, prep, meta_qkv, meta_gates_t, chunk):
    B, S, _ = qkv.shape
    L = chunk
    assert S % L == 0 and L % LANES == 0
    NC = S // L
    rows, cols, sc = prep
    trim = jnp.asarray(np.triu(np.ones((META_PAD, META_PAD), np.float32)), BF16)
    G = 4 * M_HEADS
    J = math.gcd(M_CHUNKS_PER_STEP, NC)
    T = J * L
    NS = NC // J
    fwd = lambda j: pl.BlockSpec((1, T, M_WIDTH), lambda b, c: (b, c, j))
    bwd = lambda j: pl.BlockSpec((1, T, M_WIDTH), lambda b, c: (b, NS - 1 - c, j))
    in_specs = [
        fwd(0), fwd(1), fwd(2), bwd(0), bwd(1), bwd(2),
        pl.BlockSpec((2 * N_UNITS, T), lambda b, c: (0, b * NS + c)),
        pl.BlockSpec((2 * N_UNITS, T), lambda b, c: (0, b * NS + NS - 1 - c)),
        pl.BlockSpec((T, LANES), lambda b, c: (b * NS + c, 0)),
        pl.BlockSpec((T, LANES), lambda b, c: (b * NS + NS - 1 - c, 0)),
        pl.BlockSpec((J, 4 * N_UNITS, LANES), lambda b, c: (b * NS + c, 0, 0)),
        pl.BlockSpec((J, 4 * N_UNITS, LANES), lambda b, c: (b * NS + NS - 1 - c, 0, 0)),
        pl.BlockSpec((META_PAD, M_WIDTH), lambda b, c: (0, 1)),
        pl.BlockSpec((META_PAD, M_WIDTH), lambda b, c: (0, 2)),
        pl.BlockSpec((G, META_PAD), lambda b, c: (0, 0)),
        pl.BlockSpec((META_PAD, META_PAD), lambda b, c: (0, 0)),
    ]
    out_specs = (pl.BlockSpec((1, T, M_WIDTH), lambda b, c: (b, c, 0)),
                 pl.BlockSpec((1, T, M_WIDTH), lambda b, c: (b, NS - 1 - c, 0)))
    return pl.pallas_call(
        functools.partial(_mlstm_kernel, L=L),
        grid=(B, NS),
        in_specs=in_specs,
        out_specs=out_specs,
        out_shape=(jax.ShapeDtypeStruct((B, S, M_WIDTH), F32),
                   jax.ShapeDtypeStruct((B, S, M_WIDTH), F32)),
        scratch_shapes=[pltpu.VMEM((N_UNITS, M_DK, 2 * M_DV), F32),
                        pltpu.VMEM((2 * N_UNITS, LANES), F32)],
        compiler_params=pltpu.CompilerParams(dimension_semantics=("parallel", "arbitrary"),
                                             vmem_limit_bytes=VMEM_LIMIT_BYTES),
        name="mlstm",
    )(qkv, qkv, qkv, qkv, qkv, qkv, rows, rows, cols, cols, sc, sc,
      meta_qkv, meta_qkv, meta_gates_t, trim)


def _attn_kernel(q_ref, kp_ref, kc_ref, kn_ref, km_ref, bias_ref, o_ref, *, n_blocks):
    GW = A_GROUP * A_DH
    HB = BLOCK // 2
    n_q = q_ref.shape[1] // BLOCK
    lane = lax.broadcasted_iota(jnp.int32, (HB, GW), 1)
    head_masks = [((lane >= g * A_DH) & (lane < (g + 1) * A_DH)).astype(F32)
                  for g in range(A_GROUP)]
    wide = lambda x: jnp.concatenate([x, x], axis=1)
    zeros = jnp.zeros((HB - N_META, LANES), BF16)
    for i in range(n_q):
        blk = pl.program_id(1) * n_q + i
        var = jnp.where(blk == 0, 0, jnp.where(blk == n_blocks - 1, 2, 1))
        cur = kc_ref[0, i * BLOCK:(i + 1) * BLOCK]
        prev = kp_ref[0] if i == 0 else kc_ref[0, (i - 1) * BLOCK:i * BLOCK]
        nxt = kn_ref[0] if i == n_q - 1 else kc_ref[0, (i + 1) * BLOCK:(i + 2) * BLOCK]
        for j in range(A_KV_HEADS):
            for kind in range(2):
                sl = slice((kind * A_KV_HEADS + j) * LANES, (kind * A_KV_HEADS + j + 1) * LANES)
                extra = jnp.concatenate([km_ref[:, sl], zeros], axis=0)
                first = wide(jnp.concatenate([prev[:, sl], cur[:, sl], nxt[:HB, sl], extra], axis=0))
                second = wide(jnp.concatenate([extra, prev[HB:, sl], cur[:, sl], nxt[:, sl]], axis=0))
                if kind == 0:
                    keys = (first, second)
                else:
                    vals = (first, second)
            for h in range(2):
                rows = slice(i * BLOCK + h * HB, i * BLOCK + (h + 1) * HB)
                qg = q_ref[0, rows, j * GW:(j + 1) * GW].astype(F32)
                qs = jnp.concatenate([(qg * m).astype(BF16) for m in head_masks], axis=0)
                s = _dot(qs, keys[h], _NT) + bias_ref[var, j, h]
                e = jnp.exp2(s - jnp.max(s, axis=1, keepdims=True))
                den = jnp.sum(e, axis=1, keepdims=True)
                o = _dot(e.astype(BF16), vals[h]) / den
                og = o[(A_GROUP - 1) * HB:]
                for g in range(A_GROUP - 2, -1, -1):
                    og = jnp.where(lane < (g + 1) * A_DH, o[g * HB:(g + 1) * HB], og)
                o_ref[0, rows, j * GW:(j + 1) * GW] = og.astype(BF16)


def _attention(qa, kva, meta_kva, bias):
    B, S, _ = qa.shape
    NB = S // BLOCK
    assert S % BLOCK == 0 and NB >= 2
    W = KVA_WIDTH
    Q = math.gcd(A_BLOCKS_PER_STEP, NB)
    in_specs = [
        pl.BlockSpec((1, Q * BLOCK, A_WIDTH), lambda b, n: (b, n, 0)),
        pl.BlockSpec((1, BLOCK, W), lambda b, n: (b, jnp.maximum(n * Q - 1, 0), 0)),
        pl.BlockSpec((1, Q * BLOCK, W), lambda b, n: (b, n, 0)),
        pl.BlockSpec((1, BLOCK, W), lambda b, n: (b, jnp.minimum((n + 1) * Q, NB - 1), 0)),
        pl.BlockSpec((N_META, W), lambda b, n: (0, 0)),
        _const_spec(bias.shape),
    ]
    return pl.pallas_call(
        functools.partial(_attn_kernel, n_blocks=NB),
        grid=(B, NB // Q),
        in_specs=in_specs,
        out_specs=pl.BlockSpec((1, Q * BLOCK, A_WIDTH), lambda b, n: (b, n, 0)),
        out_shape=jax.ShapeDtypeStruct((B, S, A_WIDTH), BF16),
        compiler_params=pltpu.CompilerParams(dimension_semantics=("parallel", "parallel"),
                                             vmem_limit_bytes=VMEM_LIMIT_BYTES),
        name="window_attn",
    )(qa, kva, kva, kva, meta_kva, bias)


def _t5_bucket_np(rel):
    nb = N_BUCKETS // 2
    max_exact = nb // 2
    ret = np.where(rel > 0, nb, 0)
    n = np.abs(rel)
    nf = np.maximum(n, 1).astype(np.float32)
    large = max_exact + (np.log(nf / max_exact) / math.log(MAX_DIST / max_exact)
                         * (nb - max_exact)).astype(np.int32)
    large = np.minimum(large, nb - 1)
    return ret + np.where(n < max_exact, n, large)


def _attention_bias(rel_bias, sink_logits, n_blocks):
    HB = BLOCK // 2
    t = np.arange(BLOCK)
    s_off = np.arange(3 * BLOCK) - BLOCK
    rel_band = s_off[None, :] - t[:, None]
    band_ok = np.abs(rel_band) <= WINDOW
    band_bucket = _t5_bucket_np(rel_band)
    q_pos = N_META + np.arange(n_blocks * BLOCK).reshape(n_blocks, BLOCK)
    meta_bucket = _t5_bucket_np(np.arange(N_META)[None, None, :] - q_pos[..., None])
    assert (meta_bucket[1:] == meta_bucket[1:2]).all()
    table = rel_bias.astype(F32)
    lookup = lambda bucket: jnp.einsum(
        '...k,kh->...h', jnp.asarray(np.eye(N_BUCKETS, dtype=np.float32)[bucket]), table,
        precision=lax.Precision.HIGHEST)
    neg = jnp.full((BLOCK, 3 * BLOCK, A_HEADS), NEG, F32)
    variants = []
    for var in range(3):
        ok = band_ok.copy()
        if var == 0:
            ok[:, :BLOCK] = False
        if var == 2:
            ok[:, 2 * BLOCK:] = False
        band = jnp.where(jnp.asarray(ok)[:, :, None], lookup(band_bucket), neg)
        meta = lookup(meta_bucket[0 if var == 0 else 1])
        sink = jnp.broadcast_to(sink_logits.astype(F32)[None, None, :], (BLOCK, 1, A_HEADS))
        padc = jnp.full((BLOCK, HB - N_META - 1, A_HEADS), NEG, F32)
        extra = jnp.concatenate([meta, sink, padc], axis=1)
        halves = [jnp.concatenate([band[:HB, :3 * BLOCK - HB], extra[:HB]], axis=1),
                  jnp.concatenate([extra[HB:], band[HB:, HB:]], axis=1)]
        full = jnp.transpose(jnp.stack(halves), (3, 0, 1, 2))
        full = full.reshape(A_KV_HEADS, A_GROUP, 2, HB, 3 * BLOCK)
        full = jnp.transpose(full, (0, 2, 1, 3, 4)).reshape(A_KV_HEADS, 2, A_GROUP * HB, 3 * BLOCK)
        variants.append(full)
    return jnp.stack(variants) * LOG2E


def _stage_d_kernel(x1_ref, hf_ref, hb_ref, at_ref, nm_ref, wo_ref, mg_ref, wga_ref, wgb_ref,
                    wupm_ref, wupa_ref, wout_ref, nf_ref, wgu_ref, wd_ref, out_ref, a_scr, hm_scr):
    streams = _row_streams(x1_ref.shape[0])

    def gates(rows):
        y_a = _dot(at_ref[rows], wupa_ref[...])
        x1 = x1_ref[rows]
        hn = _rms(x1, nm_ref[...]).astype(BF16)
        o = _dot(hn, wo_ref[...])
        for h in range(M_HEADS):
            sl = slice(h * M_DV, (h + 1) * M_DV)
            hs = hf_ref[rows, sl] + hb_ref[rows, sl]
            hm_scr[rows, sl] = (_rms(hs, mg_ref[:, sl]) * _sigmoid(o[:, sl])).astype(BF16)
        return x1, hn, y_a

    def mix(rows, x1, hn, y_a):
        mixed = _sigmoid(_dot(hn, wga_ref[...])) * _dot(hm_scr[rows], wupm_ref[...])
        mixed = mixed + _sigmoid(_dot(hn, wgb_ref[...])) * y_a
        return x1 + _dot(mixed.astype(BF16), wout_ref[...])

    pre = [gates(rows) for rows in streams]
    x2 = [mix(rows, *p) for rows, p in zip(streams, pre)]
    for rows, x in zip(streams, x2):
        out_ref[rows] = _ffn_residual(x, nf_ref, wgu_ref, wd_ref, a_scr.at[rows])


def _stage_d(x1, hf, hb, at, w, tile):
    n = x1.shape[0]
    row = lambda width: pl.BlockSpec((tile, width), lambda i: (i, 0))
    consts = (w['nm'], w['wo'], w['mg'], w['wga'], w['wgb'], w['wupm'], w['wupa'], w['wout'],
              w['nf2'], w['wgu2'], w['wd2'])
    return pl.pallas_call(
        _stage_d_kernel,
        grid=(n // tile,),
        in_specs=[row(D_MODEL), row(M_WIDTH), row(M_WIDTH), row(A_WIDTH)]
                 + [_const_spec(c.shape) for c in consts],
        out_specs=row(D_MODEL),
        out_shape=jax.ShapeDtypeStruct((n, D_MODEL), F32),
        scratch_shapes=[pltpu.VMEM((tile, D_FF), BF16), pltpu.VMEM((tile, M_WIDTH), BF16)],
        compiler_params=pltpu.CompilerParams(dimension_semantics=("parallel",),
                                             vmem_limit_bytes=VMEM_LIMIT_BYTES),
        name="stage_d",
    )(x1, hf, hb, at, *consts)


def _prep_weights(norm_ffn1, w_ffn1_gu, w_ffn1_down, norm_mix, w_in, b_gates, m_out_gain,
                  q_norm_gain, k_norm_gain, w_up_m, w_up_a, w_out, norm_ffn2, w_ffn2_gu,
                  w_ffn2_down):
    row = lambda v: v.reshape(1, -1).astype(F32)
    o_qm, o_om = 0, 3 * M_WIDTH
    o_gm = o_om + M_WIDTH
    o_qa = o_gm + 4 * M_HEADS
    o_ka = o_qa + A_WIDTH
    o_va = o_ka + A_KV_HEADS * A_DH
    o_ga = o_va + A_KV_HEADS * A_DH
    o_gb = o_ga + D_MODEL
    cols = lambda a, b: w_in[:, a:b]
    n_g = 4 * M_HEADS
    wkvg = jnp.pad(jnp.concatenate([cols(o_ka, o_ga), cols(o_gm, o_qa)], axis=1),
                   ((0, 0), (0, MXU_DIM - n_g)))
    gsum = np.kron(np.eye(LANES // A_DH, dtype=np.float32), np.ones((A_DH, A_DH), np.float32))
    return {
        'nf1': row(norm_ffn1), 'wgu1': w_ffn1_gu.astype(BF16), 'wd1': w_ffn1_down.astype(BF16),
        'nm': row(norm_mix),
        'wqkv': cols(o_qm, o_om).astype(BF16),
        'wo': cols(o_om, o_gm).astype(BF16),
        'bg': jnp.pad(row(b_gates), ((0, 0), (0, LANES - n_g))),
        'wqa': cols(o_qa, o_ka).astype(BF16),
        'wkvg': wkvg.astype(BF16),
        'qg': jnp.tile(row(q_norm_gain), (1, A_HEADS)),
        'kg': jnp.tile(row(k_norm_gain), (1, LANES // A_DH)),
        'gsum': jnp.asarray(np.concatenate([gsum, gsum], axis=0), BF16),
        'wga': cols(o_ga, o_gb).astype(BF16), 'wgb': cols(o_gb, o_gb + D_MODEL).astype(BF16),
        'mg': row(m_out_gain),
        'wupm': w_up_m.astype(BF16), 'wupa': w_up_a.astype(BF16), 'wout': w_out.astype(BF16),
        'nf2': row(norm_ffn2), 'wgu2': w_ffn2_gu.astype(BF16), 'wd2': w_ffn2_down.astype(BF16),
    }


def _encode(x, w, meta, bias_fn, tile, chunk):
    B, S, _ = x.shape
    meta_qkv, meta_gates, meta_kva = meta
    x1, qkv, qa, kva, *prep = _stage_a(x.reshape(B * S, D_MODEL), w, tile, chunk)
    hf, hb = _mlstm(qkv.reshape(B, S, -1), prep, meta_qkv, meta_gates, chunk)
    at = _attention(qa.reshape(B, S, -1), kva.reshape(B, S, -1), meta_kva, bias_fn(S // BLOCK))
    y = _stage_d(x1, hf.reshape(B * S, -1), hb.reshape(B * S, -1), at.reshape(B * S, -1), w, tile)
    return y.reshape(B, S, D_MODEL)


def _layer(x_groups, meta_tokens, rel_bias, norm_ffn1, w_ffn1_gu, w_ffn1_down, norm_mix, w_in,
           b_gates, m_out_gain, q_norm_gain, k_norm_gain, sink_logits, w_up_m, w_up_a, w_out,
           norm_ffn2, w_ffn2_gu, w_ffn2_down, tile=TOKEN_TILE, chunk=M_CHUNK):
    assert norm_ffn1.shape[0] == 1, "single layer"
    w = _prep_weights(norm_ffn1[0], w_ffn1_gu[0], w_ffn1_down[0], norm_mix[0], w_in[0], b_gates[0],
                      m_out_gain[0], q_norm_gain[0], k_norm_gain[0], w_up_m[0], w_up_a[0],
                      w_out[0], norm_ffn2[0], w_ffn2_gu[0], w_ffn2_down[0])
    meta_x = jnp.pad(meta_tokens.astype(F32), ((0, META_PAD - N_META), (0, 0)))
    _, m_qkv, _, m_kva, m_gates = _stage_a(meta_x, w, META_PAD)
    meta = (m_qkv, m_gates, m_kva[:N_META])
    bias_fn = functools.lru_cache(None)(
        lambda nb: _attention_bias(rel_bias, sink_logits[0], nb))
    return tuple(_encode(x, w, meta, bias_fn, tile, chunk) for x in x_groups)


def kernel(x_prompt, x_sample, meta_tokens, rel_bias, norm_ffn1, w_ffn1_gu, w_ffn1_down, norm_mix,
           w_in, b_gates, m_out_gain, q_norm_gain, k_norm_gain, sink_logits, w_up_m, w_up_a, w_out,
           norm_ffn2, w_ffn2_gu, w_ffn2_down):
    return _layer((x_prompt, x_sample), meta_tokens, rel_bias, norm_ffn1, w_ffn1_gu, w_ffn1_down,
                  norm_mix, w_in, b_gates, m_out_gain, q_norm_gain, k_norm_gain, sink_logits,
                  w_up_m, w_up_a, w_out, norm_ffn2, w_ffn2_gu, w_ffn2_down)
```

```python
import functools
import math

import numpy as np
import jax
import jax.numpy as jnp
from jax import lax
from jax.experimental import pallas as pl
from jax.experimental.pallas import tpu as pltpu

F32 = jnp.float32
BF16 = jnp.bfloat16

D_MODEL = 1024
D_FF = 2816
N_META = 16
M_HEADS = 4
M_DK = 128
M_DV = 128
M_WIDTH = M_HEADS * M_DV
A_HEADS = 8
A_KV_HEADS = 2
A_GROUP = A_HEADS // A_KV_HEADS
A_DH = 64
A_WIDTH = A_HEADS * A_DH
WINDOW = 128
BLOCK = 128
N_BUCKETS = 32
MAX_DIST = 128
EPS = 1e-6
NEG = -1e30
LOG2E = 1.4426950408889634

LANES = 128
MXU_DIM = 256
VMEM_LIMIT_BYTES = 58 * 1024 * 1024

FF_CHUNK = MXU_DIM
N_FF_CHUNKS = D_FF // FF_CHUNK
TOKEN_TILE = 512
M_CHUNK = 256
META_PAD = 128
M_CHUNKS_PER_STEP = 8
A_BLOCKS_PER_STEP = 16
KVA_WIDTH = 2 * A_KV_HEADS * LANES

_NN = (((1,), (0,)), ((), ()))
_NT = (((1,), (1,)), ((), ()))


def _dot(a, b, dims=_NN):
    return lax.dot_general(a, b, dims, preferred_element_type=F32)


def _split_dot_lhs(x_f32, a_bf16):
    hi = x_f32.astype(BF16)
    lo = (x_f32 - hi.astype(F32)).astype(BF16)
    return _dot(hi, a_bf16) + _dot(lo, a_bf16)


def _rms(x, g):
    ms = jnp.mean(x * x, axis=-1, keepdims=True)
    return x * lax.rsqrt(ms + EPS) * g


def _sigmoid(x):
    return 1.0 / (1.0 + jnp.exp(-x))


def _log_sigmoid(x):
    return jnp.minimum(x, 0.0) - jnp.log1p(jnp.exp(-jnp.abs(x)))


def _row_streams(n_rows):
    n = 2 if n_rows % (2 * MXU_DIM) == 0 else 1
    return [slice(r * (n_rows // n), (r + 1) * (n_rows // n)) for r in range(n)]


def _const_spec(shape):
    nd = len(shape)
    return pl.BlockSpec(shape, lambda *_: (0,) * nd, pipeline_mode=pl.Buffered(1))


def _ffn_residual(x, nf_ref, wgu_ref, wd_ref, a_scr):
    xn = _rms(x, nf_ref[...]).astype(BF16)
    for j in range(N_FF_CHUNKS):
        sl = slice(j * FF_CHUNK, (j + 1) * FF_CHUNK)
        g = _dot(xn, wgu_ref[:, sl])
        u = _dot(xn, wgu_ref[:, D_FF + j * FF_CHUNK:D_FF + (j + 1) * FF_CHUNK])
        a_scr[:, sl] = (g * _sigmoid(g) * u).astype(BF16)
    y = _dot(a_scr[...], wd_ref[...])
    return x + 0.5 * y


def _group_sumsq(x, gsum_ref):
    sq = x * x
    hi = sq.astype(BF16)
    lo = (sq - hi.astype(F32)).astype(BF16)
    return _dot(jnp.concatenate([hi, lo], axis=1), gsum_ref[...])


def _lane_halves_tiled(x):
    swapped = pltpu.roll(x, LANES // 2, axis=1)
    low = lax.broadcasted_iota(jnp.int32, x.shape, 1) < LANES // 2
    return jnp.where(low, x, swapped), jnp.where(low, swapped, x)


def _stage_a_kernel(x_ref, nf_ref, wgu_ref, wd_ref, nm_ref, wqkv_ref, bg_ref, wqa_ref,
                    wkvg_ref, qg_ref, kg_ref, gsum_ref, triu_ref,
                    x1_ref, qkv_ref, qa_ref, kva_ref, *gate_outs_and_scratch, chunk):
    *gate_outs, a_scr = gate_outs_and_scratch
    streams = _row_streams(x_ref.shape[0])

    def gate_path(rows, x1):
        x1_ref[rows] = x1
        hn = _rms(x1, nm_ref[...]).astype(BF16)
        kvg = _dot(hn, wkvg_ref[...])
        gates_t = jnp.transpose(kvg[:, 2 * LANES:3 * LANES] + bg_ref[...])[:4 * M_HEADS]
        k = kvg[:, :LANES]
        kn = k * lax.rsqrt(_group_sumsq(k, gsum_ref) * (1.0 / A_DH) + EPS) * kg_ref[...]
        tiles = _lane_halves_tiled(kn) + _lane_halves_tiled(kvg[:, LANES:2 * LANES])
        for t, tiled in enumerate(tiles):
            kva_ref[rows, t * LANES:(t + 1) * LANES] = tiled.astype(BF16)
        return hn, gates_t

    def gate_scalars(rows, gates_t):
        if chunk is None:
            gate_outs[0][:, rows] = gates_t
            return
        prows_ref, pcols_ref, psc_ref = gate_outs
        for c in range((rows.stop - rows.start) // chunk):
            t0 = rows.start + c * chunk
            p_rows, p_cols, p_sc = _gate_prep_chunk(gates_t[:, c * chunk:(c + 1) * chunk], triu_ref)
            prows_ref[:, t0:t0 + chunk] = p_rows
            pcols_ref[t0:t0 + chunk, :] = p_cols
            psc_ref[t0 // chunk] = p_sc

    def project(rows, hn):
        qkv_ref[rows, :M_WIDTH] = (_dot(hn, wqkv_ref[:, :M_WIDTH]) * (M_DK ** -0.5)).astype(BF16)
        qkv_ref[rows, M_WIDTH:] = _dot(hn, wqkv_ref[:, M_WIDTH:]).astype(BF16)
        q_all = _dot(hn, wqa_ref[...])
        for t in range(A_WIDTH // LANES):
            sl = slice(t * LANES, (t + 1) * LANES)
            q = q_all[:, sl]
            ss = _group_sumsq(q, gsum_ref)
            qn = q * lax.rsqrt(ss * (1.0 / A_DH) + EPS) * qg_ref[:, sl] * (A_DH ** -0.5 * LOG2E)
            qa_ref[rows, sl] = qn.astype(BF16)

    x1 = [_ffn_residual(x_ref[rows], nf_ref, wgu_ref, wd_ref, a_scr.at[rows]) for rows in streams]
    pre = [gate_path(rows, x) for rows, x in zip(streams, x1)]
    for rows, (hn, gates_t) in zip(streams, pre):
        gate_scalars(rows, gates_t)
        project(rows, hn)


def _stage_a(x2d, w, tile, chunk=None):
    n = x2d.shape[0]
    assert n % tile == 0 and (chunk is None or tile % chunk == 0)
    row = lambda width: pl.BlockSpec((tile, width), lambda i: (i, 0))
    col = lambda height: pl.BlockSpec((height, tile), lambda i: (0, i))
    out_shape = [
        jax.ShapeDtypeStruct((n, D_MODEL), F32),
        jax.ShapeDtypeStruct((n, 3 * M_WIDTH), BF16),
        jax.ShapeDtypeStruct((n, A_WIDTH), BF16),
        jax.ShapeDtypeStruct((n, KVA_WIDTH), BF16),
    ]
    out_specs = [row(D_MODEL), row(3 * M_WIDTH), row(A_WIDTH), row(KVA_WIDTH)]
    if chunk is None:
        out_shape += [jax.ShapeDtypeStruct((4 * M_HEADS, n), F32)]
        out_specs += [col(4 * M_HEADS)]
    else:
        out_shape += [jax.ShapeDtypeStruct((2 * N_UNITS, n), F32),
                      jax.ShapeDtypeStruct((n, LANES), F32),
                      jax.ShapeDtypeStruct((n // chunk, 4 * N_UNITS, LANES), F32)]
        out_specs += [col(2 * N_UNITS), row(LANES),
                      pl.BlockSpec((tile // chunk, 4 * N_UNITS, LANES), lambda i: (i, 0, 0))]
    L = M_CHUNK if chunk is None else chunk
    triu = jnp.asarray(np.triu(np.ones((L, L), np.float32)), BF16)
    consts = (w['nf1'], w['wgu1'], w['wd1'], w['nm'], w['wqkv'], w['bg'], w['wqa'],
              w['wkvg'], w['qg'], w['kg'], w['gsum'], triu)
    return pl.pallas_call(
        functools.partial(_stage_a_kernel, chunk=chunk),
        grid=(n // tile,),
        in_specs=[row(D_MODEL)] + [_const_spec(c.shape) for c in consts],
        out_specs=out_specs,
        out_shape=out_shape,
        scratch_shapes=[pltpu.VMEM((tile, D_FF), BF16)],
        compiler_params=pltpu.CompilerParams(dimension_semantics=("parallel",),
                                             vmem_limit_bytes=VMEM_LIMIT_BYTES),
        name="stage_a",
    )(x2d, *consts)


N_UNITS = 2 * M_HEADS


def _running_max(x, L):
    row = lax.broadcasted_iota(jnp.int32, x.shape, 0)
    lane = lax.broadcasted_iota(jnp.int32, x.shape, 1)
    fwd = row < M_HEADS
    k = 1
    while k < L:
        prev = jnp.where(lane >= k, pltpu.roll(x, k, axis=1), NEG)
        nxt = jnp.where(lane < L - k, pltpu.roll(x, L - k, axis=1), NEG)
        x = jnp.maximum(x, jnp.where(fwd, prev, nxt))
        k *= 2
    return x


def _state_update(kt_f32, w_row, vaug, s_old_row, caug):
    wkt = (kt_f32 * w_row).astype(BF16)
    s_old = jnp.concatenate([s_old_row, s_old_row], axis=1)
    return s_old * caug + _dot(wkt, vaug)


def _gate_prep_chunk(g, triu_ref):
    H = M_HEADS
    L = g.shape[1]
    eye = (lax.broadcasted_iota(jnp.int32, (N_UNITS, LANES), 0)
           == lax.broadcasted_iota(jnp.int32, (N_UNITS, LANES), 1))
    lane_vec = lambda x: jnp.broadcast_to(
        jnp.sum(jnp.where(eye, x, 0.0), axis=0, keepdims=True), (N_UNITS, LANES))
    lff = _log_sigmoid(g[H:2 * H])
    lfb = _log_sigmoid(g[3 * H:4 * H])
    cs = _split_dot_lhs(jnp.concatenate([lff, lfb], axis=0), triu_ref[...])
    bf = cs[0:H]
    bb = cs[H:2 * H, L - 1:L] - cs[H:2 * H] + lfb
    b = jnp.concatenate([bf, bb], axis=0)
    r = jnp.concatenate([g[0:H], g[2 * H:3 * H]], axis=0) - b
    b_last = jnp.concatenate([jnp.broadcast_to(bf[:, L - 1:L], (H, LANES)),
                              jnp.broadcast_to(bb[:, 0:1], (H, LANES))], axis=0)
    a = jnp.concatenate([b_last] * (L // LANES), axis=1) + r
    m_loc = jnp.broadcast_to(jnp.max(a, axis=1, keepdims=True), (N_UNITS, LANES))
    packed = jnp.concatenate([_running_max(r, L) * LOG2E, b * LOG2E,
                              jnp.zeros((LANES - 2 * N_UNITS, L), F32)], axis=0)
    return (jnp.concatenate([r * LOG2E, a], axis=0), jnp.transpose(packed),
            jnp.concatenate([b_last, m_loc, lane_vec(b_last), lane_vec(m_loc)], axis=0))


def _mlstm_kernel(qf_ref, kf_ref, vf_ref, qb_ref, kb_ref, vb_ref, rf_ref, rb_ref, cf_ref, cb_ref,
                  sf_ref, sb_ref, mk_ref, mv_ref, mg_ref, trim_ref,
                  hf_ref, hb_ref, c_ref, m_ref, *, L):
    H = M_HEADS
    U = N_UNITS
    n_sub = qf_ref.shape[1] // L
    c = pl.program_id(1)

    @pl.when(c == 0)
    def _init():
        c_ref[...] = jnp.zeros_like(c_ref)
        g = mg_ref[...]
        valid = lax.broadcasted_iota(jnp.int32, g.shape, 1) < N_META
        lf = jnp.where(valid, _log_sigmoid(g), 0.0)
        li = jnp.where(valid, g, NEG)
        cs = _split_dot_lhs(lf, trim_ref[...])
        tot = cs[H:2 * H, META_PAD - 1:META_PAD]
        a = tot - cs[H:2 * H] + li[0:H]
        m_new = jnp.maximum(tot, jnp.max(a, axis=1, keepdims=True))
        w = jnp.exp(a - m_new)
        ones_m = jnp.ones((META_PAD, M_DV), BF16)
        for h in range(H):
            sl = slice(h * M_DK, (h + 1) * M_DK)
            vaug = jnp.concatenate([mv_ref[:, sl], ones_m], axis=1)
            kt = jnp.transpose(mk_ref[:, sl].astype(F32))
            c_ref[h] = _state_update(kt, w[h:h + 1], vaug, jnp.zeros((1, LANES), F32),
                                     jnp.zeros(c_ref.shape[1:], F32))
        m_rows = jnp.concatenate([jnp.broadcast_to(m_new, (H, LANES)),
                                  jnp.zeros((H, LANES), F32)], axis=0)
        eye = (lax.broadcasted_iota(jnp.int32, (U, LANES), 0)
               == lax.broadcasted_iota(jnp.int32, (U, LANES), 1))
        m_lanes = jnp.sum(jnp.where(eye, m_rows, 0.0), axis=0, keepdims=True)
        m_ref[...] = jnp.concatenate([m_rows, jnp.broadcast_to(m_lanes, (U, LANES))], axis=0)

    fwd_row = lax.broadcasted_iota(jnp.int32, (U, LANES), 0) < H
    fwd_lane = lax.broadcasted_iota(jnp.int32, (U, LANES), 1) % U < H
    tile_l = lambda x: jnp.concatenate([x] * (L // LANES), axis=1)
    fwd_row_l = tile_l(fwd_row)
    lane_i = lax.broadcasted_iota(jnp.int32, (L, LANES), 1)
    row_i = lax.broadcasted_iota(jnp.int32, (L, L), 0)
    col_i = lax.broadcasted_iota(jnp.int32, (L, L), 1)
    ones = jnp.ones((L, M_DV), BF16)
    m0r = m_ref[0:U]
    m0l = m_ref[U:2 * U]
    def token_rows(i):
        jf, jb = i, n_sub - 1 - i
        return jf, jb, slice(jf * L, (jf + 1) * L), slice(jb * L, (jb + 1) * L)

    def operands(i, u):
        _, _, tf, tb = token_rows(i)
        sl = slice((u % H) * M_DK, (u % H + 1) * M_DK)
        if u < H:
            return qf_ref[0, tf, sl], kf_ref[0, tf, sl], vf_ref[0, tf, sl], row_i >= col_i
        return qb_ref[0, tb, sl], kb_ref[0, tb, sl], vb_ref[0, tb, sl], row_i <= col_i

    def scores(i, u):
        q, k, _, _ = operands(i, u)
        return _dot(q, k, _NT), jnp.transpose(k.astype(F32))

    order = [(i, u) for i in range(n_sub) for u in range(U)]
    pending = scores(*order[0])
    for i in range(n_sub):
        jf, jb, tf, tb = token_rows(i)
        pick_r = lambda n: jnp.where(fwd_row, sf_ref[jf, n * U:(n + 1) * U],
                                     sb_ref[jb, n * U:(n + 1) * U])
        pick_l = lambda n: jnp.where(fwd_lane, sf_ref[jf, n * U:(n + 1) * U],
                                     sb_ref[jb, n * U:(n + 1) * U])
        m_new_r = jnp.maximum(pick_r(0) + m0r, pick_r(1))
        m_new_l = jnp.maximum(pick_l(2) + m0l, pick_l(3))
        s_old = jnp.exp(pick_r(0) + m0r - m_new_r)
        r2 = jnp.where(fwd_row_l, rf_ref[0:U, tf], rb_ref[0:U, tb])
        w = jnp.exp(jnp.where(fwd_row_l, rf_ref[U:2 * U, tf], rb_ref[U:2 * U, tb])
                    - tile_l(m_new_r))
        m2 = m0r * LOG2E
        colsx = jnp.where(lane_i % U < H, cf_ref[tf], cb_ref[tb])
        g2 = jnp.maximum(colsx, m0l[0:1] * LOG2E)
        e_cols = jnp.exp2(-(pltpu.roll(colsx, LANES - U, axis=1) + g2))
        den = jnp.zeros((L, LANES), F32)
        for u in range(U):
            s, kt = pending
            nxt = i * U + u + 1
            pending = scores(*order[nxt]) if nxt < len(order) else None
            q, _, v, mask = operands(i, u)
            sl = slice((u % H) * M_DK, (u % H + 1) * M_DK)
            out, ts = (hf_ref, tf) if u < H else (hb_ref, tb)
            caug = c_ref[u]
            vaug = jnp.concatenate([v, ones], axis=1)
            g_col = g2[:, u:u + 1]
            x_intra = jnp.where(mask, s * jnp.exp2(r2[u:u + 1] - g_col), 0.0)
            x_inter = q.astype(F32) * jnp.exp2(m2[u:u + 1] - g_col)
            x = jnp.concatenate([x_intra, x_inter], axis=1).astype(BF16)
            tot = _dot(x, jnp.concatenate([vaug, caug.astype(BF16)], axis=0))
            out[0, ts, sl] = tot[:, :M_DV]
            den = jnp.where(lane_i == u, tot[:, M_DV:], den)
            c_ref[u] = _state_update(kt, w[u:u + 1], vaug, s_old[u:u + 1], caug)
        inv = 1.0 / jnp.maximum(jnp.abs(den), e_cols)
        for u in range(U):
            sl = slice((u % H) * M_DV, (u % H + 1) * M_DV)
            out, ts = (hf_ref, tf) if u < H else (hb_ref, tb)
            out[0, ts, sl] = out[0, ts, sl] * inv[:, u:u + 1]
        m0r, m0l = m_new_r, m_new_l
    m_ref[...] = jnp.concatenate([m0r, m0l], axis=0)


def _mlstm(qkv, prep, meta_qkv, meta_gates_t, chunk):
    B, S, _ = qkv.shape
    L = chunk
    assert S % L == 0 and L % LANES == 0
    NC = S // L
    rows, cols, sc = prep
    trim = jnp.asarray(np.triu(np.ones((META_PAD, META_PAD), np.float32)), BF16)
    G = 4 * M_HEADS
    J = math.gcd(M_CHUNKS_PER_STEP, NC)
    T = J * L
    NS = NC // J
    fwd = lambda j: pl.BlockSpec((1, T, M_WIDTH), lambda b, c: (b, c, j))
    bwd = lambda j: pl.BlockSpec((1, T, M_WIDTH), lambda b, c: (b, NS - 1 - c, j))
    in_specs = [
        fwd(0), fwd(1), fwd(2), bwd(0), bwd(1), bwd(2),
        pl.BlockSpec((2 * N_UNITS, T), lambda b, c: (0, b * NS + c)),
        pl.BlockSpec((2 * N_UNITS, T), lambda b, c: (0, b * NS + NS - 1 - c)),
        pl.BlockSpec((T, LANES), lambda b, c: (b * NS + c, 0)),
        pl.BlockSpec((T, LANES), lambda b, c: (b * NS + NS - 1 - c, 0)),
        pl.BlockSpec((J, 4 * N_UNITS, LANES), lambda b, c: (b * NS + c, 0, 0)),
        pl.BlockSpec((J, 4 * N_UNITS, LANES), lambda b, c: (b * NS + NS - 1 - c, 0, 0)),
        pl.BlockSpec((META_PAD, M_WIDTH), lambda b, c: (0, 1)),
        pl.BlockSpec((META_PAD, M_WIDTH), lambda b, c: (0, 2)),
        pl.BlockSpec((G, META_PAD), lambda b, c: (0, 0)),
        pl.BlockSpec((META_PAD, META_PAD), lambda b, c: (0, 0)),
    ]
    out_specs = (pl.BlockSpec((1, T, M_WIDTH), lambda b, c: (b, c, 0)),
                 pl.BlockSpec((1, T, M_WIDTH), lambda b, c: (b, NS - 1 - c, 0)))
    return pl.pallas_call(
        functools.partial(_mlstm_kernel, L=L),
        grid=(B, NS),
        in_specs=in_specs,
        out_specs=out_specs,
        out_shape=(jax.ShapeDtypeStruct((B, S, M_WIDTH), F32),
                   jax.ShapeDtypeStruct((B, S, M_WIDTH), F32)),
        scratch_shapes=[pltpu.VMEM((N_UNITS, M_DK, 2 * M_DV), F32),
                        pltpu.VMEM((2 * N_UNITS, LANES), F32)],
        compiler_params=pltpu.CompilerParams(dimension_semantics=("parallel", "arbitrary"),
                                             vmem_limit_bytes=VMEM_LIMIT_BYTES),
        name="mlstm",
    )(qkv, qkv, qkv, qkv, qkv, qkv, rows, rows, cols, cols, sc, sc,
      meta_qkv, meta_qkv, meta_gates_t, trim)


def _attn_kernel(q_ref, kp_ref, kc_ref, kn_ref, km_ref, bias_ref, o_ref, *, n_blocks):
    GW = A_GROUP * A_DH
    HB = BLOCK // 2
    n_q = q_ref.shape[1] // BLOCK
    lane = lax.broadcasted_iota(jnp.int32, (HB, GW), 1)
    head_masks = [((lane >= g * A_DH) & (lane < (g + 1) * A_DH)).astype(F32)
                  for g in range(A_GROUP)]
    wide = lambda x: jnp.concatenate([x, x], axis=1)
    zeros = jnp.zeros((HB - N_META, LANES), BF16)
    for i in range(n_q):
        blk = pl.program_id(1) * n_q + i
        var = jnp.where(blk == 0, 0, jnp.where(blk == n_blocks - 1, 2, 1))
        cur = kc_ref[0, i * BLOCK:(i + 1) * BLOCK]
        prev = kp_ref[0] if i == 0 else kc_ref[0, (i - 1) * BLOCK:i * BLOCK]
        nxt = kn_ref[0] if i == n_q - 1 else kc_ref[0, (i + 1) * BLOCK:(i + 2) * BLOCK]
        for j in range(A_KV_HEADS):
            for kind in range(2):
                sl = slice((kind * A_KV_HEADS + j) * LANES, (kind * A_KV_HEADS + j + 1) * LANES)
                extra = jnp.concatenate([km_ref[:, sl], zeros], axis=0)
                first = wide(jnp.concatenate([prev[:, sl], cur[:, sl], nxt[:HB, sl], extra], axis=0))
                second = wide(jnp.concatenate([extra, prev[HB:, sl], cur[:, sl], nxt[:, sl]], axis=0))
                if kind == 0:
                    keys = (first, second)
                else:
                    vals = (first, second)
            for h in range(2):
                rows = slice(i * BLOCK + h * HB, i * BLOCK + (h + 1) * HB)
                qg = q_ref[0, rows, j * GW:(j + 1) * GW].astype(F32)
                qs = jnp.concatenate([(qg * m).astype(BF16) for m in head_masks], axis=0)
                s = _dot(qs, keys[h], _NT) + bias_ref[var, j, h]
                e = jnp.exp2(s - jnp.max(s, axis=1, keepdims=True))
                den = jnp.sum(e, axis=1, keepdims=True)
                o = _dot(e.astype(BF16), vals[h]) / den
                og = o[(A_GROUP - 1) * HB:]
                for g in range(A_GROUP - 2, -1, -1):
                    og = jnp.where(lane < (g + 1) * A_DH, o[g * HB:(g + 1) * HB], og)
                o_ref[0, rows, j * GW:(j + 1) * GW] = og.astype(BF16)


def _attention(qa, kva, meta_kva, bias):
    B, S, _ = qa.shape
    NB = S // BLOCK
    assert S % BLOCK == 0 and NB >= 2
    W = KVA_WIDTH
    Q = math.gcd(A_BLOCKS_PER_STEP, NB)
    in_specs = [
        pl.BlockSpec((1, Q * BLOCK, A_WIDTH), lambda b, n: (b, n, 0)),
        pl.BlockSpec((1, BLOCK, W), lambda b, n: (b, jnp.maximum(n * Q - 1, 0), 0)),
        pl.BlockSpec((1, Q * BLOCK, W), lambda b, n: (b, n, 0)),
        pl.BlockSpec((1, BLOCK, W), lambda b, n: (b, jnp.minimum((n + 1) * Q, NB - 1), 0)),
        pl.BlockSpec((N_META, W), lambda b, n: (0, 0)),
        _const_spec(bias.shape),
    ]
    return pl.pallas_call(
        functools.partial(_attn_kernel, n_blocks=NB),
        grid=(B, NB // Q),
        in_specs=in_specs,
        out_specs=pl.BlockSpec((1, Q * BLOCK, A_WIDTH), lambda b, n: (b, n, 0)),
        out_shape=jax.ShapeDtypeStruct((B, S, A_WIDTH), BF16),
        compiler_params=pltpu.CompilerParams(dimension_semantics=("parallel", "parallel"),
                                             vmem_limit_bytes=VMEM_LIMIT_BYTES),
        name="window_attn",
    )(qa, kva, kva, kva, meta_kva, bias)


def _t5_bucket_np(rel):
    nb = N_BUCKETS // 2
    max_exact = nb // 2
    ret = np.where(rel > 0, nb, 0)
    n = np.abs(rel)
    nf = np.maximum(n, 1).astype(np.float32)
    large = max_exact + (np.log(nf / max_exact) / math.log(MAX_DIST / max_exact)
                         * (nb - max_exact)).astype(np.int32)
    large = np.minimum(large, nb - 1)
    return ret + np.where(n < max_exact, n, large)


def _attention_bias(rel_bias, sink_logits, n_blocks):
    HB = BLOCK // 2
    t = np.arange(BLOCK)
    s_off = np.arange(3 * BLOCK) - BLOCK
    rel_band = s_off[None, :] - t[:, None]
    band_ok = np.abs(rel_band) <= WINDOW
    band_bucket = _t5_bucket_np(rel_band)
    q_pos = N_META + np.arange(n_blocks * BLOCK).reshape(n_blocks, BLOCK)
    meta_bucket = _t5_bucket_np(np.arange(N_META)[None, None, :] - q_pos[..., None])
    assert (meta_bucket[1:] == meta_bucket[1:2]).all()
    table = rel_bias.astype(F32)
    lookup = lambda bucket: jnp.einsum(
        '...k,kh->...h', jnp.asarray(np.eye(N_BUCKETS, dtype=np.float32)[bucket]), table,
        precision=lax.Precision.HIGHEST)
    neg = jnp.full((BLOCK, 3 * BLOCK, A_HEADS), NEG, F32)
    variants = []
    for var in range(3):
        ok = band_ok.copy()
        if var == 0:
            ok[:, :BLOCK] = False
        if var == 2:
            ok[:, 2 * BLOCK:] = False
        band = jnp.where(jnp.asarray(ok)[:, :, None], lookup(band_bucket), neg)
        meta = lookup(meta_bucket[0 if var == 0 else 1])
        sink = jnp.broadcast_to(sink_logits.astype(F32)[None, None, :], (BLOCK, 1, A_HEADS))
        padc = jnp.full((BLOCK, HB - N_META - 1, A_HEADS), NEG, F32)
        extra = jnp.concatenate([meta, sink, padc], axis=1)
        halves = [jnp.concatenate([band[:HB, :3 * BLOCK - HB], extra[:HB]], axis=1),
                  jnp.concatenate([extra[HB:], band[HB:, HB:]], axis=1)]
        full = jnp.transpose(jnp.stack(halves), (3, 0, 1, 2))
        full = full.reshape(A_KV_HEADS, A_GROUP, 2, HB, 3 * BLOCK)
        full = jnp.transpose(full, (0, 2, 1, 3, 4)).reshape(A_KV_HEADS, 2, A_GROUP * HB, 3 * BLOCK)
        variants.append(full)
    return jnp.stack(variants) * LOG2E


def _stage_d_kernel(x1_ref, hf_ref, hb_ref, at_ref, nm_ref, wo_ref, mg_ref, wga_ref, wgb_ref,
                    wupm_ref, wupa_ref, wout_ref, nf_ref, wgu_ref, wd_ref, out_ref, a_scr, hm_scr):
    streams = _row_streams(x1_ref.shape[0])

    def gates(rows):
        y_a = _dot(at_ref[rows], wupa_ref[...])
        x1 = x1_ref[rows]
        hn = _rms(x1, nm_ref[...]).astype(BF16)
        o = _dot(hn, wo_ref[...])
        for h in range(M_HEADS):
            sl = slice(h * M_DV, (h + 1) * M_DV)
            hs = hf_ref[rows, sl] + hb_ref[rows, sl]
            hm_scr[rows, sl] = (_rms(hs, mg_ref[:, sl]) * _sigmoid(o[:, sl])).astype(BF16)
        return x1, hn, y_a

    def mix(rows, x1, hn, y_a):
        mixed = _sigmoid(_dot(hn, wga_ref[...])) * _dot(hm_scr[rows], wupm_ref[...])
        mixed = mixed + _sigmoid(_dot(hn, wgb_ref[...])) * y_a
        return x1 + _dot(mixed.astype(BF16), wout_ref[...])

    pre = [gates(rows) for rows in streams]
    x2 = [mix(rows, *p) for rows, p in zip(streams, pre)]
    for rows, x in zip(streams, x2):
        out_ref[rows] = _ffn_residual(x, nf_ref, wgu_ref, wd_ref, a_scr.at[rows])


def _stage_d(x1, hf, hb, at, w, tile):
    n = x1.shape[0]
    row = lambda width: pl.BlockSpec((tile, width), lambda i: (i, 0))
    consts = (w['nm'], w['wo'], w['mg'], w['wga'], w['wgb'], w['wupm'], w['wupa'], w['wout'],
              w['nf2'], w['wgu2'], w['wd2'])
    return pl.pallas_call(
        _stage_d_kernel,
        grid=(n // tile,),
        in_specs=[row(D_MODEL), row(M_WIDTH), row(M_WIDTH), row(A_WIDTH)]
                 + [_const_spec(c.shape) for c in consts],
        out_specs=row(D_MODEL),
        out_shape=jax.ShapeDtypeStruct((n, D_MODEL), F32),
        scratch_shapes=[pltpu.VMEM((tile, D_FF), BF16), pltpu.VMEM((tile, M_WIDTH), BF16)],
        compiler_params=pltpu.CompilerParams(dimension_semantics=("parallel",),
                                             vmem_limit_bytes=VMEM_LIMIT_BYTES),
        name="stage_d",
    )(x1, hf, hb, at, *consts)


def _prep_weights(norm_ffn1, w_ffn1_gu, w_ffn1_down, norm_mix, w_in, b_gates, m_out_gain,
                  q_norm_gain, k_norm_gain, w_up_m, w_up_a, w_out, norm_ffn2, w_ffn2_gu,
                  w_ffn2_down):
    row = lambda v: v.reshape(1, -1).astype(F32)
    o_qm, o_om = 0, 3 * M_WIDTH
    o_gm = o_om + M_WIDTH
    o_qa = o_gm + 4 * M_HEADS
    o_ka = o_qa + A_WIDTH
    o_va = o_ka + A_KV_HEADS * A_DH
    o_ga = o_va + A_KV_HEADS * A_DH
    o_gb = o_ga + D_MODEL
    cols = lambda a, b: w_in[:, a:b]
    n_g = 4 * M_HEADS
    wkvg = jnp.pad(jnp.concatenate([cols(o_ka, o_ga), cols(o_gm, o_qa)], axis=1),
                   ((0, 0), (0, MXU_DIM - n_g)))
    gsum = np.kron(np.eye(LANES // A_DH, dtype=np.float32), np.ones((A_DH, A_DH), np.float32))
    return {
        'nf1': row(norm_ffn1), 'wgu1': w_ffn1_gu.astype(BF16), 'wd1': w_ffn1_down.astype(BF16),
        'nm': row(norm_mix),
        'wqkv': cols(o_qm, o_om).astype(BF16),
        'wo': cols(o_om, o_gm).astype(BF16),
        'bg': jnp.pad(row(b_gates), ((0, 0), (0, LANES - n_g))),
        'wqa': cols(o_qa, o_ka).astype(BF16),
        'wkvg': wkvg.astype(BF16),
        'qg': jnp.tile(row(q_norm_gain), (1, A_HEADS)),
        'kg': jnp.tile(row(k_norm_gain), (1, LANES // A_DH)),
        'gsum': jnp.asarray(np.concatenate([gsum, gsum], axis=0), BF16),
        'wga': cols(o_ga, o_gb).astype(BF16), 'wgb': cols(o_gb, o_gb + D_MODEL).astype(BF16),
        'mg': row(m_out_gain),
        'wupm': w_up_m.astype(BF16), 'wupa': w_up_a.astype(BF16), 'wout': w_out.astype(BF16),
        'nf2': row(norm_ffn2), 'wgu2': w_ffn2_gu.astype(BF16), 'wd2': w_ffn2_down.astype(BF16),
    }


def _encode(x, w, meta, bias_fn, tile, chunk):
    B, S, _ = x.shape
    meta_qkv, meta_gates, meta_kva = meta
    x1, qkv, qa, kva, *prep = _stage_a(x.reshape(B * S, D_MODEL), w, tile, chunk)
    hf, hb = _mlstm(qkv.reshape(B, S, -1), prep, meta_qkv, meta_gates, chunk)
    at = _attention(qa.reshape(B, S, -1), kva.reshape(B, S, -1), meta_kva, bias_fn(S // BLOCK))
    y = _stage_d(x1, hf.reshape(B * S, -1), hb.reshape(B * S, -1), at.reshape(B * S, -1), w, tile)
    return y.reshape(B, S, D_MODEL)


def _layer(x_groups, meta_tokens, rel_bias, norm_ffn1, w_ffn1_gu, w_ffn1_down, norm_mix, w_in,
           b_gates, m_out_gain, q_norm_gain, k_norm_gain, sink_logits, w_up_m, w_up_a, w_out,
           norm_ffn2, w_ffn2_gu, w_ffn2_down, tile=TOKEN_TILE, chunk=M_CHUNK):
    assert norm_ffn1.shape[0] == 1, "single layer"
    w = _prep_weights(norm_ffn1[0], w_ffn1_gu[0], w_ffn1_down[0], norm_mix[0], w_in[0], b_gates[0],
                      m_out_gain[0], q_norm_gain[0], k_norm_gain[0], w_up_m[0], w_up_a[0],
                      w_out[0], norm_ffn2[0], w_ffn2_gu[0], w_ffn2_down[0])
    meta_x = jnp.pad(meta_tokens.astype(F32), ((0, META_PAD - N_META), (0, 0)))
    _, m_qkv, _, m_kva, m_gates = _stage_a(meta_x, w, META_PAD)
    meta = (m_qkv, m_gates, m_kva[:N_META])
    bias_fn = functools.lru_cache(None)(
        lambda nb: _attention_bias(rel_bias, sink_logits[0], nb))
    return tuple(_encode(x, w, meta, bias_fn, tile, chunk) for x in x_groups)


def kernel(x_prompt, x_sample, meta_tokens, rel_bias, norm_ffn1, w_ffn1_gu, w_ffn1_down, norm_mix,
           w_in, b_gates, m_out_gain, q_norm_gain, k_norm_gain, sink_logits, w_up_m, w_up_a, w_out,
           norm_ffn2, w_ffn2_gu, w_ffn2_down):
    return _layer((x_prompt, x_sample), meta_tokens, rel_bias, norm_ffn1, w_ffn1_gu, w_ffn1_down,
                  norm_mix, w_in, b_gates, m_out_gain, q_norm_gain, k_norm_gain, sink_logits,
                  w_up_m, w_up_a, w_out, norm_ffn2, w_ffn2_gu, w_ffn2_down)
```

```python
import functools
import math

import numpy as np
import jax
import jax.numpy as jnp
from jax import lax
from jax.experimental import pallas as pl
from jax.experimental.pallas import tpu as pltpu

F32 = jnp.float32
BF16 = jnp.bfloat16

D_MODEL = 1024
D_FF = 2816
N_META = 16
M_HEADS = 4
M_DK = 128
M_DV = 128
M_WIDTH = M_HEADS * M_DV
A_HEADS = 8
A_KV_HEADS = 2
A_GROUP = A_HEADS // A_KV_HEADS
A_DH = 64
A_WIDTH = A_HEADS * A_DH
WINDOW = 128
BLOCK = 128
N_BUCKETS = 32
MAX_DIST = 128
EPS = 1e-6
NEG = -1e30
LOG2E = 1.4426950408889634

LANES = 128
MXU_DIM = 256
VMEM_LIMIT_BYTES = 58 * 1024 * 1024

FF_CHUNK = MXU_DIM
N_FF_CHUNKS = D_FF // FF_CHUNK
TOKEN_TILE = 512
M_CHUNK = 256
META_PAD = 128
M_CHUNKS_PER_STEP = 8
A_BLOCKS_PER_STEP = 16
KVA_WIDTH = 2 * A_KV_HEADS * LANES

_NN = (((1,), (0,)), ((), ()))
_NT = (((1,), (1,)), ((), ()))


def _dot(a, b, dims=_NN):
    return lax.dot_general(a, b, dims, preferred_element_type=F32)


def _split_dot_lhs(x_f32, a_bf16):
    hi = x_f32.astype(BF16)
    lo = (x_f32 - hi.astype(F32)).astype(BF16)
    return _dot(hi, a_bf16) + _dot(lo, a_bf16)


def _rms(x, g):
    ms = jnp.mean(x * x, axis=-1, keepdims=True)
    return x * lax.rsqrt(ms + EPS) * g


def _sigmoid(x):
    return 1.0 / (1.0 + jnp.exp(-x))


def _log_sigmoid(x):
    return jnp.minimum(x, 0.0) - jnp.log1p(jnp.exp(-jnp.abs(x)))


def _row_streams(n_rows):
    n = 2 if n_rows % (2 * MXU_DIM) == 0 else 1
    return [slice(r * (n_rows // n), (r + 1) * (n_rows // n)) for r in range(n)]


def _const_spec(shape):
    nd = len(shape)
    return pl.BlockSpec(shape, lambda *_: (0,) * nd, pipeline_mode=pl.Buffered(1))


def _ffn_residual(x, nf_ref, wgu_ref, wd_ref, a_scr, xn=None):
    if xn is None:
        xn = _rms(x, nf_ref[...]).astype(BF16)
    for j in range(N_FF_CHUNKS):
        sl = slice(j * FF_CHUNK, (j + 1) * FF_CHUNK)
        g = _dot(xn, wgu_ref[:, sl])
        u = _dot(xn, wgu_ref[:, D_FF + j * FF_CHUNK:D_FF + (j + 1) * FF_CHUNK])
        a_scr[:, sl] = (g * _sigmoid(g) * u).astype(BF16)
    y = _dot(a_scr[...], wd_ref[...])
    return x + 0.5 * y


def _group_sumsq(x, gsum_ref):
    sq = x * x
    hi = sq.astype(BF16)
    lo = (sq - hi.astype(F32)).astype(BF16)
    return _dot(jnp.concatenate([hi, lo], axis=1), gsum_ref[...])


def _lane_halves_tiled(x):
    swapped = pltpu.roll(x, LANES // 2, axis=1)
    low = lax.broadcasted_iota(jnp.int32, x.shape, 1) < LANES // 2
    return jnp.where(low, x, swapped), jnp.where(low, swapped, x)


def _stage_a_kernel(x_ref, xnext_ref, nf_ref, wgu_ref, wd_ref, nm_ref, wqkv_ref, bg_ref, wqa_ref,
                    wkvg_ref, qg_ref, kg_ref, gsum_ref, triu_ref,
                    x1_ref, qkv_ref, qa_ref, kva_ref, *gate_outs_and_scratch, chunk):
    *gate_outs, a_scr, xn_scr = gate_outs_and_scratch
    streams = _row_streams(x_ref.shape[0])

    @pl.when(pl.program_id(0) == 0)
    def _first_norm():
        xn_scr[...] = _rms(x_ref[streams[0]], nf_ref[...]).astype(BF16)

    def gate_path(rows, x1):
        x1_ref[rows] = x1
        hn = _rms(x1, nm_ref[...]).astype(BF16)
        kvg = _dot(hn, wkvg_ref[...])
        gates_t = jnp.transpose(kvg[:, 2 * LANES:3 * LANES] + bg_ref[...])[:4 * M_HEADS]
        k = kvg[:, :LANES]
        kn = k * lax.rsqrt(_group_sumsq(k, gsum_ref) * (1.0 / A_DH) + EPS) * kg_ref[...]
        tiles = _lane_halves_tiled(kn) + _lane_halves_tiled(kvg[:, LANES:2 * LANES])
        for t, tiled in enumerate(tiles):
            kva_ref[rows, t * LANES:(t + 1) * LANES] = tiled.astype(BF16)
        return hn, gates_t

    def gate_scalars(rows, gates_t):
        if chunk is None:
            gate_outs[0][:, rows] = gates_t
            return
        prows_ref, pcols_ref, psc_ref = gate_outs
        for c in range((rows.stop - rows.start) // chunk):
            t0 = rows.start + c * chunk
            p_rows, p_cols, p_sc = _gate_prep_chunk(gates_t[:, c * chunk:(c + 1) * chunk], triu_ref)
            prows_ref[:, t0:t0 + chunk] = p_rows
            pcols_ref[t0:t0 + chunk, :] = p_cols
            psc_ref[t0 // chunk] = p_sc

    def project(rows, hn):
        qkv_ref[rows, :M_WIDTH] = (_dot(hn, wqkv_ref[:, :M_WIDTH]) * (M_DK ** -0.5)).astype(BF16)
        qkv_ref[rows, M_WIDTH:] = _dot(hn, wqkv_ref[:, M_WIDTH:]).astype(BF16)
        q_all = _dot(hn, wqa_ref[...])
        for t in range(A_WIDTH // LANES):
            sl = slice(t * LANES, (t + 1) * LANES)
            q = q_all[:, sl]
            ss = _group_sumsq(q, gsum_ref)
            qn = q * lax.rsqrt(ss * (1.0 / A_DH) + EPS) * qg_ref[:, sl] * (A_DH ** -0.5 * LOG2E)
            qa_ref[rows, sl] = qn.astype(BF16)

    x1 = [_ffn_residual(x_ref[rows], nf_ref, wgu_ref, wd_ref, a_scr.at[rows],
                        xn=xn_scr[...] if n == 0 else None) for n, rows in enumerate(streams)]
    xn_scr[...] = _rms(xnext_ref[...], nf_ref[...]).astype(BF16)
    pre = [gate_path(rows, x) for rows, x in zip(streams, x1)]
    for rows, (hn, gates_t) in zip(streams, pre):
        gate_scalars(rows, gates_t)
        project(rows, hn)


def _stage_a(x2d, w, tile, chunk=None):
    n = x2d.shape[0]
    assert n % tile == 0 and (chunk is None or tile % chunk == 0)
    row = lambda width: pl.BlockSpec((tile, width), lambda i: (i, 0))
    col = lambda height: pl.BlockSpec((height, tile), lambda i: (0, i))
    out_shape = [
        jax.ShapeDtypeStruct((n, D_MODEL), F32),
        jax.ShapeDtypeStruct((n, 3 * M_WIDTH), BF16),
        jax.ShapeDtypeStruct((n, A_WIDTH), BF16),
        jax.ShapeDtypeStruct((n, KVA_WIDTH), BF16),
    ]
    out_specs = [row(D_MODEL), row(3 * M_WIDTH), row(A_WIDTH), row(KVA_WIDTH)]
    if chunk is None:
        out_shape += [jax.ShapeDtypeStruct((4 * M_HEADS, n), F32)]
        out_specs += [col(4 * M_HEADS)]
    else:
        out_shape += [jax.ShapeDtypeStruct((2 * N_UNITS, n), F32),
                      jax.ShapeDtypeStruct((n, LANES), F32),
                      jax.ShapeDtypeStruct((n // chunk, 4 * N_UNITS, LANES), F32)]
        out_specs += [col(2 * N_UNITS), row(LANES),
                      pl.BlockSpec((tile // chunk, 4 * N_UNITS, LANES), lambda i: (i, 0, 0))]
    head = _row_streams(tile)[0].stop
    next_head = pl.BlockSpec((head, D_MODEL),
                             lambda i: (jnp.minimum((i + 1) * (tile // head), n // head - 1), 0))
    L = M_CHUNK if chunk is None else chunk
    triu = jnp.asarray(np.triu(np.ones((L, L), np.float32)), BF16)
    consts = (w['nf1'], w['wgu1'], w['wd1'], w['nm'], w['wqkv'], w['bg'], w['wqa'],
              w['wkvg'], w['qg'], w['kg'], w['gsum'], triu)
    return pl.pallas_call(
        functools.partial(_stage_a_kernel, chunk=chunk),
        grid=(n // tile,),
        in_specs=[row(D_MODEL), next_head] + [_const_spec(c.shape) for c in consts],
        out_specs=out_specs,
        out_shape=out_shape,
        scratch_shapes=[pltpu.VMEM((tile, D_FF), BF16), pltpu.VMEM((head, D_MODEL), BF16)],
        compiler_params=pltpu.CompilerParams(dimension_semantics=("arbitrary",),
                                             vmem_limit_bytes=VMEM_LIMIT_BYTES),
        name="stage_a",
    )(x2d, x2d, *consts)


N_UNITS = 2 * M_HEADS


def _running_max(x, L):
    row = lax.broadcasted_iota(jnp.int32, x.shape, 0)
    lane = lax.broadcasted_iota(jnp.int32, x.shape, 1)
    fwd = row < M_HEADS
    k = 1
    while k < L:
        prev = jnp.where(lane >= k, pltpu.roll(x, k, axis=1), NEG)
        nxt = jnp.where(lane < L - k, pltpu.roll(x, L - k, axis=1), NEG)
        x = jnp.maximum(x, jnp.where(fwd, prev, nxt))
        k *= 2
    return x


def _state_update(kt_f32, w_row, vaug, s_old_row, caug):
    wkt = (kt_f32 * w_row).astype(BF16)
    s_old = jnp.concatenate([s_old_row, s_old_row], axis=1)
    return s_old * caug + _dot(wkt, vaug)


def _gate_prep_chunk(g, triu_ref):
    H = M_HEADS
    L = g.shape[1]
    eye = (lax.broadcasted_iota(jnp.int32, (N_UNITS, LANES), 0)
           == lax.broadcasted_iota(jnp.int32, (N_UNITS, LANES), 1))
    lane_vec = lambda x: jnp.broadcast_to(
        jnp.sum(jnp.where(eye, x, 0.0), axis=0, keepdims=True), (N_UNITS, LANES))
    lff = _log_sigmoid(g[H:2 * H])
    lfb = _log_sigmoid(g[3 * H:4 * H])
    cs = _split_dot_lhs(jnp.concatenate([lff, lfb], axis=0), triu_ref[...])
    bf = cs[0:H]
    bb = cs[H:2 * H, L - 1:L] - cs[H:2 * H] + lfb
    b = jnp.concatenate([bf, bb], axis=0)
    r = jnp.concatenate([g[0:H], g[2 * H:3 * H]], axis=0) - b
    b_last = jnp.concatenate([jnp.broadcast_to(bf[:, L - 1:L], (H, LANES)),
                              jnp.broadcast_to(bb[:, 0:1], (H, LANES))], axis=0)
    a = jnp.concatenate([b_last] * (L // LANES), axis=1) + r
    m_loc = jnp.broadcast_to(jnp.max(a, axis=1, keepdims=True), (N_UNITS, LANES))
    packed = jnp.concatenate([_running_max(r, L) * LOG2E, b * LOG2E,
                              jnp.zeros((LANES - 2 * N_UNITS, L), F32)], axis=0)
    return (jnp.concatenate([r * LOG2E, a], axis=0), jnp.transpose(packed),
            jnp.concatenate([b_last, m_loc, lane_vec(b_last), lane_vec(m_loc)], axis=0))


def _mlstm_kernel(qf_ref, kf_ref, vf_ref, qb_ref, kb_ref, vb_ref, rf_ref, rb_ref, cf_ref, cb_ref,
                  sf_ref, sb_ref, mk_ref, mv_ref, mg_ref, trim_ref,
                  hf_ref, hb_ref, c_ref, m_ref, *, L):
    H = M_HEADS
    U = N_UNITS
    n_sub = qf_ref.shape[1] // L
    c = pl.program_id(1)

    @pl.when(c == 0)
    def _init():
        c_ref[...] = jnp.zeros_like(c_ref)
        g = mg_ref[...]
        valid = lax.broadcasted_iota(jnp.int32, g.shape, 1) < N_META
        lf = jnp.where(valid, _log_sigmoid(g), 0.0)
        li = jnp.where(valid, g, NEG)
        cs = _split_dot_lhs(lf, trim_ref[...])
        tot = cs[H:2 * H, META_PAD - 1:META_PAD]
        a = tot - cs[H:2 * H] + li[0:H]
        m_new = jnp.maximum(tot, jnp.max(a, axis=1, keepdims=True))
        w = jnp.exp(a - m_new)
        ones_m = jnp.ones((META_PAD, M_DV), BF16)
        for h in range(H):
            sl = slice(h * M_DK, (h + 1) * M_DK)
            vaug = jnp.concatenate([mv_ref[:, sl], ones_m], axis=1)
            kt = jnp.transpose(mk_ref[:, sl].astype(F32))
            c_ref[h] = _state_update(kt, w[h:h + 1], vaug, jnp.zeros((1, LANES), F32),
                                     jnp.zeros(c_ref.shape[1:], F32))
        m_rows = jnp.concatenate([jnp.broadcast_to(m_new, (H, LANES)),
                                  jnp.zeros((H, LANES), F32)], axis=0)
        eye = (lax.broadcasted_iota(jnp.int32, (U, LANES), 0)
               == lax.broadcasted_iota(jnp.int32, (U, LANES), 1))
        m_lanes = jnp.sum(jnp.where(eye, m_rows, 0.0), axis=0, keepdims=True)
        m_ref[...] = jnp.concatenate([m_rows, jnp.broadcast_to(m_lanes, (U, LANES))], axis=0)

    fwd_row = lax.broadcasted_iota(jnp.int32, (U, LANES), 0) < H
    fwd_lane = lax.broadcasted_iota(jnp.int32, (U, LANES), 1) % U < H
    tile_l = lambda x: jnp.concatenate([x] * (L // LANES), axis=1)
    fwd_row_l = tile_l(fwd_row)
    lane_i = lax.broadcasted_iota(jnp.int32, (L, LANES), 1)
    row_i = lax.broadcasted_iota(jnp.int32, (L, L), 0)
    col_i = lax.broadcasted_iota(jnp.int32, (L, L), 1)
    ones = jnp.ones((L, M_DV), BF16)
    m0r = m_ref[0:U]
    m0l = m_ref[U:2 * U]
    def token_rows(i):
        jf, jb = i, n_sub - 1 - i
        return jf, jb, slice(jf * L, (jf + 1) * L), slice(jb * L, (jb + 1) * L)

    def operands(i, u):
        _, _, tf, tb = token_rows(i)
        sl = slice((u % H) * M_DK, (u % H + 1) * M_DK)
        if u < H:
            return qf_ref[0, tf, sl], kf_ref[0, tf, sl], vf_ref[0, tf, sl], row_i >= col_i
        return qb_ref[0, tb, sl], kb_ref[0, tb, sl], vb_ref[0, tb, sl], row_i <= col_i

    def scores(i, u):
        q, k, _, _ = operands(i, u)
        return _dot(q, k, _NT), jnp.transpose(k.astype(F32))

    order = [(i, u) for i in range(n_sub) for u in range(U)]
    pending = scores(*order[0])
    for i in range(n_sub):
        jf, jb, tf, tb = token_rows(i)
        pick_r = lambda n: jnp.where(fwd_row, sf_ref[jf, n * U:(n + 1) * U],
                                     sb_ref[jb, n * U:(n + 1) * U])
        pick_l = lambda n: jnp.where(fwd_lane, sf_ref[jf, n * U:(n + 1) * U],
                                     sb_ref[jb, n * U:(n + 1) * U])
        m_new_r = jnp.maximum(pick_r(0) + m0r, pick_r(1))
        m_new_l = jnp.maximum(pick_l(2) + m0l, pick_l(3))
        s_old = jnp.exp(pick_r(0) + m0r - m_new_r)
        r2 = jnp.where(fwd_row_l, rf_ref[0:U, tf], rb_ref[0:U, tb])
        w = jnp.exp(jnp.where(fwd_row_l, rf_ref[U:2 * U, tf], rb_ref[U:2 * U, tb])
                    - tile_l(m_new_r))
        m2 = m0r * LOG2E
        colsx = jnp.where(lane_i % U < H, cf_ref[tf], cb_ref[tb])
        g2 = jnp.maximum(colsx, m0l[0:1] * LOG2E)
        e_cols = jnp.exp2(-(pltpu.roll(colsx, LANES - U, axis=1) + g2))
        den = jnp.zeros((L, LANES), F32)
        for u in range(U):
            s, kt = pending
            nxt = i * U + u + 1
            pending = scores(*order[nxt]) if nxt < len(order) else None
            q, _, v, mask = operands(i, u)
            sl = slice((u % H) * M_DK, (u % H + 1) * M_DK)
            out, ts = (hf_ref, tf) if u < H else (hb_ref, tb)
            caug = c_ref[u]
            vaug = jnp.concatenate([v, ones], axis=1)
            g_col = g2[:, u:u + 1]
            x_intra = jnp.where(mask, s * jnp.exp2(r2[u:u + 1] - g_col), 0.0)
            x_inter = q.astype(F32) * jnp.exp2(m2[u:u + 1] - g_col)
            x = jnp.concatenate([x_intra, x_inter], axis=1).astype(BF16)
            tot = _dot(x, jnp.concatenate([vaug, caug.astype(BF16)], axis=0))
            out[0, ts, sl] = tot[:, :M_DV]
            den = jnp.where(lane_i == u, tot[:, M_DV:], den)
            c_ref[u] = _state_update(kt, w[u:u + 1], vaug, s_old[u:u + 1], caug)
        inv = 1.0 / jnp.maximum(jnp.abs(den), e_cols)
        for u in range(U):
            sl = slice((u % H) * M_DV, (u % H + 1) * M_DV)
            out, ts = (hf_ref, tf) if u < H else (hb_ref, tb)
            out[0, ts, sl] = out[0, ts, sl] * inv[:, u:u + 1]
        m0r, m0l = m_new_r, m_new_l
    m_ref[...] = jnp.concatenate([m0r, m0l], axis=0)


def _mlstm(qkv, prep, meta_qkv, meta_gates_t, chunk):
    B, S, _ = qkv.shape
    L = chunk
    assert S % L == 0 and L % LANES == 0
    NC = S // L
    rows, cols, sc = prep
    trim = jnp.asarray(np.triu(np.ones((META_PAD, META_PAD), np.float32)), BF16)
    G = 4 * M_HEADS
    J = math.gcd(M_CHUNKS_PER_STEP, NC)
    T = J * L
    NS = NC // J
    fwd = lambda j: pl.BlockSpec((1, T, M_WIDTH), lambda b, c: (b, c, j))
    bwd = lambda j: pl.BlockSpec((1, T, M_WIDTH), lambda b, c: (b, NS - 1 - c, j))
    in_specs = [
        fwd(0), fwd(1), fwd(2), bwd(0), bwd(1), bwd(2),
        pl.BlockSpec((2 * N_UNITS, T), lambda b, c: (0, b * NS + c)),
        pl.BlockSpec((2 * N_UNITS, T), lambda b, c: (0, b * NS + NS - 1 - c)),
        pl.BlockSpec((T, LANES), lambda b, c: (b * NS + c, 0)),
        pl.BlockSpec((T, LANES), lambda b, c: (b * NS + NS - 1 - c, 0)),
        pl.BlockSpec((J, 4 * N_UNITS, LANES), lambda b, c: (b * NS + c, 0, 0)),
        pl.BlockSpec((J, 4 * N_UNITS, LANES), lambda b, c: (b * NS + NS - 1 - c, 0, 0)),
        pl.BlockSpec((META_PAD, M_WIDTH), lambda b, c: (0, 1)),
        pl.BlockSpec((META_PAD, M_WIDTH), lambda b, c: (0, 2)),
        pl.BlockSpec((G, META_PAD), lambda b, c: (0, 0)),
        pl.BlockSpec((META_PAD, META_PAD), lambda b, c: (0, 0)),
    ]
    out_specs = (pl.BlockSpec((1, T, M_WIDTH), lambda b, c: (b, c, 0)),
                 pl.BlockSpec((1, T, M_WIDTH), lambda b, c: (b, NS - 1 - c, 0)))
    return pl.pallas_call(
        functools.partial(_mlstm_kernel, L=L),
        grid=(B, NS),
        in_specs=in_specs,
        out_specs=out_specs,
        out_shape=(jax.ShapeDtypeStruct((B, S, M_WIDTH), F32),
                   jax.ShapeDtypeStruct((B, S, M_WIDTH), F32)),
        scratch_shapes=[pltpu.VMEM((N_UNITS, M_DK, 2 * M_DV), F32),
                        pltpu.VMEM((2 * N_UNITS, LANES), F32)],
        compiler_params=pltpu.CompilerParams(dimension_semantics=("parallel", "arbitrary"),
                                             vmem_limit_bytes=VMEM_LIMIT_BYTES),
        name="mlstm",
    )(qkv, qkv, qkv, qkv, qkv, qkv, rows, rows, cols, cols, sc, sc,
      meta_qkv, meta_qkv, meta_gates_t, trim)


def _attn_kernel(q_ref, kp_ref, kc_ref, kn_ref, km_ref, bias_ref, o_ref, *, n_blocks):
    GW = A_GROUP * A_DH
    HB = BLOCK // 2
    n_q = q_ref.shape[1] // BLOCK
    lane = lax.broadcasted_iota(jnp.int32, (HB, GW), 1)
    head_masks = [((lane >= g * A_DH) & (lane < (g + 1) * A_DH)).astype(F32)
                  for g in range(A_GROUP)]
    wide = lambda x: jnp.concatenate([x, x], axis=1)
    zeros = jnp.zeros((HB - N_META, LANES), BF16)
    for i in range(n_q):
        blk = pl.program_id(1) * n_q + i
        var = jnp.where(blk == 0, 0, jnp.where(blk == n_blocks - 1, 2, 1))
        cur = kc_ref[0, i * BLOCK:(i + 1) * BLOCK]
        prev = kp_ref[0] if i == 0 else kc_ref[0, (i - 1) * BLOCK:i * BLOCK]
        nxt = kn_ref[0] if i == n_q - 1 else kc_ref[0, (i + 1) * BLOCK:(i + 2) * BLOCK]
        for j in range(A_KV_HEADS):
            for kind in range(2):
                sl = slice((kind * A_KV_HEADS + j) * LANES, (kind * A_KV_HEADS + j + 1) * LANES)
                extra = jnp.concatenate([km_ref[:, sl], zeros], axis=0)
                first = wide(jnp.concatenate([prev[:, sl], cur[:, sl], nxt[:HB, sl], extra], axis=0))
                second = wide(jnp.concatenate([extra, prev[HB:, sl], cur[:, sl], nxt[:, sl]], axis=0))
                if kind == 0:
                    keys = (first, second)
                else:
                    vals = (first, second)
            for h in range(2):
                rows = slice(i * BLOCK + h * HB, i * BLOCK + (h + 1) * HB)
                qg = q_ref[0, rows, j * GW:(j + 1) * GW].astype(F32)
                qs = jnp.concatenate([(qg * m).astype(BF16) for m in head_masks], axis=0)
                s = _dot(qs, keys[h], _NT) + bias_ref[var, j, h]
                e = jnp.exp2(s - jnp.max(s, axis=1, keepdims=True))
                den = jnp.sum(e, axis=1, keepdims=True)
                o = _dot(e.astype(BF16), vals[h]) / den
                og = o[(A_GROUP - 1) * HB:]
                for g in range(A_GROUP - 2, -1, -1):
                    og = jnp.where(lane < (g + 1) * A_DH, o[g * HB:(g + 1) * HB], og)
                o_ref[0, rows, j * GW:(j + 1) * GW] = og.astype(BF16)


def _attention(qa, kva, meta_kva, bias):
    B, S, _ = qa.shape
    NB = S // BLOCK
    assert S % BLOCK == 0 and NB >= 2
    W = KVA_WIDTH
    Q = math.gcd(A_BLOCKS_PER_STEP, NB)
    in_specs = [
        pl.BlockSpec((1, Q * BLOCK, A_WIDTH), lambda b, n: (b, n, 0)),
        pl.BlockSpec((1, BLOCK, W), lambda b, n: (b, jnp.maximum(n * Q - 1, 0), 0)),
        pl.BlockSpec((1, Q * BLOCK, W), lambda b, n: (b, n, 0)),
        pl.BlockSpec((1, BLOCK, W), lambda b, n: (b, jnp.minimum((n + 1) * Q, NB - 1), 0)),
        pl.BlockSpec((N_META, W), lambda b, n: (0, 0)),
        _const_spec(bias.shape),
    ]
    return pl.pallas_call(
        functools.partial(_attn_kernel, n_blocks=NB),
        grid=(B, NB // Q),
        in_specs=in_specs,
        out_specs=pl.BlockSpec((1, Q * BLOCK, A_WIDTH), lambda b, n: (b, n, 0)),
        out_shape=jax.ShapeDtypeStruct((B, S, A_WIDTH), BF16),
        compiler_params=pltpu.CompilerParams(dimension_semantics=("parallel", "parallel"),
                                             vmem_limit_bytes=VMEM_LIMIT_BYTES),
        name="window_attn",
    )(qa, kva, kva, kva, meta_kva, bias)


def _t5_bucket_np(rel):
    nb = N_BUCKETS // 2
    max_exact = nb // 2
    ret = np.where(rel > 0, nb, 0)
    n = np.abs(rel)
    nf = np.maximum(n, 1).astype(np.float32)
    large = max_exact + (np.log(nf / max_exact) / math.log(MAX_DIST / max_exact)
                         * (nb - max_exact)).astype(np.int32)
    large = np.minimum(large, nb - 1)
    return ret + np.where(n < max_exact, n, large)


def _attention_bias(rel_bias, sink_logits, n_blocks):
    HB = BLOCK // 2
    t = np.arange(BLOCK)
    s_off = np.arange(3 * BLOCK) - BLOCK
    rel_band = s_off[None, :] - t[:, None]
    band_ok = np.abs(rel_band) <= WINDOW
    band_bucket = _t5_bucket_np(rel_band)
    q_pos = N_META + np.arange(n_blocks * BLOCK).reshape(n_blocks, BLOCK)
    meta_bucket = _t5_bucket_np(np.arange(N_META)[None, None, :] - q_pos[..., None])
    assert (meta_bucket[1:] == meta_bucket[1:2]).all()
    table = rel_bias.astype(F32)
    lookup = lambda bucket: jnp.einsum(
        '...k,kh->...h', jnp.asarray(np.eye(N_BUCKETS, dtype=np.float32)[bucket]), table,
        precision=lax.Precision.HIGHEST)
    neg = jnp.full((BLOCK, 3 * BLOCK, A_HEADS), NEG, F32)
    variants = []
    for var in range(3):
        ok = band_ok.copy()
        if var == 0:
            ok[:, :BLOCK] = False
        if var == 2:
            ok[:, 2 * BLOCK:] = False
        band = jnp.where(jnp.asarray(ok)[:, :, None], lookup(band_bucket), neg)
        meta = lookup(meta_bucket[0 if var == 0 else 1])
        sink = jnp.broadcast_to(sink_logits.astype(F32)[None, None, :], (BLOCK, 1, A_HEADS))
        padc = jnp.full((BLOCK, HB - N_META - 1, A_HEADS), NEG, F32)
        extra = jnp.concatenate([meta, sink, padc], axis=1)
        halves = [jnp.concatenate([band[:HB, :3 * BLOCK - HB], extra[:HB]], axis=1),
                  jnp.concatenate([extra[HB:], band[HB:, HB:]], axis=1)]
        full = jnp.transpose(jnp.stack(halves), (3, 0, 1, 2))
        full = full.reshape(A_KV_HEADS, A_GROUP, 2, HB, 3 * BLOCK)
        full = jnp.transpose(full, (0, 2, 1, 3, 4)).reshape(A_KV_HEADS, 2, A_GROUP * HB, 3 * BLOCK)
        variants.append(full)
    return jnp.stack(variants) * LOG2E


def _stage_d_kernel(x1_ref, hf_ref, hb_ref, at_ref, nm_ref, wo_ref, mg_ref, wga_ref, wgb_ref,
                    wupm_ref, wupa_ref, wout_ref, nf_ref, wgu_ref, wd_ref, out_ref, a_scr, hm_scr):
    streams = _row_streams(x1_ref.shape[0])

    def gates(rows):
        y_a = _dot(at_ref[rows], wupa_ref[...])
        x1 = x1_ref[rows]
        hn = _rms(x1, nm_ref[...]).astype(BF16)
        o = _dot(hn, wo_ref[...])
        for h in range(M_HEADS):
            sl = slice(h * M_DV, (h + 1) * M_DV)
            hs = hf_ref[rows, sl] + hb_ref[rows, sl]
            hm_scr[rows, sl] = (_rms(hs, mg_ref[:, sl]) * _sigmoid(o[:, sl])).astype(BF16)
        return x1, hn, y_a

    def mix(rows, x1, hn, y_a):
        mixed = _sigmoid(_dot(hn, wga_ref[...])) * _dot(hm_scr[rows], wupm_ref[...])
        mixed = mixed + _sigmoid(_dot(hn, wgb_ref[...])) * y_a
        return x1 + _dot(mixed.astype(BF16), wout_ref[...])

    pre = [gates(rows) for rows in streams]
    x2 = [mix(rows, *p) for rows, p in zip(streams, pre)]
    for rows, x in zip(streams, x2):
        out_ref[rows] = _ffn_residual(x, nf_ref, wgu_ref, wd_ref, a_scr.at[rows])


def _stage_d(x1, hf, hb, at, w, tile):
    n = x1.shape[0]
    row = lambda width: pl.BlockSpec((tile, width), lambda i: (i, 0))
    consts = (w['nm'], w['wo'], w['mg'], w['wga'], w['wgb'], w['wupm'], w['wupa'], w['wout'],
              w['nf2'], w['wgu2'], w['wd2'])
    return pl.pallas_call(
        _stage_d_kernel,
        grid=(n // tile,),
        in_specs=[row(D_MODEL), row(M_WIDTH), row(M_WIDTH), row(A_WIDTH)]
                 + [_const_spec(c.shape) for c in consts],
        out_specs=row(D_MODEL),
        out_shape=jax.ShapeDtypeStruct((n, D_MODEL), F32),
        scratch_shapes=[pltpu.VMEM((tile, D_FF), BF16), pltpu.VMEM((tile, M_WIDTH), BF16)],
        compiler_params=pltpu.CompilerParams(dimension_semantics=("parallel",),
                                             vmem_limit_bytes=VMEM_LIMIT_BYTES),
        name="stage_d",
    )(x1, hf, hb, at, *consts)


def _prep_weights(norm_ffn1, w_ffn1_gu, w_ffn1_down, norm_mix, w_in, b_gates, m_out_gain,
                  q_norm_gain, k_norm_gain, w_up_m, w_up_a, w_out, norm_ffn2, w_ffn2_gu,
                  w_ffn2_down):
    row = lambda v: v.reshape(1, -1).astype(F32)
    o_qm, o_om = 0, 3 * M_WIDTH
    o_gm = o_om + M_WIDTH
    o_qa = o_gm + 4 * M_HEADS
    o_ka = o_qa + A_WIDTH
    o_va = o_ka + A_KV_HEADS * A_DH
    o_ga = o_va + A_KV_HEADS * A_DH
    o_gb = o_ga + D_MODEL
    cols = lambda a, b: w_in[:, a:b]
    n_g = 4 * M_HEADS
    wkvg = jnp.pad(jnp.concatenate([cols(o_ka, o_ga), cols(o_gm, o_qa)], axis=1),
                   ((0, 0), (0, MXU_DIM - n_g)))
    gsum = np.kron(np.eye(LANES // A_DH, dtype=np.float32), np.ones((A_DH, A_DH), np.float32))
    return {
        'nf1': row(norm_ffn1), 'wgu1': w_ffn1_gu.astype(BF16), 'wd1': w_ffn1_down.astype(BF16),
        'nm': row(norm_mix),
        'wqkv': cols(o_qm, o_om).astype(BF16),
        'wo': cols(o_om, o_gm).astype(BF16),
        'bg': jnp.pad(row(b_gates), ((0, 0), (0, LANES - n_g))),
        'wqa': cols(o_qa, o_ka).astype(BF16),
        'wkvg': wkvg.astype(BF16),
        'qg': jnp.tile(row(q_norm_gain), (1, A_HEADS)),
        'kg': jnp.tile(row(k_norm_gain), (1, LANES // A_DH)),
        'gsum': jnp.asarray(np.concatenate([gsum, gsum], axis=0), BF16),
        'wga': cols(o_ga, o_gb).astype(BF16), 'wgb': cols(o_gb, o_gb + D_MODEL).astype(BF16),
        'mg': row(m_out_gain),
        'wupm': w_up_m.astype(BF16), 'wupa': w_up_a.astype(BF16), 'wout': w_out.astype(BF16),
        'nf2': row(norm_ffn2), 'wgu2': w_ffn2_gu.astype(BF16), 'wd2': w_ffn2_down.astype(BF16),
    }


def _encode(x, w, meta, bias_fn, tile, chunk):
    B, S, _ = x.shape
    meta_qkv, meta_gates, meta_kva = meta
    x1, qkv, qa, kva, *prep = _stage_a(x.reshape(B * S, D_MODEL), w, tile, chunk)
    hf, hb = _mlstm(qkv.reshape(B, S, -1), prep, meta_qkv, meta_gates, chunk)
    at = _attention(qa.reshape(B, S, -1), kva.reshape(B, S, -1), meta_kva, bias_fn(S // BLOCK))
    y = _stage_d(x1, hf.reshape(B * S, -1), hb.reshape(B * S, -1), at.reshape(B * S, -1), w, tile)
    return y.reshape(B, S, D_MODEL)


def _layer(x_groups, meta_tokens, rel_bias, norm_ffn1, w_ffn1_gu, w_ffn1_down, norm_mix, w_in,
           b_gates, m_out_gain, q_norm_gain, k_norm_gain, sink_logits, w_up_m, w_up_a, w_out,
           norm_ffn2, w_ffn2_gu, w_ffn2_down, tile=TOKEN_TILE, chunk=M_CHUNK):
    assert norm_ffn1.shape[0] == 1, "single layer"
    w = _prep_weights(norm_ffn1[0], w_ffn1_gu[0], w_ffn1_down[0], norm_mix[0], w_in[0], b_gates[0],
                      m_out_gain[0], q_norm_gain[0], k_norm_gain[0], w_up_m[0], w_up_a[0],
                      w_out[0], norm_ffn2[0], w_ffn2_gu[0], w_ffn2_down[0])
    meta_x = jnp.pad(meta_tokens.astype(F32), ((0, META_PAD - N_META), (0, 0)))
    _, m_qkv, _, m_kva, m_gates = _stage_a(meta_x, w, META_PAD)
    meta = (m_qkv, m_gates, m_kva[:N_META])
    bias_fn = functools.lru_cache(None)(
        lambda nb: _attention_bias(rel_bias, sink_logits[0], nb))
    return tuple(_encode(x, w, meta, bias_fn, tile, chunk) for x in x_groups)


def kernel(x_prompt, x_sample, meta_tokens, rel_bias, norm_ffn1, w_ffn1_gu, w_ffn1_down, norm_mix,
           w_in, b_gates, m_out_gain, q_norm_gain, k_norm_gain, sink_logits, w_up_m, w_up_a, w_out,
           norm_ffn2, w_ffn2_gu, w_ffn2_down):
    return _layer((x_prompt, x_sample), meta_tokens, rel_bias, norm_ffn1, w_ffn1_gu, w_ffn1_down,
                  norm_mix, w_in, b_gates, m_out_gain, q_norm_gain, k_norm_gain, sink_logits,
                  w_up_m, w_up_a, w_out, norm_ffn2, w_ffn2_gu, w_ffn2_down)
```

```python
import functools
import math

import numpy as np
import jax
import jax.numpy as jnp
from jax import lax
from jax.experimental import pallas as pl
from jax.experimental.pallas import tpu as pltpu

F32 = jnp.float32
BF16 = jnp.bfloat16

D_MODEL = 1024
D_FF = 2816
N_META = 16
M_HEADS = 4
M_DK = 128
M_DV = 128
M_WIDTH = M_HEADS * M_DV
A_HEADS = 8
A_KV_HEADS = 2
A_GROUP = A_HEADS // A_KV_HEADS
A_DH = 64
A_WIDTH = A_HEADS * A_DH
WINDOW = 128
BLOCK = 128
N_BUCKETS = 32
MAX_DIST = 128
EPS = 1e-6
NEG = -1e30
LOG2E = 1.4426950408889634

LANES = 128
MXU_DIM = 256
VMEM_LIMIT_BYTES = 58 * 1024 * 1024

FF_CHUNK = MXU_DIM
N_FF_CHUNKS = D_FF // FF_CHUNK
TOKEN_TILE = 512
M_CHUNK = 256
META_PAD = 128
M_CHUNKS_PER_STEP = 8
A_BLOCKS_PER_STEP = 16
KVA_WIDTH = 2 * A_KV_HEADS * LANES

_NN = (((1,), (0,)), ((), ()))
_NT = (((1,), (1,)), ((), ()))


def _dot(a, b, dims=_NN):
    return lax.dot_general(a, b, dims, preferred_element_type=F32)


def _split_dot_lhs(x_f32, a_bf16):
    hi = x_f32.astype(BF16)
    lo = (x_f32 - hi.astype(F32)).astype(BF16)
    return _dot(hi, a_bf16) + _dot(lo, a_bf16)


def _rms(x, g):
    ms = jnp.mean(x * x, axis=-1, keepdims=True)
    return x * lax.rsqrt(ms + EPS) * g


def _sigmoid(x):
    return 1.0 / (1.0 + jnp.exp(-x))


def _log_sigmoid(x):
    return jnp.minimum(x, 0.0) - jnp.log1p(jnp.exp(-jnp.abs(x)))


def _row_streams(n_rows):
    n = 2 if n_rows % (2 * MXU_DIM) == 0 else 1
    return [slice(r * (n_rows // n), (r + 1) * (n_rows // n)) for r in range(n)]


def _const_spec(shape):
    nd = len(shape)
    return pl.BlockSpec(shape, lambda *_: (0,) * nd, pipeline_mode=pl.Buffered(1))


def _ffn_residual(x, nf_ref, wgu_ref, wd_ref, a_scr):
    xn = _rms(x, nf_ref[...]).astype(BF16)
    for j in range(N_FF_CHUNKS):
        sl = slice(j * FF_CHUNK, (j + 1) * FF_CHUNK)
        g = _dot(xn, wgu_ref[:, sl])
        u = _dot(xn, wgu_ref[:, D_FF + j * FF_CHUNK:D_FF + (j + 1) * FF_CHUNK])
        a_scr[:, sl] = (g * _sigmoid(g) * u).astype(BF16)
    y = _dot(a_scr[...], wd_ref[...])
    return x + 0.5 * y


def _group_sumsq(x, gsum_ref):
    sq = x * x
    hi = sq.astype(BF16)
    lo = (sq - hi.astype(F32)).astype(BF16)
    return _dot(jnp.concatenate([hi, lo], axis=1), gsum_ref[...])


def _lane_halves_tiled(x):
    swapped = pltpu.roll(x, LANES // 2, axis=1)
    low = lax.broadcasted_iota(jnp.int32, x.shape, 1) < LANES // 2
    return jnp.where(low, x, swapped), jnp.where(low, swapped, x)


def _stage_a_kernel(x_ref, nf_ref, wgu_ref, wd_ref, nm_ref, wqkv_ref, bg_ref, wqa_ref,
                    wkvg_ref, qg_ref, kg_ref, gsum_ref, triu_ref,
                    x1_ref, qkv_ref, qa_ref, kva_ref, *gate_outs_and_scratch, chunk):
    *gate_outs, a_scr = gate_outs_and_scratch
    streams = _row_streams(x_ref.shape[0])

    def gate_path(rows, x1):
        x1_ref[rows] = x1
        hn = _rms(x1, nm_ref[...]).astype(BF16)
        kvg = _dot(hn, wkvg_ref[...])
        gates_t = jnp.transpose(kvg[:, 2 * LANES:3 * LANES] + bg_ref[...])[:4 * M_HEADS]
        k = kvg[:, :LANES]
        kn = k * lax.rsqrt(_group_sumsq(k, gsum_ref) * (1.0 / A_DH) + EPS) * kg_ref[...]
        tiles = _lane_halves_tiled(kn) + _lane_halves_tiled(kvg[:, LANES:2 * LANES])
        for t, tiled in enumerate(tiles):
            kva_ref[rows, t * LANES:(t + 1) * LANES] = tiled.astype(BF16)
        return hn, gates_t

    def gate_scalars(rows, gates_t):
        if chunk is None:
            gate_outs[0][:, rows] = gates_t
            return
        prows_ref, pcols_ref, psc_ref = gate_outs
        for c in range((rows.stop - rows.start) // chunk):
            t0 = rows.start + c * chunk
            p_rows, p_cols, p_sc = _gate_prep_chunk(gates_t[:, c * chunk:(c + 1) * chunk], triu_ref)
            prows_ref[:, t0:t0 + chunk] = p_rows
            pcols_ref[t0:t0 + chunk, :] = p_cols
            psc_ref[t0 // chunk] = p_sc

    def project(rows, hn):
        qkv_ref[rows, :M_WIDTH] = (_dot(hn, wqkv_ref[:, :M_WIDTH]) * (M_DK ** -0.5)).astype(BF16)
        qkv_ref[rows, M_WIDTH:] = _dot(hn, wqkv_ref[:, M_WIDTH:]).astype(BF16)
        q_all = _dot(hn, wqa_ref[...])
        same_head = (lax.broadcasted_iota(jnp.int32, (MXU_DIM, MXU_DIM), 0) // A_DH
                     == lax.broadcasted_iota(jnp.int32, (MXU_DIM, MXU_DIM), 1) // A_DH).astype(BF16)
        for t in range(A_WIDTH // MXU_DIM):
            sl = slice(t * MXU_DIM, (t + 1) * MXU_DIM)
            q = q_all[:, sl]
            ss = _dot((q * q).astype(BF16), same_head)
            qn = q * lax.rsqrt(ss * (1.0 / A_DH) + EPS) * qg_ref[:, sl] * (A_DH ** -0.5 * LOG2E)
            qa_ref[rows, sl] = qn.astype(BF16)

    x1 = [_ffn_residual(x_ref[rows], nf_ref, wgu_ref, wd_ref, a_scr.at[rows]) for rows in streams]
    pre = [gate_path(rows, x) for rows, x in zip(streams, x1)]
    for rows, (hn, gates_t) in zip(streams, pre):
        gate_scalars(rows, gates_t)
        project(rows, hn)


def _stage_a(x2d, w, tile, chunk=None):
    n = x2d.shape[0]
    assert n % tile == 0 and (chunk is None or tile % chunk == 0)
    row = lambda width: pl.BlockSpec((tile, width), lambda i: (i, 0))
    col = lambda height: pl.BlockSpec((height, tile), lambda i: (0, i))
    out_shape = [
        jax.ShapeDtypeStruct((n, D_MODEL), F32),
        jax.ShapeDtypeStruct((n, 3 * M_WIDTH), BF16),
        jax.ShapeDtypeStruct((n, A_WIDTH), BF16),
        jax.ShapeDtypeStruct((n, KVA_WIDTH), BF16),
    ]
    out_specs = [row(D_MODEL), row(3 * M_WIDTH), row(A_WIDTH), row(KVA_WIDTH)]
    if chunk is None:
        out_shape += [jax.ShapeDtypeStruct((4 * M_HEADS, n), F32)]
        out_specs += [col(4 * M_HEADS)]
    else:
        out_shape += [jax.ShapeDtypeStruct((2 * N_UNITS, n), F32),
                      jax.ShapeDtypeStruct((n, LANES), F32),
                      jax.ShapeDtypeStruct((n // chunk, 4 * N_UNITS, LANES), F32)]
        out_specs += [col(2 * N_UNITS), row(LANES),
                      pl.BlockSpec((tile // chunk, 4 * N_UNITS, LANES), lambda i: (i, 0, 0))]
    L = M_CHUNK if chunk is None else chunk
    triu = jnp.asarray(np.triu(np.ones((L, L), np.float32)), BF16)
    consts = (w['nf1'], w['wgu1'], w['wd1'], w['nm'], w['wqkv'], w['bg'], w['wqa'],
              w['wkvg'], w['qg'], w['kg'], w['gsum'], triu)
    return pl.pallas_call(
        functools.partial(_stage_a_kernel, chunk=chunk),
        grid=(n // tile,),
        in_specs=[row(D_MODEL)] + [_const_spec(c.shape) for c in consts],
        out_specs=out_specs,
        out_shape=out_shape,
        scratch_shapes=[pltpu.VMEM((tile, D_FF), BF16)],
        compiler_params=pltpu.CompilerParams(dimension_semantics=("parallel",),
                                             vmem_limit_bytes=VMEM_LIMIT_BYTES),
        name="stage_a",
    )(x2d, *consts)


N_UNITS = 2 * M_HEADS


def _running_max(x, L):
    row = lax.broadcasted_iota(jnp.int32, x.shape, 0)
    lane = lax.broadcasted_iota(jnp.int32, x.shape, 1)
    fwd = row < M_HEADS
    k = 1
    while k < L:
        prev = jnp.where(lane >= k, pltpu.roll(x, k, axis=1), NEG)
        nxt = jnp.where(lane < L - k, pltpu.roll(x, L - k, axis=1), NEG)
        x = jnp.maximum(x, jnp.where(fwd, prev, nxt))
        k *= 2
    return x


def _state_update(kt_f32, w_row, vaug, s_old_row, caug):
    wkt = (kt_f32 * w_row).astype(BF16)
    s_old = jnp.concatenate([s_old_row, s_old_row], axis=1)
    return s_old * caug + _dot(wkt, vaug)


def _gate_prep_chunk(g, triu_ref):
    H = M_HEADS
    L = g.shape[1]
    eye = (lax.broadcasted_iota(jnp.int32, (N_UNITS, LANES), 0)
           == lax.broadcasted_iota(jnp.int32, (N_UNITS, LANES), 1))
    lane_vec = lambda x: jnp.broadcast_to(
        jnp.sum(jnp.where(eye, x, 0.0), axis=0, keepdims=True), (N_UNITS, LANES))
    lff = _log_sigmoid(g[H:2 * H])
    lfb = _log_sigmoid(g[3 * H:4 * H])
    cs = _split_dot_lhs(jnp.concatenate([lff, lfb], axis=0), triu_ref[...])
    bf = cs[0:H]
    bb = cs[H:2 * H, L - 1:L] - cs[H:2 * H] + lfb
    b = jnp.concatenate([bf, bb], axis=0)
    r = jnp.concatenate([g[0:H], g[2 * H:3 * H]], axis=0) - b
    b_last = jnp.concatenate([jnp.broadcast_to(bf[:, L - 1:L], (H, LANES)),
                              jnp.broadcast_to(bb[:, 0:1], (H, LANES))], axis=0)
    a = jnp.concatenate([b_last] * (L // LANES), axis=1) + r
    m_loc = jnp.broadcast_to(jnp.max(a, axis=1, keepdims=True), (N_UNITS, LANES))
    packed = jnp.concatenate([_running_max(r, L) * LOG2E, b * LOG2E,
                              jnp.zeros((LANES - 2 * N_UNITS, L), F32)], axis=0)
    return (jnp.concatenate([r * LOG2E, a], axis=0), jnp.transpose(packed),
            jnp.concatenate([b_last, m_loc, lane_vec(b_last), lane_vec(m_loc)], axis=0))


def _mlstm_kernel(qf_ref, kf_ref, vf_ref, qb_ref, kb_ref, vb_ref, rf_ref, rb_ref, cf_ref, cb_ref,
                  sf_ref, sb_ref, mk_ref, mv_ref, mg_ref, trim_ref,
                  hf_ref, hb_ref, c_ref, m_ref, *, L):
    H = M_HEADS
    U = N_UNITS
    n_sub = qf_ref.shape[1] // L
    c = pl.program_id(1)

    @pl.when(c == 0)
    def _init():
        c_ref[...] = jnp.zeros_like(c_ref)
        g = mg_ref[...]
        valid = lax.broadcasted_iota(jnp.int32, g.shape, 1) < N_META
        lf = jnp.where(valid, _log_sigmoid(g), 0.0)
        li = jnp.where(valid, g, NEG)
        cs = _split_dot_lhs(lf, trim_ref[...])
        tot = cs[H:2 * H, META_PAD - 1:META_PAD]
        a = tot - cs[H:2 * H] + li[0:H]
        m_new = jnp.maximum(tot, jnp.max(a, axis=1, keepdims=True))
        w = jnp.exp(a - m_new)
        ones_m = jnp.ones((META_PAD, M_DV), BF16)
        for h in range(H):
            sl = slice(h * M_DK, (h + 1) * M_DK)
            vaug = jnp.concatenate([mv_ref[:, sl], ones_m], axis=1)
            kt = jnp.transpose(mk_ref[:, sl].astype(F32))
            c_ref[h] = _state_update(kt, w[h:h + 1], vaug, jnp.zeros((1, LANES), F32),
                                     jnp.zeros(c_ref.shape[1:], F32))
        m_rows = jnp.concatenate([jnp.broadcast_to(m_new, (H, LANES)),
                                  jnp.zeros((H, LANES), F32)], axis=0)
        eye = (lax.broadcasted_iota(jnp.int32, (U, LANES), 0)
               == lax.broadcasted_iota(jnp.int32, (U, LANES), 1))
        m_lanes = jnp.sum(jnp.where(eye, m_rows, 0.0), axis=0, keepdims=True)
        m_ref[...] = jnp.concatenate([m_rows, jnp.broadcast_to(m_lanes, (U, LANES))], axis=0)

    fwd_row = lax.broadcasted_iota(jnp.int32, (U, LANES), 0) < H
    fwd_lane = lax.broadcasted_iota(jnp.int32, (U, LANES), 1) % U < H
    tile_l = lambda x: jnp.concatenate([x] * (L // LANES), axis=1)
    fwd_row_l = tile_l(fwd_row)
    lane_i = lax.broadcasted_iota(jnp.int32, (L, LANES), 1)
    row_i = lax.broadcasted_iota(jnp.int32, (L, L), 0)
    col_i = lax.broadcasted_iota(jnp.int32, (L, L), 1)
    ones = jnp.ones((L, M_DV), BF16)
    m0r = m_ref[0:U]
    m0l = m_ref[U:2 * U]
    def token_rows(i):
        jf, jb = i, n_sub - 1 - i
        return jf, jb, slice(jf * L, (jf + 1) * L), slice(jb * L, (jb + 1) * L)

    def operands(i, u):
        _, _, tf, tb = token_rows(i)
        sl = slice((u % H) * M_DK, (u % H + 1) * M_DK)
        if u < H:
            return qf_ref[0, tf, sl], kf_ref[0, tf, sl], vf_ref[0, tf, sl], row_i >= col_i
        return qb_ref[0, tb, sl], kb_ref[0, tb, sl], vb_ref[0, tb, sl], row_i <= col_i

    def scores(i, u):
        q, k, _, _ = operands(i, u)
        return _dot(q, k, _NT), jnp.transpose(k.astype(F32))

    order = [(i, u) for i in range(n_sub) for u in range(U)]
    pending = scores(*order[0])
    for i in range(n_sub):
        jf, jb, tf, tb = token_rows(i)
        pick_r = lambda n: jnp.where(fwd_row, sf_ref[jf, n * U:(n + 1) * U],
                                     sb_ref[jb, n * U:(n + 1) * U])
        pick_l = lambda n: jnp.where(fwd_lane, sf_ref[jf, n * U:(n + 1) * U],
                                     sb_ref[jb, n * U:(n + 1) * U])
        m_new_r = jnp.maximum(pick_r(0) + m0r, pick_r(1))
        m_new_l = jnp.maximum(pick_l(2) + m0l, pick_l(3))
        s_old = jnp.exp(pick_r(0) + m0r - m_new_r)
        r2 = jnp.where(fwd_row_l, rf_ref[0:U, tf], rb_ref[0:U, tb])
        w = jnp.exp(jnp.where(fwd_row_l, rf_ref[U:2 * U, tf], rb_ref[U:2 * U, tb])
                    - tile_l(m_new_r))
        m2 = m0r * LOG2E
        colsx = jnp.where(lane_i % U < H, cf_ref[tf], cb_ref[tb])
        g2 = jnp.maximum(colsx, m0l[0:1] * LOG2E)
        e_cols = jnp.exp2(-(pltpu.roll(colsx, LANES - U, axis=1) + g2))
        den = jnp.zeros((L, LANES), F32)
        for u in range(U):
            s, kt = pending
            nxt = i * U + u + 1
            pending = scores(*order[nxt]) if nxt < len(order) else None
            q, _, v, mask = operands(i, u)
            sl = slice((u % H) * M_DK, (u % H + 1) * M_DK)
            out, ts = (hf_ref, tf) if u < H else (hb_ref, tb)
            caug = c_ref[u]
            vaug = jnp.concatenate([v, ones], axis=1)
            g_col = g2[:, u:u + 1]
            x_intra = jnp.where(mask, s * jnp.exp2(r2[u:u + 1] - g_col), 0.0)
            x_inter = q.astype(F32) * jnp.exp2(m2[u:u + 1] - g_col)
            x = jnp.concatenate([x_intra, x_inter], axis=1).astype(BF16)
            tot = _dot(x, jnp.concatenate([vaug, caug.astype(BF16)], axis=0))
            out[0, ts, sl] = tot[:, :M_DV]
            den = jnp.where(lane_i == u, tot[:, M_DV:], den)
            c_ref[u] = _state_update(kt, w[u:u + 1], vaug, s_old[u:u + 1], caug)
        inv = 1.0 / jnp.maximum(jnp.abs(den), e_cols)
        for u in range(U):
            sl = slice((u % H) * M_DV, (u % H + 1) * M_DV)
            out, ts = (hf_ref, tf) if u < H else (hb_ref, tb)
            out[0, ts, sl] = out[0, ts, sl] * inv[:, u:u + 1]
        m0r, m0l = m_new_r, m_new_l
    m_ref[...] = jnp.concatenate([m0r, m0l], axis=0)


def _mlstm(qkv, prep, meta_qkv, meta_gates_t, chunk):
    B, S, _ = qkv.shape
    L = chunk
    assert S % L == 0 and L % LANES == 0
    NC = S // L
    rows, cols, sc = prep
    trim = jnp.asarray(np.triu(np.ones((META_PAD, META_PAD), np.float32)), BF16)
    G = 4 * M_HEADS
    J = math.gcd(M_CHUNKS_PER_STEP, NC)
    T = J * L
    NS = NC // J
    fwd = lambda j: pl.BlockSpec((1, T, M_WIDTH), lambda b, c: (b, c, j))
    bwd = lambda j: pl.BlockSpec((1, T, M_WIDTH), lambda b, c: (b, NS - 1 - c, j))
    in_specs = [
        fwd(0), fwd(1), fwd(2), bwd(0), bwd(1), bwd(2),
        pl.BlockSpec((2 * N_UNITS, T), lambda b, c: (0, b * NS + c)),
        pl.BlockSpec((2 * N_UNITS, T), lambda b, c: (0, b * NS + NS - 1 - c)),
        pl.BlockSpec((T, LANES), lambda b, c: (b * NS + c, 0)),
        pl.BlockSpec((T, LANES), lambda b, c: (b * NS + NS - 1 - c, 0)),
        pl.BlockSpec((J, 4 * N_UNITS, LANES), lambda b, c: (b * NS + c, 0, 0)),
        pl.BlockSpec((J, 4 * N_UNITS, LANES), lambda b, c: (b * NS + NS - 1 - c, 0, 0)),
        pl.BlockSpec((META_PAD, M_WIDTH), lambda b, c: (0, 1)),
        pl.BlockSpec((META_PAD, M_WIDTH), lambda b, c: (0, 2)),
        pl.BlockSpec((G, META_PAD), lambda b, c: (0, 0)),
        pl.BlockSpec((META_PAD, META_PAD), lambda b, c: (0, 0)),
    ]
    out_specs = (pl.BlockSpec((1, T, M_WIDTH), lambda b, c: (b, c, 0)),
                 pl.BlockSpec((1, T, M_WIDTH), lambda b, c: (b, NS - 1 - c, 0)))
    return pl.pallas_call(
        functools.partial(_mlstm_kernel, L=L),
        grid=(B, NS),
        in_specs=in_specs,
        out_specs=out_specs,
        out_shape=(jax.ShapeDtypeStruct((B, S, M_WIDTH), F32),
                   jax.ShapeDtypeStruct((B, S, M_WIDTH), F32)),
        scratch_shapes=[pltpu.VMEM((N_UNITS, M_DK, 2 * M_DV), F32),
                        pltpu.VMEM((2 * N_UNITS, LANES), F32)],
        compiler_params=pltpu.CompilerParams(dimension_semantics=("parallel", "arbitrary"),
                                             vmem_limit_bytes=VMEM_LIMIT_BYTES),
        name="mlstm",
    )(qkv, qkv, qkv, qkv, qkv, qkv, rows, rows, cols, cols, sc, sc,
      meta_qkv, meta_qkv, meta_gates_t, trim)


def _attn_kernel(q_ref, kp_ref, kc_ref, kn_ref, km_ref, bias_ref, o_ref, *, n_blocks):
    GW = A_GROUP * A_DH
    HB = BLOCK // 2
    n_q = q_ref.shape[1] // BLOCK
    lane = lax.broadcasted_iota(jnp.int32, (HB, GW), 1)
    head_masks = [((lane >= g * A_DH) & (lane < (g + 1) * A_DH)).astype(F32)
                  for g in range(A_GROUP)]
    wide = lambda x: jnp.concatenate([x, x], axis=1)
    zeros = jnp.zeros((HB - N_META, LANES), BF16)
    for i in range(n_q):
        blk = pl.program_id(1) * n_q + i
        var = jnp.where(blk == 0, 0, jnp.where(blk == n_blocks - 1, 2, 1))
        cur = kc_ref[0, i * BLOCK:(i + 1) * BLOCK]
        prev = kp_ref[0] if i == 0 else kc_ref[0, (i - 1) * BLOCK:i * BLOCK]
        nxt = kn_ref[0] if i == n_q - 1 else kc_ref[0, (i + 1) * BLOCK:(i + 2) * BLOCK]
        for j in range(A_KV_HEADS):
            for kind in range(2):
                sl = slice((kind * A_KV_HEADS + j) * LANES, (kind * A_KV_HEADS + j + 1) * LANES)
                extra = jnp.concatenate([km_ref[:, sl], zeros], axis=0)
                first = wide(jnp.concatenate([prev[:, sl], cur[:, sl], nxt[:HB, sl], extra], axis=0))
                second = wide(jnp.concatenate([extra, prev[HB:, sl], cur[:, sl], nxt[:, sl]], axis=0))
                if kind == 0:
                    keys = (first, second)
                else:
                    vals = (first, second)
            for h in range(2):
                rows = slice(i * BLOCK + h * HB, i * BLOCK + (h + 1) * HB)
                qg = q_ref[0, rows, j * GW:(j + 1) * GW].astype(F32)
                qs = jnp.concatenate([(qg * m).astype(BF16) for m in head_masks], axis=0)
                s = _dot(qs, keys[h], _NT) + bias_ref[var, j, h]
                e = jnp.exp2(s - jnp.max(s, axis=1, keepdims=True))
                den = jnp.sum(e, axis=1, keepdims=True)
                o = _dot(e.astype(BF16), vals[h]) / den
                og = o[(A_GROUP - 1) * HB:]
                for g in range(A_GROUP - 2, -1, -1):
                    og = jnp.where(lane < (g + 1) * A_DH, o[g * HB:(g + 1) * HB], og)
                o_ref[0, rows, j * GW:(j + 1) * GW] = og.astype(BF16)


def _attention(qa, kva, meta_kva, bias):
    B, S, _ = qa.shape
    NB = S // BLOCK
    assert S % BLOCK == 0 and NB >= 2
    W = KVA_WIDTH
    Q = math.gcd(A_BLOCKS_PER_STEP, NB)
    in_specs = [
        pl.BlockSpec((1, Q * BLOCK, A_WIDTH), lambda b, n: (b, n, 0)),
        pl.BlockSpec((1, BLOCK, W), lambda b, n: (b, jnp.maximum(n * Q - 1, 0), 0)),
        pl.BlockSpec((1, Q * BLOCK, W), lambda b, n: (b, n, 0)),
        pl.BlockSpec((1, BLOCK, W), lambda b, n: (b, jnp.minimum((n + 1) * Q, NB - 1), 0)),
        pl.BlockSpec((N_META, W), lambda b, n: (0, 0)),
        _const_spec(bias.shape),
    ]
    return pl.pallas_call(
        functools.partial(_attn_kernel, n_blocks=NB),
        grid=(B, NB // Q),
        in_specs=in_specs,
        out_specs=pl.BlockSpec((1, Q * BLOCK, A_WIDTH), lambda b, n: (b, n, 0)),
        out_shape=jax.ShapeDtypeStruct((B, S, A_WIDTH), BF16),
        compiler_params=pltpu.CompilerParams(dimension_semantics=("parallel", "parallel"),
                                             vmem_limit_bytes=VMEM_LIMIT_BYTES),
        name="window_attn",
    )(qa, kva, kva, kva, meta_kva, bias)


def _t5_bucket_np(rel):
    nb = N_BUCKETS // 2
    max_exact = nb // 2
    ret = np.where(rel > 0, nb, 0)
    n = np.abs(rel)
    nf = np.maximum(n, 1).astype(np.float32)
    large = max_exact + (np.log(nf / max_exact) / math.log(MAX_DIST / max_exact)
                         * (nb - max_exact)).astype(np.int32)
    large = np.minimum(large, nb - 1)
    return ret + np.where(n < max_exact, n, large)


def _attention_bias(rel_bias, sink_logits, n_blocks):
    HB = BLOCK // 2
    t = np.arange(BLOCK)
    s_off = np.arange(3 * BLOCK) - BLOCK
    rel_band = s_off[None, :] - t[:, None]
    band_ok = np.abs(rel_band) <= WINDOW
    band_bucket = _t5_bucket_np(rel_band)
    q_pos = N_META + np.arange(n_blocks * BLOCK).reshape(n_blocks, BLOCK)
    meta_bucket = _t5_bucket_np(np.arange(N_META)[None, None, :] - q_pos[..., None])
    assert (meta_bucket[1:] == meta_bucket[1:2]).all()
    table = rel_bias.astype(F32)
    lookup = lambda bucket: jnp.einsum(
        '...k,kh->...h', jnp.asarray(np.eye(N_BUCKETS, dtype=np.float32)[bucket]), table,
        precision=lax.Precision.HIGHEST)
    neg = jnp.full((BLOCK, 3 * BLOCK, A_HEADS), NEG, F32)
    variants = []
    for var in range(3):
        ok = band_ok.copy()
        if var == 0:
            ok[:, :BLOCK] = False
        if var == 2:
            ok[:, 2 * BLOCK:] = False
        band = jnp.where(jnp.asarray(ok)[:, :, None], lookup(band_bucket), neg)
        meta = lookup(meta_bucket[0 if var == 0 else 1])
        sink = jnp.broadcast_to(sink_logits.astype(F32)[None, None, :], (BLOCK, 1, A_HEADS))
        padc = jnp.full((BLOCK, HB - N_META - 1, A_HEADS), NEG, F32)
        extra = jnp.concatenate([meta, sink, padc], axis=1)
        halves = [jnp.concatenate([band[:HB, :3 * BLOCK - HB], extra[:HB]], axis=1),
                  jnp.concatenate([extra[HB:], band[HB:, HB:]], axis=1)]
        full = jnp.transpose(jnp.stack(halves), (3, 0, 1, 2))
        full = full.reshape(A_KV_HEADS, A_GROUP, 2, HB, 3 * BLOCK)
        full = jnp.transpose(full, (0, 2, 1, 3, 4)).reshape(A_KV_HEADS, 2, A_GROUP * HB, 3 * BLOCK)
        variants.append(full)
    return jnp.stack(variants) * LOG2E


def _stage_d_kernel(x1_ref, hf_ref, hb_ref, at_ref, nm_ref, wo_ref, mg_ref, wga_ref, wgb_ref,
                    wupm_ref, wupa_ref, wout_ref, nf_ref, wgu_ref, wd_ref, out_ref, a_scr, hm_scr):
    streams = _row_streams(x1_ref.shape[0])

    def gates(rows):
        y_a = _dot(at_ref[rows], wupa_ref[...])
        x1 = x1_ref[rows]
        hn = _rms(x1, nm_ref[...]).astype(BF16)
        o = _dot(hn, wo_ref[...])
        for h in range(M_HEADS):
            sl = slice(h * M_DV, (h + 1) * M_DV)
            hs = hf_ref[rows, sl] + hb_ref[rows, sl]
            hm_scr[rows, sl] = (_rms(hs, mg_ref[:, sl]) * _sigmoid(o[:, sl])).astype(BF16)
        return x1, hn, y_a

    def mix(rows, x1, hn, y_a):
        mixed = _sigmoid(_dot(hn, wga_ref[...])) * _dot(hm_scr[rows], wupm_ref[...])
        mixed = mixed + _sigmoid(_dot(hn, wgb_ref[...])) * y_a
        return x1 + _dot(mixed.astype(BF16), wout_ref[...])

    pre = [gates(rows) for rows in streams]
    x2 = [mix(rows, *p) for rows, p in zip(streams, pre)]
    for rows, x in zip(streams, x2):
        out_ref[rows] = _ffn_residual(x, nf_ref, wgu_ref, wd_ref, a_scr.at[rows])


def _stage_d(x1, hf, hb, at, w, tile):
    n = x1.shape[0]
    row = lambda width: pl.BlockSpec((tile, width), lambda i: (i, 0))
    consts = (w['nm'], w['wo'], w['mg'], w['wga'], w['wgb'], w['wupm'], w['wupa'], w['wout'],
              w['nf2'], w['wgu2'], w['wd2'])
    return pl.pallas_call(
        _stage_d_kernel,
        grid=(n // tile,),
        in_specs=[row(D_MODEL), row(M_WIDTH), row(M_WIDTH), row(A_WIDTH)]
                 + [_const_spec(c.shape) for c in consts],
        out_specs=row(D_MODEL),
        out_shape=jax.ShapeDtypeStruct((n, D_MODEL), F32),
        scratch_shapes=[pltpu.VMEM((tile, D_FF), BF16), pltpu.VMEM((tile, M_WIDTH), BF16)],
        compiler_params=pltpu.CompilerParams(dimension_semantics=("parallel",),
                                             vmem_limit_bytes=VMEM_LIMIT_BYTES),
        name="stage_d",
    )(x1, hf, hb, at, *consts)


def _prep_weights(norm_ffn1, w_ffn1_gu, w_ffn1_down, norm_mix, w_in, b_gates, m_out_gain,
                  q_norm_gain, k_norm_gain, w_up_m, w_up_a, w_out, norm_ffn2, w_ffn2_gu,
                  w_ffn2_down):
    row = lambda v: v.reshape(1, -1).astype(F32)
    o_qm, o_om = 0, 3 * M_WIDTH
    o_gm = o_om + M_WIDTH
    o_qa = o_gm + 4 * M_HEADS
    o_ka = o_qa + A_WIDTH
    o_va = o_ka + A_KV_HEADS * A_DH
    o_ga = o_va + A_KV_HEADS * A_DH
    o_gb = o_ga + D_MODEL
    cols = lambda a, b: w_in[:, a:b]
    n_g = 4 * M_HEADS
    wkvg = jnp.pad(jnp.concatenate([cols(o_ka, o_ga), cols(o_gm, o_qa)], axis=1),
                   ((0, 0), (0, MXU_DIM - n_g)))
    gsum = np.kron(np.eye(LANES // A_DH, dtype=np.float32), np.ones((A_DH, A_DH), np.float32))
    return {
        'nf1': row(norm_ffn1), 'wgu1': w_ffn1_gu.astype(BF16), 'wd1': w_ffn1_down.astype(BF16),
        'nm': row(norm_mix),
        'wqkv': cols(o_qm, o_om).astype(BF16),
        'wo': cols(o_om, o_gm).astype(BF16),
        'bg': jnp.pad(row(b_gates), ((0, 0), (0, LANES - n_g))),
        'wqa': cols(o_qa, o_ka).astype(BF16),
        'wkvg': wkvg.astype(BF16),
        'qg': jnp.tile(row(q_norm_gain), (1, A_HEADS)),
        'kg': jnp.tile(row(k_norm_gain), (1, LANES // A_DH)),
        'gsum': jnp.asarray(np.concatenate([gsum, gsum], axis=0), BF16),
        'wga': cols(o_ga, o_gb).astype(BF16), 'wgb': cols(o_gb, o_gb + D_MODEL).astype(BF16),
        'mg': row(m_out_gain),
        'wupm': w_up_m.astype(BF16), 'wupa': w_up_a.astype(BF16), 'wout': w_out.astype(BF16),
        'nf2': row(norm_ffn2), 'wgu2': w_ffn2_gu.astype(BF16), 'wd2': w_ffn2_down.astype(BF16),
    }


def _encode(x, w, meta, bias_fn, tile, chunk):
    B, S, _ = x.shape
    meta_qkv, meta_gates, meta_kva = meta
    x1, qkv, qa, kva, *prep = _stage_a(x.reshape(B * S, D_MODEL), w, tile, chunk)
    hf, hb = _mlstm(qkv.reshape(B, S, -1), prep, meta_qkv, meta_gates, chunk)
    at = _attention(qa.reshape(B, S, -1), kva.reshape(B, S, -1), meta_kva, bias_fn(S // BLOCK))
    y = _stage_d(x1, hf.reshape(B * S, -1), hb.reshape(B * S, -1), at.reshape(B * S, -1), w, tile)
    return y.reshape(B, S, D_MODEL)


def _layer(x_groups, meta_tokens, rel_bias, norm_ffn1, w_ffn1_gu, w_ffn1_down, norm_mix, w_in,
           b_gates, m_out_gain, q_norm_gain, k_norm_gain, sink_logits, w_up_m, w_up_a, w_out,
           norm_ffn2, w_ffn2_gu, w_ffn2_down, tile=TOKEN_TILE, chunk=M_CHUNK):
    assert norm_ffn1.shape[0] == 1, "single layer"
    w = _prep_weights(norm_ffn1[0], w_ffn1_gu[0], w_ffn1_down[0], norm_mix[0], w_in[0], b_gates[0],
                      m_out_gain[0], q_norm_gain[0], k_norm_gain[0], w_up_m[0], w_up_a[0],
                      w_out[0], norm_ffn2[0], w_ffn2_gu[0], w_ffn2_down[0])
    meta_x = jnp.pad(meta_tokens.astype(F32), ((0, META_PAD - N_META), (0, 0)))
    _, m_qkv, _, m_kva, m_gates = _stage_a(meta_x, w, META_PAD)
    meta = (m_qkv, m_gates, m_kva[:N_META])
    bias_fn = functools.lru_cache(None)(
        lambda nb: _attention_bias(rel_bias, sink_logits[0], nb))
    return tuple(_encode(x, w, meta, bias_fn, tile, chunk) for x in x_groups)


def kernel(x_prompt, x_sample, meta_tokens, rel_bias, norm_ffn1, w_ffn1_gu, w_ffn1_down, norm_mix,
           w_in, b_gates, m_out_gain, q_norm_gain, k_norm_gain, sink_logits, w_up_m, w_up_a, w_out,
           norm_ffn2, w_ffn2_gu, w_ffn2_down):
    return _layer((x_prompt, x_sample), meta_tokens, rel_bias, norm_ffn1, w_ffn1_gu, w_ffn1_down,
                  norm_mix, w_in, b_gates, m_out_gain, q_norm_gain, k_norm_gain, sink_logits,
                  w_up_m, w_up_a, w_out, norm_ffn2, w_ffn2_gu, w_ffn2_down)
```
